```python
import jax
import jax.numpy as jnp
from jax import lax
import numpy as np

D_MODEL = 1024
BATCH = 16
SEQ = 4096
DEPTH = 4

GRID_W = 64
CTX_LEN = 256
N_MIXERS = 3
ALPHA = (2.0 * DEPTH) ** 0.25
BETA = (8.0 * DEPTH) ** -0.25
LN_EPS = 1e-5
RMS_EPS = 1e-6
ROPE_BASE = 10000.0

RET_HEADS = 4
RET_DK = D_MODEL // RET_HEADS
RET_DV = 2 * RET_DK
RET_QK = RET_HEADS * RET_DK
RET_VW = RET_HEADS * RET_DV
RET_CHUNK = 128

MLA_HEADS = 8
MLA_Q_LORA = 3 * D_MODEL // 8
MLA_KV_LORA = D_MODEL // 4
MLA_NOPE = 128
MLA_ROPE = 64
MLA_V = 128
Q_BLOCK = 128

CONV_WIDTH = 3

MOE_GROUPS = 4
MOE_EXPERTS_PER_GROUP = 8
MOE_EXPERTS = MOE_GROUPS * MOE_EXPERTS_PER_GROUP
MOE_TOPK = 2
MOE_HIDDEN = D_MODEL // 2
MOE_BLOCK = 256

N_RET = (DEPTH + 2) // 3
N_MLA = (DEPTH + 1) // 3
N_CONV = DEPTH // 3

kernel_name = 'hybrid_retention_mla_shortconv_hmoe_dit'


def layer_norm(x, g, b):
    xf = x.astype(jnp.float32)
    mu = jnp.mean(xf, -1, keepdims=True)
    var = jnp.mean(jnp.square(xf - mu), -1, keepdims=True)
    y = (xf - mu) * lax.rsqrt(var + LN_EPS)
    return (y * g.astype(jnp.float32) + b.astype(jnp.float32)).astype(x.dtype)


def rms_norm(x, g):
    xf = x.astype(jnp.float32)
    y = xf * lax.rsqrt(jnp.mean(xf * xf, -1, keepdims=True) + RMS_EPS)
    return (y * g.astype(jnp.float32)).astype(x.dtype)


def modulate(h, shift, scale):
    return h * (1.0 + scale) + shift


def axial_rope_tables(n_tokens, dim):
    n_rows = n_tokens // GRID_W
    rows = jnp.repeat(jnp.arange(n_rows, dtype=jnp.float32), GRID_W)
    cols = jnp.tile(jnp.arange(GRID_W, dtype=jnp.float32), n_rows)
    quarter = dim // 4
    inv_freq = ROPE_BASE ** (-jnp.arange(quarter, dtype=jnp.float32) / quarter)
    ang = jnp.concatenate([rows[:, None] * inv_freq, cols[:, None] * inv_freq], axis=-1)
    return jnp.cos(ang), jnp.sin(ang)


def apply_rope(x, cos, sin):
    half = x.shape[-1] // 2
    cos = cos.astype(x.dtype)
    sin = sin.astype(x.dtype)
    x1, x2 = x[..., :half], x[..., half:]
    return jnp.concatenate([x1 * cos - x2 * sin, x1 * sin + x2 * cos], axis=-1)


def retention_scan(k, v, log_gamma, state0, q=None):
    b, h, t, _ = k.shape
    L = RET_CHUNK
    nc = t // L
    chunks = lambda a: jnp.moveaxis(a.reshape(b, h, nc, L, a.shape[-1]), 2, 0)
    idx = jnp.arange(L, dtype=jnp.float32)
    lg = log_gamma[:, None]
    k_dec = jnp.exp((L - 1.0 - idx) * lg)[..., None]
    c_dec = jnp.exp(L * lg)[..., None]

    def update(state, kc, vc):
        return state * c_dec + jnp.einsum('bhjd,bhje->bhde', kc * k_dec, vc)

    if q is None:
        final, _ = lax.scan(lambda st, kv: (update(st, kv[0], kv[1]), None), state0, (chunks(k), chunks(v)))
        return None, final
    rel = idx[:, None] - idx[None, :]
    intra = jnp.where(rel >= 0, jnp.exp(jnp.maximum(rel, 0.0) * lg[:, :, None]), 0.0)
    q_dec = jnp.exp((idx + 1.0) * lg)[..., None]

    def step(state, qkv):
        qc, kc, vc = qkv
        scores = jnp.einsum('bhid,bhjd->bhij', qc, kc) * intra
        out = jnp.einsum('bhij,bhje->bhie', scores, vc) + jnp.einsum('bhid,bhde->bhie', qc * q_dec, state)
        return update(state, kc, vc), out

    final, outs = lax.scan(step, state0, (chunks(q), chunks(k), chunks(v)))
    return jnp.moveaxis(outs, 0, 2).reshape(b, h, t, v.shape[-1]), final


def retention_mixer(u_ctx, u_lat, w_in, decay_logit, gn_g, w_out, need_ctx):
    b, s, _ = u_lat.shape
    w_q, w_k, w_v, w_g = jnp.split(w_in, [RET_QK, 2 * RET_QK, 2 * RET_QK + RET_VW], axis=1)

    def heads(z, dh):
        return z.reshape(b, -1, RET_HEADS, dh).transpose(0, 2, 1, 3).astype(jnp.float32)

    cos, sin = axial_rope_tables(s, RET_DK)
    q_l = apply_rope(heads(u_lat @ w_q, RET_DK), cos, sin)
    k_l = apply_rope(heads(u_lat @ w_k, RET_DK), cos, sin) * RET_DK ** -0.5
    v_l = heads(u_lat @ w_v, RET_DV)
    k_c = heads(u_ctx @ w_k, RET_DK) * RET_DK ** -0.5
    v_c = heads(u_ctx @ w_v, RET_DV)
    q_c = heads(u_ctx @ w_q, RET_DK) if need_ctx else None
    log_gamma = jax.nn.log_sigmoid(decay_logit.astype(jnp.float32))
    zero = jnp.zeros((b, RET_HEADS, RET_DK, RET_DV), jnp.float32)
    rev = lambda a: None if a is None else a[:, :, ::-1]
    o_cf, s_cf = retention_scan(k_c, v_c, log_gamma[0], zero, q_c)
    o_cb, s_cb = retention_scan(rev(k_c), rev(v_c), log_gamma[1], zero, rev(q_c))
    o_lf, _ = retention_scan(k_l, v_l, log_gamma[0], s_cf, q_l)
    o_lb, _ = retention_scan(rev(k_l), rev(v_l), log_gamma[1], s_cb, rev(q_l))

    def finish(o, u):
        mu = jnp.mean(o, -1, keepdims=True)
        var = jnp.mean(jnp.square(o - mu), -1, keepdims=True)
        on = ((o - mu) * lax.rsqrt(var + LN_EPS)).transpose(0, 2, 1, 3).reshape(b, -1, RET_VW)
        on = on * gn_g.astype(jnp.float32)
        return (jax.nn.silu(u @ w_g) * on.astype(u.dtype)) @ w_out

    y_lat = finish(o_lf + rev(o_lb), u_lat)
    y_ctx = finish(o_cf + rev(o_cb), u_ctx) if need_ctx else None
    return y_ctx, y_lat


def mla_mixer(u_ctx, u_lat, w_down, q_norm, kv_norm, w_uq, w_ukv, w_out, need_ctx):
    b, s, _ = u_lat.shape
    w_dq, w_dkv, w_kr = jnp.split(w_down, [MLA_Q_LORA, MLA_Q_LORA + MLA_KV_LORA], axis=1)
    cos, sin = axial_rope_tables(s, MLA_ROPE)
    scale = (MLA_NOPE + MLA_ROPE) ** -0.5

    def heads(z, dh):
        return z.reshape(b, -1, MLA_HEADS, dh).transpose(0, 2, 1, 3)

    def project_q(u):
        q = heads(rms_norm(u @ w_dq, q_norm) @ w_uq, MLA_NOPE + MLA_ROPE)
        return q[..., :MLA_NOPE], q[..., MLA_NOPE:]

    def project_kv(u):
        kv = heads(rms_norm(u @ w_dkv, kv_norm) @ w_ukv, MLA_NOPE + MLA_V)
        return kv[..., :MLA_NOPE], u @ w_kr, kv[..., MLA_NOPE:]

    def attend(qn, qr, kn, kr, v):
        sc = jnp.einsum('bhqd,bhkd->bhqk', qn, kn) + jnp.einsum('bhqd,bkd->bhqk', qr, kr)
        p = jax.nn.softmax(sc.astype(jnp.float32) * scale, axis=-1).astype(v.dtype)
        return jnp.einsum('bhqk,bhkd->bhqd', p, v)

    def merge(o):
        return o.transpose(0, 2, 1, 3).reshape(b, -1, MLA_HEADS * MLA_V) @ w_out

    qn_l, qr_l = project_q(u_lat)
    qr_l = apply_rope(qr_l, cos, sin)
    kn_l, kr_l, v_l = project_kv(u_lat)
    kr_l = apply_rope(kr_l, cos, sin)
    kn_c, kr_c, v_c = project_kv(u_ctx)
    kn_all = jnp.concatenate([kn_c, kn_l], axis=2)
    kr_all = jnp.concatenate([kr_c, kr_l], axis=1)
    v_all = jnp.concatenate([v_c, v_l], axis=2)
    nb = s // Q_BLOCK
    blocks = lambda a: jnp.moveaxis(a.reshape(b, MLA_HEADS, nb, Q_BLOCK, a.shape[-1]), 2, 0)
    o = lax.map(lambda qq: attend(qq[0], qq[1], kn_all, kr_all, v_all), (blocks(qn_l), blocks(qr_l)))
    y_lat = merge(jnp.moveaxis(o, 0, 2).reshape(b, MLA_HEADS, s, MLA_V))
    if not need_ctx:
        return None, y_lat
    qn_c, qr_c = project_q(u_ctx)
    y_ctx = merge(attend(qn_c, qr_c, kn_c, kr_c, v_c))
    return y_ctx, y_lat


def short_conv(h, w, bias):
    y = lax.conv_general_dilated(h, w[:, None, :].astype(h.dtype), window_strides=(1,),
                                 padding=((CONV_WIDTH // 2, CONV_WIDTH // 2),),
                                 dimension_numbers=('NWC', 'WIO', 'NWC'), feature_group_count=h.shape[-1])
    return y + bias


def conv_mixer(u, w_in, w, bias, w_out):
    gate_b, gate_c, h = jnp.split(u @ w_in, 3, axis=-1)
    return (gate_b * short_conv(gate_c * h, w, bias)) @ w_out


def hier_moe(t, w_group, b_group, w_expert, b_expert, w1, w3, w2):
    n, d = t.shape
    g_prob = jax.nn.softmax((t @ w_group).astype(jnp.float32) + b_group.astype(jnp.float32), axis=-1)
    g_p, g_idx = lax.top_k(g_prob, 1)
    e_logits = ((t @ w_expert).astype(jnp.float32) + b_expert.astype(jnp.float32)).reshape(n, MOE_GROUPS, MOE_EXPERTS_PER_GROUP)
    e_logits = jnp.take_along_axis(e_logits, g_idx[:, :, None], axis=1)[:, 0]
    e_p, e_idx = lax.top_k(jax.nn.softmax(e_logits, axis=-1), MOE_TOPK)
    gate = g_p * e_p / jnp.sum(e_p, -1, keepdims=True)
    expert = g_idx * MOE_EXPERTS_PER_GROUP + e_idx
    nk = n * MOE_TOPK
    flat_e = expert.reshape(nk)
    flat_tok = jnp.repeat(jnp.arange(n, dtype=jnp.int32), MOE_TOPK)
    flat_w = gate.reshape(nk)
    order = jnp.argsort(flat_e)
    se, stok, sw = flat_e[order], flat_tok[order], flat_w[order]
    counts = jnp.bincount(flat_e, length=MOE_EXPERTS)
    starts = jnp.cumsum(counts) - counts
    padded = (counts + MOE_BLOCK - 1) // MOE_BLOCK * MOE_BLOCK
    pad_ends = jnp.cumsum(padded)
    pad_starts = pad_ends - padded
    dest = pad_starts[se] + jnp.arange(nk) - starts[se]
    n_blocks = -(-nk // MOE_BLOCK) + MOE_EXPERTS
    n_rows = n_blocks * MOE_BLOCK
    row_tok = jnp.full((n_rows,), n, jnp.int32).at[dest].set(stok)
    row_w = jnp.zeros((n_rows,), jnp.float32).at[dest].set(sw)
    block_e = jnp.minimum(jnp.searchsorted(pad_ends, jnp.arange(n_blocks) * MOE_BLOCK, side='right'), MOE_EXPERTS - 1)
    xs = jnp.concatenate([t, jnp.zeros((1, d), t.dtype)], axis=0)[row_tok].reshape(n_blocks, MOE_BLOCK, d)

    def expert_block(args):
        xb, e = args
        hdn = jax.nn.silu(xb @ w1[e]) * (xb @ w3[e])
        return hdn @ w2[e]

    ys = lax.map(expert_block, (xs, block_e)).reshape(n_rows, d)
    out = jax.ops.segment_sum(ys * row_w[:, None].astype(ys.dtype), row_tok, num_segments=n + 1)
    return out[:n]


def setup_inputs(seed: int = 0) -> dict:
    key = jax.random.key(seed)
    ks = jax.random.split(key, 32)
    nrm = lambda k, shape, sc: jax.random.normal(k, shape, jnp.float32) * sc
    D = D_MODEL
    base_logit = jnp.log(2.0 ** (5.0 + jnp.arange(RET_HEADS, dtype=jnp.float32)) - 1.0)
    return {
        'x': nrm(ks[0], (BATCH, SEQ, D), 1.0),
        'c': nrm(ks[1], (BATCH, D), 1.0),
        'ctx': nrm(ks[2], (BATCH, CTX_LEN, D), 1.0),
        'c_ctx': nrm(ks[3], (D,), 1.0),
        'ada_w': nrm(ks[4], (DEPTH, D, 6 * D), 0.5 * D ** -0.5),
        'ada_b': nrm(ks[5], (DEPTH, 6 * D), 0.02),
        'ln_g': 1.0 + nrm(ks[6], (DEPTH, 2, D), 0.02),
        'ln_b': nrm(ks[7], (DEPTH, 2, D), 0.02),
        'ret_w_in': nrm(ks[8], (N_RET, D, 2 * RET_QK + 2 * RET_VW), D ** -0.5),
        'ret_decay': base_logit + nrm(ks[9], (N_RET, 2, RET_HEADS), 0.1),
        'ret_gn_g': 1.0 + nrm(ks[10], (N_RET, RET_VW), 0.02),
        'ret_w_out': nrm(ks[11], (N_RET, RET_VW, D), BETA * RET_VW ** -0.5),
        'mla_w_down': nrm(ks[12], (N_MLA, D, MLA_Q_LORA + MLA_KV_LORA + MLA_ROPE), D ** -0.5),
        'mla_q_norm': 1.0 + nrm(ks[13], (N_MLA, MLA_Q_LORA), 0.02),
        'mla_kv_norm': 1.0 + nrm(ks[14], (N_MLA, MLA_KV_LORA), 0.02),
        'mla_w_uq': nrm(ks[15], (N_MLA, MLA_Q_LORA, MLA_HEADS * (MLA_NOPE + MLA_ROPE)), MLA_Q_LORA ** -0.5),
        'mla_w_ukv': nrm(ks[16], (N_MLA, MLA_KV_LORA, MLA_HEADS * (MLA_NOPE + MLA_V)), MLA_KV_LORA ** -0.5),
        'mla_w_out': nrm(ks[17], (N_MLA, MLA_HEADS * MLA_V, D), BETA * (MLA_HEADS * MLA_V) ** -0.5),
        'conv_w_in': nrm(ks[18], (N_CONV, D, 3 * D), D ** -0.5),
        'conv_w': nrm(ks[19], (N_CONV, CONV_WIDTH, D), CONV_WIDTH ** -0.5),
        'conv_b': nrm(ks[20], (N_CONV, D), 0.02),
        'conv_w_out': nrm(ks[21], (N_CONV, D, D), BETA * D ** -0.5),
        'moe_w_group': nrm(ks[22], (DEPTH, D, MOE_GROUPS), D ** -0.5),
        'moe_b_group': nrm(ks[23], (DEPTH, MOE_GROUPS), 0.01),
        'moe_w_expert': nrm(ks[24], (DEPTH, D, MOE_EXPERTS), D ** -0.5),
        'moe_b_expert': nrm(ks[25], (DEPTH, MOE_EXPERTS), 0.01),
        'moe_w1': nrm(ks[26], (DEPTH, MOE_EXPERTS, D, MOE_HIDDEN), D ** -0.5),
        'moe_w3': nrm(ks[27], (DEPTH, MOE_EXPERTS, D, MOE_HIDDEN), D ** -0.5),
        'moe_w2': nrm(ks[28], (DEPTH, MOE_EXPERTS, MOE_HIDDEN, D), BETA * MOE_HIDDEN ** -0.5),
    }


def reference(x, c, ctx, c_ctx, ada_w, ada_b, ln_g, ln_b, ret_w_in, ret_decay, ret_gn_g, ret_w_out,
              mla_w_down, mla_q_norm, mla_kv_norm, mla_w_uq, mla_w_ukv, mla_w_out,
              conv_w_in, conv_w, conv_b, conv_w_out,
              moe_w_group, moe_b_group, moe_w_expert, moe_b_expert, moe_w1, moe_w3, moe_w2):
    b, s, d = x.shape
    n_ctx = ctx.shape[1]
    h_lat, h_ctx = x, ctx
    act_lat = jax.nn.silu(c)[:, None, :]
    act_ctx = jax.nn.silu(c_ctx)[None, None, :]
    for i in range(DEPTH):
        need_ctx = i < DEPTH - 1
        kind, j = i % N_MIXERS, i // N_MIXERS
        m_lat = jnp.split(act_lat @ ada_w[i] + ada_b[i], 6, axis=-1)
        m_ctx = jnp.split(act_ctx @ ada_w[i] + ada_b[i], 6, axis=-1)
        u_lat = modulate(h_lat, m_lat[0], m_lat[1])
        u_ctx = modulate(h_ctx, m_ctx[0], m_ctx[1])
        if kind == 0:
            y_ctx, y_lat = retention_mixer(u_ctx, u_lat, ret_w_in[j], ret_decay[j], ret_gn_g[j], ret_w_out[j], need_ctx)
        elif kind == 1:
            y_ctx, y_lat = mla_mixer(u_ctx, u_lat, mla_w_down[j], mla_q_norm[j], mla_kv_norm[j],
                                     mla_w_uq[j], mla_w_ukv[j], mla_w_out[j], need_ctx)
        else:
            y_lat = conv_mixer(u_lat, conv_w_in[j], conv_w[j], conv_b[j], conv_w_out[j])
            y_ctx = conv_mixer(u_ctx, conv_w_in[j], conv_w[j], conv_b[j], conv_w_out[j]) if need_ctx else None
        h_lat = layer_norm(ALPHA * h_lat + m_lat[2] * y_lat, ln_g[i, 0], ln_b[i, 0])
        v_lat = modulate(h_lat, m_lat[3], m_lat[4]).reshape(b * s, d)
        if need_ctx:
            h_ctx = layer_norm(ALPHA * h_ctx + m_ctx[2] * y_ctx, ln_g[i, 0], ln_b[i, 0])
            v_ctx = modulate(h_ctx, m_ctx[3], m_ctx[4]).reshape(b * n_ctx, d)
            tokens = jnp.concatenate([v_ctx, v_lat], axis=0)
        else:
            tokens = v_lat
        f = hier_moe(tokens, moe_w_group[i], moe_b_group[i], moe_w_expert[i], moe_b_expert[i],
                     moe_w1[i], moe_w3[i], moe_w2[i])
        f_lat = f[f.shape[0] - b * s:].reshape(b, s, d)
        h_lat = layer_norm(ALPHA * h_lat + m_lat[5] * f_lat, ln_g[i, 1], ln_b[i, 1])
        if need_ctx:
            f_ctx = f[:b * n_ctx].reshape(b, n_ctx, d)
            h_ctx = layer_norm(ALPHA * h_ctx + m_ctx[5] * f_ctx, ln_g[i, 1], ln_b[i, 1])
    return h_lat
```

```python
import functools

import jax
import jax.numpy as jnp
from jax import lax
from jax.experimental import pallas as pl
from jax.experimental.pallas import tpu as pltpu

F32 = jnp.float32
BF16 = jnp.bfloat16

GRID_W = 64
LN_EPS = 1e-5
RMS_EPS = 1e-6
ROPE_BASE = 10000.0
N_MIXERS = 3
RET_HEADS = 4
MLA_HEADS = 8
MLA_NOPE = 128
MLA_ROPE = 64
MLA_V = 128
MLA_QK = 256
MOE_GROUPS = 4
MOE_EPG = 8
MOE_EXPERTS = MOE_GROUPS * MOE_EPG
MOE_TOPK = 2
ROUTER_W = 128

CHUNK = 256
TM = 512
TQ = 512
TOK_TILE = 512
EXP_BLOCK = 512
VMEM_LIMIT = 56 * 1024 * 1024


def _params(*sem):
    return pltpu.CompilerParams(dimension_semantics=sem, vmem_limit_bytes=VMEM_LIMIT)


def _const_spec(shape):
    nd = len(shape)
    return pl.BlockSpec(shape, lambda *_: (0,) * nd, pipeline_mode=pl.Buffered(1))


def _dot(a, b):
    return jnp.dot(a, b, preferred_element_type=F32)


def _dot_nt(a, b):
    return lax.dot_general(a, b, (((1,), (1,)), ((), ())), preferred_element_type=F32)


def _dot_tn(a, b):
    return lax.dot_general(a, b, (((0,), (0,)), ((), ())), preferred_element_type=F32)


def _silu(x):
    return x * jax.nn.sigmoid(x)


def _layer_norm(x, g, b):
    mu = jnp.mean(x, axis=-1, keepdims=True)
    xc = x - mu
    var = jnp.mean(xc * xc, axis=-1, keepdims=True)
    return xc * lax.rsqrt(var + LN_EPS) * g + b


def _rms_norm(x, g):
    return x * lax.rsqrt(jnp.mean(x * x, axis=-1, keepdims=True) + RMS_EPS) * g


def _ada_kernel(c_ref, w_ref, b_ref, o_ref):
    a = _silu(c_ref[...]).astype(BF16)
    o_ref[...] = _dot(a, w_ref[...].astype(BF16)) + b_ref[...]


def _ada(cc, ada_w, ada_b):
    depth, d, n6 = ada_w.shape
    rows = cc.shape[0]
    tn = 1536
    return pl.pallas_call(
        _ada_kernel,
        grid=(depth, n6 // tn),
        in_specs=[
            pl.BlockSpec((rows, d), lambda l, j: (0, 0)),
            pl.BlockSpec((None, d, tn), lambda l, j: (l, 0, j)),
            pl.BlockSpec((None, 1, tn), lambda l, j: (l, 0, j)),
        ],
        out_specs=pl.BlockSpec((None, rows, tn), lambda l, j: (l, 0, j)),
        out_shape=jax.ShapeDtypeStruct((depth, rows, n6), F32),
        compiler_params=_params("parallel", "parallel"),
        name="ada_mod",
    )(cc, ada_w, ada_b.reshape(depth, 1, n6))


class _Layout:
    def __init__(self, batch, seq, ctx_len):
        assert ctx_len == CHUNK and seq % TM == 0 and seq % CHUNK == 0
        self.batch, self.seq, self.ctx = batch, seq, ctx_len
        self.n_ctx = batch * ctx_len
        self.n = self.n_ctx + batch * seq
        assert self.n_ctx % TM == 0 and self.n_ctx % seq == 0 and self.n % TOK_TILE == 0

    def mod_row(self, tile, rows_per_tile):
        ctx_tiles = self.n_ctx // rows_per_tile
        per_batch = self.seq // rows_per_tile
        return jnp.where(tile < ctx_tiles, self.batch, (tile - ctx_tiles) // per_batch)

    def pos_block(self, tile, rows_per_tile):
        ctx_tiles = self.n_ctx // rows_per_tile
        per_batch = self.seq // rows_per_tile
        return jnp.where(tile < ctx_tiles, 0, 1 + (tile - ctx_tiles) % per_batch)


def _rope_angles(seq, dim):
    n_rows = seq // GRID_W
    rows = jnp.repeat(jnp.arange(n_rows, dtype=F32), GRID_W)
    cols = jnp.tile(jnp.arange(GRID_W, dtype=F32), n_rows)
    quarter = dim // 4
    inv_freq = ROPE_BASE ** (-jnp.arange(quarter, dtype=F32) / quarter)
    return jnp.concatenate([rows[:, None] * inv_freq, cols[:, None] * inv_freq], axis=-1)


def _with_identity_block(table, fill, rows):
    ident = jnp.full((rows, table.shape[1]), fill, F32)
    return jnp.concatenate([ident, table], axis=0)


def _proj_ret_kernel(h_ref, mod_ref, cos_ref, sin_ref, kdec_ref, wq_ref, wk_ref, wv_ref, wg_ref,
                     q_ref, k_ref, kf_ref, kb_ref, v_ref, g_ref, *, dk):
    u = (h_ref[...] * (1.0 + mod_ref[1:2, :]) + mod_ref[0:1, :]).astype(BF16)
    cos = cos_ref[...]
    sin = sin_ref[...]
    half = dk // 2
    yq = _dot(u, wq_ref[...])
    for hd in range(RET_HEADS):
        x1 = yq[:, hd * dk:hd * dk + half]
        x2 = yq[:, hd * dk + half:(hd + 1) * dk]
        q_ref[:, hd * dk:hd * dk + half] = (x1 * cos - x2 * sin).astype(BF16)
        q_ref[:, hd * dk + half:(hd + 1) * dk] = (x1 * sin + x2 * cos).astype(BF16)
    yk = _dot(u, wk_ref[...])
    k_scale = dk ** -0.5
    for hd in range(RET_HEADS):
        x1 = yk[:, hd * dk:hd * dk + half]
        x2 = yk[:, hd * dk + half:(hd + 1) * dk]
        df = kdec_ref[:, hd:hd + 1]
        db = kdec_ref[:, RET_HEADS + hd:RET_HEADS + hd + 1]
        for part, o in ((0, (x1 * cos - x2 * sin) * k_scale), (1, (x1 * sin + x2 * cos) * k_scale)):
            sl = slice(hd * dk + part * half, hd * dk + (part + 1) * half)
            k_ref[:, sl] = o.astype(BF16)
            kf_ref[:, sl] = (o * df).astype(BF16)
            kb_ref[:, sl] = (o * db).astype(BF16)
    v_ref[...] = _dot(u, wv_ref[...]).astype(BF16)
    g_ref[...] = _silu(_dot(u, wg_ref[...]))


def _ret_scan_kernel(cdec_ref, q_ref, k_ref, kd_ref, v_ref, intra_ref, qdec_ref, *rest, dk, dv, direction):
    if direction == 0:
        o_ref, state_ref = rest
    else:
        of_ref, g_ref, gn_ref, o_ref, state_ref = rest

    @pl.when(pl.program_id(1) == 0)
    def _():
        state_ref[...] = jnp.zeros_like(state_ref)

    for hd in range(RET_HEADS):
        q = q_ref[:, hd * dk:(hd + 1) * dk]
        k = k_ref[:, hd * dk:(hd + 1) * dk]
        kd = kd_ref[:, hd * dk:(hd + 1) * dk]
        v = v_ref[:, hd * dv:(hd + 1) * dv]
        p = (_dot_nt(q, k) * intra_ref[hd]).astype(BF16)
        st = state_ref[hd]
        col = direction * RET_HEADS + hd
        o = _dot(p, v) + qdec_ref[:, col:col + 1] * _dot(q, st.astype(BF16))
        state_ref[hd] = st * cdec_ref[col] + _dot_tn(kd, v)
        if direction == 0:
            o_ref[:, hd * dv:(hd + 1) * dv] = o
        else:
            o = o + of_ref[:, hd * dv:(hd + 1) * dv]
            mu = jnp.mean(o, axis=-1, keepdims=True)
            oc = o - mu
            var = jnp.mean(oc * oc, axis=-1, keepdims=True)
            on = oc * lax.rsqrt(var + LN_EPS) * gn_ref[:, hd * dv:(hd + 1) * dv]
            o_ref[:, hd * dv:(hd + 1) * dv] = (g_ref[:, hd * dv:(hd + 1) * dv] * on).astype(BF16)


def _retention(lay, h, mod, w_in, decay_logit, gn_g, d_model):
    n, b, nc = lay.n, lay.batch, lay.seq // CHUNK
    dk = d_model // RET_HEADS
    dv = 2 * dk
    qk, vw = RET_HEADS * dk, RET_HEADS * dv
    w_in = w_in.astype(BF16)
    wq, wk, wv, wg = w_in[:, :qk], w_in[:, qk:2 * qk], w_in[:, 2 * qk:2 * qk + vw], w_in[:, 2 * qk + vw:]

    ang = _rope_angles(lay.seq, dk)
    cos_t = _with_identity_block(jnp.cos(ang), 1.0, TM)
    sin_t = _with_identity_block(jnp.sin(ang), 0.0, TM)

    lg = jax.nn.log_sigmoid(decay_logit.astype(F32))
    idx = jnp.arange(CHUNK, dtype=F32)
    k_pow = jnp.stack([CHUNK - 1.0 - idx, idx])
    q_pow = jnp.stack([idx + 1.0, CHUNK - idx])
    kdec = jnp.exp(k_pow[:, :, None] * lg[:, None, :])
    qdec = jnp.exp(q_pow[:, :, None] * lg[:, None, :])
    kdec = jnp.moveaxis(kdec, 0, 1).reshape(CHUNK, 2 * RET_HEADS)
    qdec = jnp.moveaxis(qdec, 0, 1).reshape(CHUNK, 2 * RET_HEADS)
    cdec = jnp.exp(CHUNK * lg).reshape(2 * RET_HEADS)
    rel = idx[:, None] - idx[None, :]
    rel = jnp.stack([rel, -rel])
    intra = jnp.where(rel[:, None] >= 0, jnp.exp(jnp.maximum(rel[:, None], 0.0) * lg[:, :, None, None]), 0.0)
    kdec_tm = jnp.tile(kdec, (TM // CHUNK, 1))

    n_tiles = n // TM
    row = lambda i: (i, 0)
    q, k, kf, kb, v, g = pl.pallas_call(
        functools.partial(_proj_ret_kernel, dk=dk),
        grid=(n_tiles,),
        in_specs=[
            pl.BlockSpec((TM, d_model), row),
            pl.BlockSpec((None, 6, d_model), lambda i: (lay.mod_row(i, TM), 0, 0)),
            pl.BlockSpec((TM, dk // 2), lambda i: (lay.pos_block(i, TM), 0)),
            pl.BlockSpec((TM, dk // 2), lambda i: (lay.pos_block(i, TM), 0)),
            _const_spec((TM, 2 * RET_HEADS)),
            _const_spec((d_model, qk)), _const_spec((d_model, qk)),
            _const_spec((d_model, vw)), _const_spec((d_model, vw)),
        ],
        out_specs=[pl.BlockSpec((TM, qk), row)] * 4 + [pl.BlockSpec((TM, vw), row)] * 2,
        out_shape=[jax.ShapeDtypeStruct((n, qk), BF16)] * 4
        + [jax.ShapeDtypeStruct((n, vw), BF16), jax.ShapeDtypeStruct((n, vw), F32)],
        compiler_params=_params("parallel"),
        name="ret_proj",
    )(h, mod, cos_t, sin_t, kdec_tm, wq, wk, wv, wg)

    def chunk_fwd(bi, c):
        return (jnp.where(c == 0, bi, b + bi * nc + c - 1), 0)

    def chunk_bwd(bi, c):
        return (jnp.where(c == 0, bi, b + bi * nc + nc - c), 0)

    def scan(direction, chunk_map, kd, extra_in, extra_specs, out_dtype, name):
        return pl.pallas_call(
            functools.partial(_ret_scan_kernel, dk=dk, dv=dv, direction=direction),
            grid=(b, nc + 1),
            in_specs=[
                pl.BlockSpec(memory_space=pltpu.SMEM),
                pl.BlockSpec((CHUNK, qk), chunk_map),
                pl.BlockSpec((CHUNK, qk), chunk_map),
                pl.BlockSpec((CHUNK, qk), chunk_map),
                pl.BlockSpec((CHUNK, vw), chunk_map),
                _const_spec((RET_HEADS, CHUNK, CHUNK)),
                _const_spec((CHUNK, 2 * RET_HEADS)),
            ] + extra_specs,
            out_specs=pl.BlockSpec((CHUNK, vw), chunk_map),
            out_shape=jax.ShapeDtypeStruct((n, vw), out_dtype),
            scratch_shapes=[pltpu.VMEM((RET_HEADS, dk, dv), F32)],
            compiler_params=_params("parallel", "arbitrary"),
            name=name,
        )(cdec, q, k, kd, v, intra[direction], qdec, *extra_in)

    o_f = scan(0, chunk_fwd, kf, [], [], F32, "ret_scan_fwd")
    return scan(1, chunk_bwd, kb, [o_f, g, gn_g.reshape(1, vw).astype(F32)],
                [pl.BlockSpec((CHUNK, vw), chunk_bwd), pl.BlockSpec((CHUNK, vw), chunk_bwd),
                 _const_spec((1, vw))], BF16, "ret_scan_bwd")


def _proj_mla_kernel(h_ref, mod_ref, ct_ref, st_ref, wdq_ref, wdkv_ref, wkr_ref, qn_ref, kvn_ref,
                     wuq_ref, wukn_ref, wuv_ref, q_ref, k_ref, v_ref):
    u = (h_ref[...] * (1.0 + mod_ref[1:2, :]) + mod_ref[0:1, :]).astype(BF16)
    ct = ct_ref[...]
    st = st_ref[...]
    half = MLA_QK // 2

    def rope(x):
        return x * ct + pltpu.roll(x, half // 2, 1) * st

    cq = _rms_norm(_dot(u, wdq_ref[...]), qn_ref[...]).astype(BF16)
    yq = _dot(cq, wuq_ref[...])
    ckv = _rms_norm(_dot(u, wdkv_ref[...]), kvn_ref[...]).astype(BF16)
    kn = _dot(ckv, wukn_ref[...])
    kr = rope(_dot(u, wkr_ref[...])).astype(BF16)
    for hd in range(MLA_HEADS):
        q_ref[:, hd * MLA_QK:hd * MLA_QK + half] = yq[:, hd * MLA_QK:hd * MLA_QK + half].astype(BF16)
        q_ref[:, hd * MLA_QK + half:(hd + 1) * MLA_QK] = rope(
            yq[:, hd * MLA_QK + half:(hd + 1) * MLA_QK]).astype(BF16)
        k_ref[:, hd * MLA_QK:hd * MLA_QK + half] = kn[:, hd * half:(hd + 1) * half].astype(BF16)
        k_ref[:, hd * MLA_QK + half:(hd + 1) * MLA_QK] = kr
    v_ref[...] = _dot(ckv, wuv_ref[...]).astype(BF16)


def _attn_kernel(q_ref, *rest, scale, n_kv):
    kv_refs, o_ref = rest[:2 * n_kv], rest[-1]
    q = q_ref[...]
    scores = [_dot_nt(q, kv_refs[2 * j][...]) * scale for j in range(n_kv)]
    m = scores[0].max(axis=-1, keepdims=True)
    for s in scores[1:]:
        m = jnp.maximum(m, s.max(axis=-1, keepdims=True))
    acc = None
    den = None
    for j, s in enumerate(scores):
        p = jnp.exp(s - m)
        l = p.sum(axis=-1, keepdims=True)
        pv = _dot(p.astype(BF16), kv_refs[2 * j + 1][...])
        acc = pv if acc is None else acc + pv
        den = l if den is None else den + l
    o_ref[...] = (acc / den).astype(BF16)


def _pad_rope_cols(w):
    z = jnp.zeros((w.shape[0], MLA_ROPE // 2), w.dtype)
    return jnp.concatenate([w[:, :MLA_ROPE // 2], z, w[:, MLA_ROPE // 2:], z], axis=1)


def _mla(lay, h, mod, w_down, q_norm, kv_norm, w_uq, w_ukv, d_model):
    n, b, s = lay.n, lay.batch, lay.seq
    q_lora, kv_lora = q_norm.shape[0], kv_norm.shape[0]
    w_dq = w_down[:, :q_lora].astype(BF16)
    w_dkv = w_down[:, q_lora:q_lora + kv_lora].astype(BF16)
    w_kr = _pad_rope_cols(w_down[:, q_lora + kv_lora:]).astype(BF16)
    w_uq = w_uq.reshape(q_lora, MLA_HEADS, MLA_NOPE + MLA_ROPE)
    w_uq = jnp.concatenate(
        [w_uq[:, :, :MLA_NOPE], jax.vmap(_pad_rope_cols, 1, 1)(w_uq[:, :, MLA_NOPE:])], axis=2)
    w_uq = w_uq.reshape(q_lora, MLA_HEADS * MLA_QK).astype(BF16)
    w_ukv = w_ukv.reshape(kv_lora, MLA_HEADS, MLA_NOPE + MLA_V)
    w_ukn = w_ukv[:, :, :MLA_NOPE].reshape(kv_lora, MLA_HEADS * MLA_NOPE).astype(BF16)
    w_uv = w_ukv[:, :, MLA_NOPE:].reshape(kv_lora, MLA_HEADS * MLA_V).astype(BF16)

    ang = _rope_angles(s, MLA_ROPE)
    z = jnp.zeros_like(ang)
    ct = _with_identity_block(jnp.concatenate([jnp.cos(ang), z, jnp.cos(ang), z], axis=1), 1.0, TM)
    st = _with_identity_block(jnp.concatenate([-jnp.sin(ang), z, jnp.sin(ang), z], axis=1), 0.0, TM)

    row = lambda i: (i, 0)
    qw, vw = MLA_HEADS * MLA_QK, MLA_HEADS * MLA_V
    q, k, v = pl.pallas_call(
        _proj_mla_kernel,
        grid=(n // TM,),
        in_specs=[
            pl.BlockSpec((TM, d_model), row),
            pl.BlockSpec((None, 6, d_model), lambda i: (lay.mod_row(i, TM), 0, 0)),
            pl.BlockSpec((TM, MLA_QK // 2), lambda i: (lay.pos_block(i, TM), 0)),
            pl.BlockSpec((TM, MLA_QK // 2), lambda i: (lay.pos_block(i, TM), 0)),
            _const_spec((d_model, q_lora)), _const_spec((d_model, kv_lora)), _const_spec((d_model, MLA_QK // 2)),
            _const_spec((1, q_lora)), _const_spec((1, kv_lora)),
            _const_spec((q_lora, qw)), _const_spec((kv_lora, vw)), _const_spec((kv_lora, vw)),
        ],
        out_specs=[pl.BlockSpec((TM, qw), row), pl.BlockSpec((TM, qw), row), pl.BlockSpec((TM, vw), row)],
        out_shape=[jax.ShapeDtypeStruct((n, qw), BF16), jax.ShapeDtypeStruct((n, qw), BF16),
                   jax.ShapeDtypeStruct((n, vw), BF16)],
        compiler_params=_params("parallel"),
        name="mla_proj",
    )(h, mod, ct, st, w_dq, w_dkv, w_kr, q_norm.reshape(1, -1).astype(F32), kv_norm.reshape(1, -1).astype(F32),
      w_uq, w_ukn, w_uv)

    scale = (MLA_NOPE + MLA_ROPE) ** -0.5
    ctx_q = lambda bi, hd: (bi, hd)
    att = pl.pallas_call(
        functools.partial(_attn_kernel, scale=scale, n_kv=1),
        grid=(b, MLA_HEADS),
        in_specs=[pl.BlockSpec((CHUNK, MLA_QK), ctx_q), pl.BlockSpec((CHUNK, MLA_QK), ctx_q),
                  pl.BlockSpec((CHUNK, MLA_V), ctx_q)],
        out_specs=pl.BlockSpec((CHUNK, MLA_V), ctx_q),
        out_shape=jax.ShapeDtypeStruct((n, vw), BF16),
        compiler_params=_params("parallel", "parallel"),
        name="mla_attn_ctx",
    )(q, k, v)

    lat_blk = lay.n_ctx // s
    q_tiles = s // TQ
    lat_q = lambda bi, hd, t: (lay.n_ctx // TQ + bi * q_tiles + t, hd)
    ctx_kv = lambda bi, hd, t: (bi, hd)
    lat_kv = lambda bi, hd, t: (lat_blk + bi, hd)
    return pl.pallas_call(
        functools.partial(_attn_kernel, scale=scale, n_kv=2),
        grid=(b, MLA_HEADS, q_tiles),
        in_specs=[pl.BlockSpec((TQ, MLA_QK), lat_q),
                  pl.BlockSpec((CHUNK, MLA_QK), ctx_kv), pl.BlockSpec((CHUNK, MLA_V), ctx_kv),
                  pl.BlockSpec((s, MLA_QK), lat_kv), pl.BlockSpec((s, MLA_V), lat_kv),
                  pl.BlockSpec(memory_space=pl.ANY)],
        out_specs=pl.BlockSpec((TQ, MLA_V), lat_q),
        out_shape=jax.ShapeDtypeStruct((n, vw), BF16),
        input_output_aliases={5: 0},
        compiler_params=_params("parallel", "parallel", "arbitrary"),
        name="mla_attn_lat",
    )(q, k, v, k, v, att)


def _post_tail(y, h, mod_ref, ln_ref, wr_ref, br_ref, h_out, t_out, lg_out, alpha):
    h1 = _layer_norm(alpha * h + mod_ref[2:3, :] * y, ln_ref[0:1, :], ln_ref[1:2, :])
    t = h1 * (1.0 + mod_ref[4:5, :]) + mod_ref[3:4, :]
    h_out[...] = h1
    t_out[...] = t
    lg_out[...] = _dot(t.astype(BF16), wr_ref[...]) + br_ref[...]


def _post_kernel(a_ref, h_ref, mod_ref, w_ref, ln_ref, wr_ref, br_ref, h_out, t_out, lg_out, *, alpha):
    y = _dot(a_ref[...], w_ref[...])
    _post_tail(y, h_ref[...], mod_ref, ln_ref, wr_ref, br_ref, h_out, t_out, lg_out, alpha)


def _post(lay, a, h, mod, w_out, ln, w_router, b_router, alpha):
    n, d = h.shape
    ka = a.shape[1]
    row = lambda i: (i, 0)
    return pl.pallas_call(
        functools.partial(_post_kernel, alpha=alpha),
        grid=(n // TM,),
        in_specs=[
            pl.BlockSpec((TM, ka), row),
            pl.BlockSpec((TM, d), row),
            pl.BlockSpec((None, 6, d), lambda i: (lay.mod_row(i, TM), 0, 0)),
            _const_spec((ka, d)), _const_spec((2, d)), _const_spec((d, ROUTER_W)), _const_spec((1, ROUTER_W)),
        ],
        out_specs=[pl.BlockSpec((TM, d), row), pl.BlockSpec((TM, d), row), pl.BlockSpec((TM, ROUTER_W), row)],
        out_shape=[jax.ShapeDtypeStruct((n, d), F32), jax.ShapeDtypeStruct((n, d), F32),
                   jax.ShapeDtypeStruct((n, ROUTER_W), F32)],
        compiler_params=_params("parallel"),
        name="post_mixer",
    )(a, h, mod, w_out.astype(BF16), ln, w_router, b_router)


HALO = 8


def _conv_kernel(h_ref, hp_ref, hn_ref, mod_ref, win_ref, cw_ref, cb_ref, w_ref, ln_ref, wr_ref, br_ref,
                 h_out, t_out, lg_out, *, alpha, tiles_per_seq, ctx_tiles):
    i = pl.program_id(0)
    d = h_ref.shape[1]
    h = h_ref[...]
    hx = jnp.concatenate([hp_ref[...], h, hn_ref[...]], axis=0)
    u = (hx * (1.0 + mod_ref[1:2, :]) + mod_ref[0:1, :]).astype(BF16)
    y = _dot(u, win_ref[...])
    gate_b = y[HALO:HALO + CHUNK, :d]
    zx = y[:, d:2 * d] * y[:, 2 * d:]
    z = zx[HALO:HALO + CHUNK]
    pos = jnp.where(i < ctx_tiles, 0, (i - ctx_tiles) % tiles_per_seq)
    last = jnp.where(i < ctx_tiles, 0, tiles_per_seq - 1)
    z_before = jnp.where(pos == 0, 0.0, zx[HALO - 1:HALO])
    z_after = jnp.where(pos == last, 0.0, zx[HALO + CHUNK:HALO + CHUNK + 1])
    r = lax.broadcasted_iota(jnp.int32, (CHUNK, 1), 0)
    z_prev = jnp.where(r == 0, z_before, pltpu.roll(z, 1, 0))
    z_next = jnp.where(r == CHUNK - 1, z_after, pltpu.roll(z, CHUNK - 1, 0))
    conv = cw_ref[0:1, :] * z_prev + cw_ref[1:2, :] * z + cw_ref[2:3, :] * z_next + cb_ref[...]
    yo = _dot((gate_b * conv).astype(BF16), w_ref[...])
    _post_tail(yo, h, mod_ref, ln_ref, wr_ref, br_ref, h_out, t_out, lg_out, alpha)


def _conv_layer(lay, h, mod, w_in, cw, cb, w_out, ln, w_router, b_router, alpha):
    n, d = h.shape
    n_tiles = n // CHUNK
    per = CHUNK // HALO
    row = lambda i: (i, 0)
    return pl.pallas_call(
        functools.partial(_conv_kernel, alpha=alpha, tiles_per_seq=lay.seq // CHUNK, ctx_tiles=lay.n_ctx // CHUNK),
        grid=(n_tiles,),
        in_specs=[
            pl.BlockSpec((CHUNK, d), row),
            pl.BlockSpec((HALO, d), lambda i: (jnp.maximum(i * per - 1, 0), 0)),
            pl.BlockSpec((HALO, d), lambda i: (jnp.minimum((i + 1) * per, n_tiles * per - 1), 0)),
            pl.BlockSpec((None, 6, d), lambda i: (lay.mod_row(i, CHUNK), 0, 0)),
            _const_spec((d, 3 * d)), _const_spec((3, d)), _const_spec((1, d)), _const_spec((d, d)),
            _const_spec((2, d)), _const_spec((d, ROUTER_W)), _const_spec((1, ROUTER_W)),
        ],
        out_specs=[pl.BlockSpec((CHUNK, d), row), pl.BlockSpec((CHUNK, d), row),
                   pl.BlockSpec((CHUNK, ROUTER_W), row)],
        out_shape=[jax.ShapeDtypeStruct((n, d), F32), jax.ShapeDtypeStruct((n, d), F32),
                   jax.ShapeDtypeStruct((n, ROUTER_W), F32)],
        compiler_params=_params("parallel"),
        name="conv_layer",
    )(h, h, h, mod, w_in.astype(BF16), cw.astype(F32), cb.reshape(1, d).astype(F32), w_out.astype(BF16),
      ln, w_router, b_router)


def _route(logits):
    n = logits.shape[0]
    g_prob = jax.nn.softmax(logits[:, :MOE_GROUPS], axis=-1)
    g_p, g_idx = lax.top_k(g_prob, 1)
    e_logits = logits[:, MOE_GROUPS:MOE_GROUPS + MOE_EXPERTS].reshape(n, MOE_GROUPS, MOE_EPG)
    e_logits = jnp.take_along_axis(e_logits, g_idx[:, :, None], axis=1)[:, 0]
    e_p, e_idx = lax.top_k(jax.nn.softmax(e_logits, axis=-1), MOE_TOPK)
    gate = g_p * e_p / jnp.sum(e_p, -1, keepdims=True)
    return g_idx * MOE_EPG + e_idx, gate


def _plan(expert, n_blocks):
    n = expert.shape[0]
    flat_e = expert.reshape(-1)
    onehot = (flat_e[:, None] == jnp.arange(MOE_EXPERTS, dtype=flat_e.dtype)[None, :]).astype(jnp.int32)
    csum = jnp.cumsum(onehot, axis=0)
    rank = jnp.take_along_axis(csum, flat_e[:, None], axis=1)[:, 0] - 1
    counts = csum[-1]
    padded = (counts + EXP_BLOCK - 1) // EXP_BLOCK * EXP_BLOCK
    pad_ends = jnp.cumsum(padded)
    pad_starts = pad_ends - padded
    dest = (pad_starts[flat_e] + rank).astype(jnp.int32)
    block_e = jnp.minimum(jnp.searchsorted(pad_ends, jnp.arange(n_blocks, dtype=jnp.int32) * EXP_BLOCK,
                                           side='right'), MOE_EXPERTS - 1).astype(jnp.int32)
    n_valid = (pad_ends[-1] // EXP_BLOCK).astype(jnp.int32).reshape(1)
    dest = dest.reshape(n // TOK_TILE, TOK_TILE, MOE_TOPK).transpose(0, 2, 1).reshape(-1)
    return dest, block_e, n_valid


def _row_copy(src, s, dst, d, sem):
    return pltpu.make_async_copy(src.at[pl.ds(s, 1)], dst.at[pl.ds(d, 1)], sem)


def _load_indices(idx_hbm, idx_smem, sem):
    n_idx = idx_smem.shape[0]
    cp = pltpu.make_async_copy(idx_hbm.at[pl.ds(pl.program_id(0) * n_idx, n_idx)], idx_smem, sem)
    cp.start()
    cp.wait()


def _dispatch_kernel(idx_hbm, t_ref, xs_in, xs_out, idx_smem, sem_idx, sem):
    del xs_in
    _load_indices(idx_hbm, idx_smem, sem_idx)
    for kk in range(MOE_TOPK):
        def issue(r, c, kk=kk):
            _row_copy(t_ref, r, xs_out, idx_smem[kk * TOK_TILE + r], sem).start()
            return c
        lax.fori_loop(0, TOK_TILE, issue, 0, unroll=8)

    def drain(j, c):
        _row_copy(t_ref, 0, xs_out, 0, sem).wait()
        return c
    lax.fori_loop(0, MOE_TOPK * TOK_TILE, drain, 0, unroll=8)


def _expert_kernel(be_ref, nv_ref, x_ref, w1_ref, w3_ref, w2_ref, o_ref):
    del be_ref
    valid = pl.program_id(0) < nv_ref[0]

    @pl.when(valid)
    def _():
        x = x_ref[...].astype(BF16)
        hdn = _silu(_dot(x, w1_ref[...])) * _dot(x, w3_ref[...])
        o_ref[...] = _dot(hdn.astype(BF16), w2_ref[...])

    @pl.when(jnp.logical_not(valid))
    def _():
        o_ref[...] = jnp.zeros_like(o_ref)


def _combine_kernel(idx_hbm, ys_hbm, gate_ref, h_ref, mod_ref, ln_ref, o_ref, idx_smem, buf, sem_idx, sem,
                    *, alpha):
    _load_indices(idx_hbm, idx_smem, sem_idx)

    def issue(j, c):
        _row_copy(ys_hbm, idx_smem[j], buf, j, sem).start()
        return c
    lax.fori_loop(0, MOE_TOPK * TOK_TILE, issue, 0, unroll=8)

    def drain(j, c):
        _row_copy(ys_hbm, 0, buf, 0, sem).wait()
        return c
    lax.fori_loop(0, MOE_TOPK * TOK_TILE, drain, 0, unroll=8)

    f = gate_ref[:, 0:1] * buf[0:TOK_TILE, :] + gate_ref[:, 1:2] * buf[TOK_TILE:2 * TOK_TILE, :]
    o_ref[...] = _layer_norm(alpha * h_ref[...] + mod_ref[5:6, :] * f, ln_ref[0:1, :], ln_ref[1:2, :])


def _moe(lay, t, logits, h, mod, ln, w1, w3, w2, alpha):
    n, d = t.shape
    hid = w1.shape[2]
    nk = n * MOE_TOPK
    n_blocks = -(-nk // EXP_BLOCK) + MOE_EXPERTS
    n_rows = n_blocks * EXP_BLOCK
    expert, gate = _route(logits)
    dest, block_e, n_valid = _plan(expert, n_blocks)
    n_tiles = n // TOK_TILE
    n_idx = MOE_TOPK * TOK_TILE
    row = lambda i: (i, 0)

    xs = pl.pallas_call(
        _dispatch_kernel,
        grid=(n_tiles,),
        in_specs=[pl.BlockSpec(memory_space=pl.ANY), pl.BlockSpec((TOK_TILE, d), row),
                  pl.BlockSpec(memory_space=pl.ANY)],
        out_specs=pl.BlockSpec(memory_space=pl.ANY),
        out_shape=jax.ShapeDtypeStruct((n_rows, d), F32),
        scratch_shapes=[pltpu.SMEM((n_idx,), jnp.int32), pltpu.SemaphoreType.DMA, pltpu.SemaphoreType.DMA],
        input_output_aliases={2: 0},
        compiler_params=_params("arbitrary"),
        name="moe_dispatch",
    )(dest, t, jnp.zeros((n_rows, d), F32))

    blk = lambda i, be, nv: (jnp.minimum(i, nv[0] - 1), 0)
    ys = pl.pallas_call(
        _expert_kernel,
        grid_spec=pltpu.PrefetchScalarGridSpec(
            num_scalar_prefetch=2,
            grid=(n_blocks,),
            in_specs=[
                pl.BlockSpec((EXP_BLOCK, d), blk),
                pl.BlockSpec((None, d, hid), lambda i, be, nv: (be[i], 0, 0)),
                pl.BlockSpec((None, d, hid), lambda i, be, nv: (be[i], 0, 0)),
                pl.BlockSpec((None, hid, d), lambda i, be, nv: (be[i], 0, 0)),
            ],
            out_specs=pl.BlockSpec((EXP_BLOCK, d), lambda i, be, nv: (i, 0)),
        ),
        out_shape=jax.ShapeDtypeStruct((n_rows, d), F32),
        compiler_params=_params("arbitrary"),
        name="moe_experts",
    )(block_e, n_valid, xs, w1.astype(BF16), w3.astype(BF16), w2.astype(BF16))

    return pl.pallas_call(
        functools.partial(_combine_kernel, alpha=alpha),
        grid=(n_tiles,),
        in_specs=[pl.BlockSpec(memory_space=pl.ANY), pl.BlockSpec(memory_space=pl.ANY),
                  pl.BlockSpec((TOK_TILE, MOE_TOPK), row), pl.BlockSpec((TOK_TILE, d), row),
                  pl.BlockSpec((None, 6, d), lambda i: (lay.mod_row(i, TOK_TILE), 0, 0)),
                  _const_spec((2, d))],
        out_specs=pl.BlockSpec((TOK_TILE, d), row),
        out_shape=jax.ShapeDtypeStruct((n, d), F32),
        scratch_shapes=[pltpu.SMEM((n_idx,), jnp.int32), pltpu.VMEM((n_idx, d), F32),
                        pltpu.SemaphoreType.DMA, pltpu.SemaphoreType.DMA],
        compiler_params=_params("arbitrary"),
        name="moe_combine",
    )(dest, ys, gate, h, mod, ln)


def kernel(x, c, ctx, c_ctx, ada_w, ada_b, ln_g, ln_b, ret_w_in, ret_decay, ret_gn_g, ret_w_out, mla_w_down, mla_q_norm, mla_kv_norm, mla_w_uq, mla_w_ukv, mla_w_out, conv_w_in, conv_w, conv_b, conv_w_out, moe_w_group, moe_b_group, moe_w_expert, moe_b_expert, moe_w1, moe_w3, moe_w2):
    b, s, d = x.shape
    depth = ada_w.shape[0]
    lay = _Layout(b, s, ctx.shape[1])
    alpha = (2.0 * depth) ** 0.25

    mod_rows = -(-(b + 1) // 8) * 8
    cc = jnp.concatenate([c, c_ctx[None, :], jnp.zeros((mod_rows - b - 1, d), F32)], axis=0)
    mod_all = _ada(cc, ada_w, ada_b).reshape(depth, mod_rows, 6, d)

    h = jnp.concatenate([ctx.reshape(lay.n_ctx, d), x.reshape(b * s, d)], axis=0)
    for i in range(depth):
        kind, j = i % N_MIXERS, i // N_MIXERS
        mod = mod_all[i]
        ln1 = jnp.stack([ln_g[i, 0], ln_b[i, 0]])
        ln2 = jnp.stack([ln_g[i, 1], ln_b[i, 1]])
        pad = jnp.zeros((d, ROUTER_W - MOE_GROUPS - MOE_EXPERTS), F32)
        w_router = jnp.concatenate([moe_w_group[i], moe_w_expert[i], pad], axis=1).astype(BF16)
        b_router = jnp.concatenate([moe_b_group[i], moe_b_expert[i], pad[0]]).reshape(1, ROUTER_W)
        if kind == 0:
            a = _retention(lay, h, mod, ret_w_in[j], ret_decay[j], ret_gn_g[j], d)
            h1, t, logits = _post(lay, a, h, mod, ret_w_out[j], ln1, w_router, b_router, alpha)
        elif kind == 1:
            a = _mla(lay, h, mod, mla_w_down[j], mla_q_norm[j], mla_kv_norm[j], mla_w_uq[j], mla_w_ukv[j], d)
            h1, t, logits = _post(lay, a, h, mod, mla_w_out[j], ln1, w_router, b_router, alpha)
        else:
            h1, t, logits = _conv_layer(lay, h, mod, conv_w_in[j], conv_w[j], conv_b[j], conv_w_out[j],
                                        ln1, w_router, b_router, alpha)
        h = _moe(lay, t, logits, h1, mod, ln2, moe_w1[i], moe_w3[i], moe_w2[i], alpha)
    return h[lay.n_ctx:].reshape(b, s, d)
```

```python
import functools

import jax
import jax.numpy as jnp
from jax import lax
from jax.experimental import pallas as pl
from jax.experimental.pallas import tpu as pltpu

F32 = jnp.float32
BF16 = jnp.bfloat16

GRID_W = 64
LN_EPS = 1e-5
RMS_EPS = 1e-6
ROPE_BASE = 10000.0
N_MIXERS = 3
RET_HEADS = 4
MLA_HEADS = 8
MLA_NOPE = 128
MLA_ROPE = 64
MLA_V = 128
MLA_QK = 256
MOE_GROUPS = 4
MOE_EPG = 8
MOE_EXPERTS = MOE_GROUPS * MOE_EPG
MOE_TOPK = 2
ROUTER_W = 128

CHUNK = 256
TM = 512
TQ = 512
TOK_TILE = 512
EXP_BLOCK = 512
VMEM_LIMIT = 56 * 1024 * 1024


def _params(*sem):
    return pltpu.CompilerParams(dimension_semantics=sem, vmem_limit_bytes=VMEM_LIMIT)


def _const_spec(shape):
    nd = len(shape)
    return pl.BlockSpec(shape, lambda *_: (0,) * nd, pipeline_mode=pl.Buffered(1))


def _dot(a, b):
    return jnp.dot(a, b, preferred_element_type=F32)


def _dot_nt(a, b):
    return lax.dot_general(a, b, (((1,), (1,)), ((), ())), preferred_element_type=F32)


def _dot_tn(a, b):
    return lax.dot_general(a, b, (((0,), (0,)), ((), ())), preferred_element_type=F32)


def _silu(x):
    return x * jax.nn.sigmoid(x)


def _layer_norm(x, g, b):
    mu = jnp.mean(x, axis=-1, keepdims=True)
    xc = x - mu
    var = jnp.mean(xc * xc, axis=-1, keepdims=True)
    return xc * lax.rsqrt(var + LN_EPS) * g + b


def _rms_norm(x, g):
    return x * lax.rsqrt(jnp.mean(x * x, axis=-1, keepdims=True) + RMS_EPS) * g


def _ada_kernel(c_ref, w_ref, b_ref, o_ref):
    a = _silu(c_ref[...]).astype(BF16)
    o_ref[...] = _dot(a, w_ref[...].astype(BF16)) + b_ref[...]


def _ada(cc, ada_w, ada_b):
    depth, d, n6 = ada_w.shape
    rows = cc.shape[0]
    tn = 1536
    return pl.pallas_call(
        _ada_kernel,
        grid=(depth, n6 // tn),
        in_specs=[
            pl.BlockSpec((rows, d), lambda l, j: (0, 0)),
            pl.BlockSpec((None, d, tn), lambda l, j: (l, 0, j)),
            pl.BlockSpec((None, 1, tn), lambda l, j: (l, 0, j)),
        ],
        out_specs=pl.BlockSpec((None, rows, tn), lambda l, j: (l, 0, j)),
        out_shape=jax.ShapeDtypeStruct((depth, rows, n6), F32),
        compiler_params=_params("parallel", "parallel"),
        name="ada_mod",
    )(cc, ada_w, ada_b.reshape(depth, 1, n6))


class _Layout:
    def __init__(self, batch, seq, ctx_len):
        assert ctx_len == CHUNK and seq % TM == 0 and seq % CHUNK == 0
        self.batch, self.seq, self.ctx = batch, seq, ctx_len
        self.n_ctx = batch * ctx_len
        self.n = self.n_ctx + batch * seq
        assert self.n_ctx % TM == 0 and self.n_ctx % seq == 0 and self.n % TOK_TILE == 0

    def mod_row(self, tile, rows_per_tile):
        ctx_tiles = self.n_ctx // rows_per_tile
        per_batch = self.seq // rows_per_tile
        return jnp.where(tile < ctx_tiles, self.batch, (tile - ctx_tiles) // per_batch)

    def pos_block(self, tile, rows_per_tile):
        ctx_tiles = self.n_ctx // rows_per_tile
        per_batch = self.seq // rows_per_tile
        return jnp.where(tile < ctx_tiles, 0, 1 + (tile - ctx_tiles) % per_batch)


def _rope_angles(seq, dim):
    n_rows = seq // GRID_W
    rows = jnp.repeat(jnp.arange(n_rows, dtype=F32), GRID_W)
    cols = jnp.tile(jnp.arange(GRID_W, dtype=F32), n_rows)
    quarter = dim // 4
    inv_freq = ROPE_BASE ** (-jnp.arange(quarter, dtype=F32) / quarter)
    return jnp.concatenate([rows[:, None] * inv_freq, cols[:, None] * inv_freq], axis=-1)


def _with_identity_block(table, fill, rows):
    ident = jnp.full((rows, table.shape[1]), fill, F32)
    return jnp.concatenate([ident, table], axis=0)


def _proj_ret_kernel(h_ref, mod_ref, cos_ref, sin_ref, kdec_ref, wq_ref, wk_ref, wv_ref, wg_ref,
                     q_ref, k_ref, kf_ref, kb_ref, v_ref, g_ref, *, dk):
    u = (h_ref[...] * (1.0 + mod_ref[1:2, :]) + mod_ref[0:1, :]).astype(BF16)
    cos = cos_ref[...]
    sin = sin_ref[...]
    half = dk // 2
    yq = _dot(u, wq_ref[...])
    for hd in range(RET_HEADS):
        x1 = yq[:, hd * dk:hd * dk + half]
        x2 = yq[:, hd * dk + half:(hd + 1) * dk]
        q_ref[:, hd * dk:hd * dk + half] = (x1 * cos - x2 * sin).astype(BF16)
        q_ref[:, hd * dk + half:(hd + 1) * dk] = (x1 * sin + x2 * cos).astype(BF16)
    yk = _dot(u, wk_ref[...])
    k_scale = dk ** -0.5
    for hd in range(RET_HEADS):
        x1 = yk[:, hd * dk:hd * dk + half]
        x2 = yk[:, hd * dk + half:(hd + 1) * dk]
        df = kdec_ref[:, hd:hd + 1]
        db = kdec_ref[:, RET_HEADS + hd:RET_HEADS + hd + 1]
        for part, o in ((0, (x1 * cos - x2 * sin) * k_scale), (1, (x1 * sin + x2 * cos) * k_scale)):
            sl = slice(hd * dk + part * half, hd * dk + (part + 1) * half)
            k_ref[:, sl] = o.astype(BF16)
            kf_ref[:, sl] = (o * df).astype(BF16)
            kb_ref[:, sl] = (o * db).astype(BF16)
    v_ref[...] = _dot(u, wv_ref[...]).astype(BF16)
    g_ref[...] = _silu(_dot(u, wg_ref[...]))


def _ret_scan_kernel(cdec_ref, q_ref, k_ref, kd_ref, v_ref, intra_ref, qdec_ref, *rest, dk, dv, direction):
    if direction == 0:
        o_ref, state_ref = rest
    else:
        of_ref, g_ref, gn_ref, o_ref, state_ref = rest

    @pl.when(pl.program_id(1) == 0)
    def _():
        state_ref[...] = jnp.zeros_like(state_ref)

    for hd in range(RET_HEADS):
        q = q_ref[:, hd * dk:(hd + 1) * dk]
        k = k_ref[:, hd * dk:(hd + 1) * dk]
        kd = kd_ref[:, hd * dk:(hd + 1) * dk]
        v = v_ref[:, hd * dv:(hd + 1) * dv]
        p = (_dot_nt(q, k) * intra_ref[hd]).astype(BF16)
        st = state_ref[hd]
        col = direction * RET_HEADS + hd
        o = _dot(p, v) + qdec_ref[:, col:col + 1] * _dot(q, st.astype(BF16))
        state_ref[hd] = st * cdec_ref[col] + _dot_tn(kd, v)
        if direction == 0:
            o_ref[:, hd * dv:(hd + 1) * dv] = o
        else:
            o = o + of_ref[:, hd * dv:(hd + 1) * dv]
            mu = jnp.mean(o, axis=-1, keepdims=True)
            oc = o - mu
            var = jnp.mean(oc * oc, axis=-1, keepdims=True)
            on = oc * lax.rsqrt(var + LN_EPS) * gn_ref[:, hd * dv:(hd + 1) * dv]
            o_ref[:, hd * dv:(hd + 1) * dv] = (g_ref[:, hd * dv:(hd + 1) * dv] * on).astype(BF16)


def _retention(lay, h, mod, w_in, decay_logit, gn_g, d_model):
    n, b, nc = lay.n, lay.batch, lay.seq // CHUNK
    dk = d_model // RET_HEADS
    dv = 2 * dk
    qk, vw = RET_HEADS * dk, RET_HEADS * dv
    w_in = w_in.astype(BF16)
    wq, wk, wv, wg = w_in[:, :qk], w_in[:, qk:2 * qk], w_in[:, 2 * qk:2 * qk + vw], w_in[:, 2 * qk + vw:]

    ang = _rope_angles(lay.seq, dk)
    cos_t = _with_identity_block(jnp.cos(ang), 1.0, TM)
    sin_t = _with_identity_block(jnp.sin(ang), 0.0, TM)

    lg = jax.nn.log_sigmoid(decay_logit.astype(F32))
    idx = jnp.arange(CHUNK, dtype=F32)
    k_pow = jnp.stack([CHUNK - 1.0 - idx, idx])
    q_pow = jnp.stack([idx + 1.0, CHUNK - idx])
    kdec = jnp.exp(k_pow[:, :, None] * lg[:, None, :])
    qdec = jnp.exp(q_pow[:, :, None] * lg[:, None, :])
    kdec = jnp.moveaxis(kdec, 0, 1).reshape(CHUNK, 2 * RET_HEADS)
    qdec = jnp.moveaxis(qdec, 0, 1).reshape(CHUNK, 2 * RET_HEADS)
    cdec = jnp.exp(CHUNK * lg).reshape(2 * RET_HEADS)
    rel = idx[:, None] - idx[None, :]
    rel = jnp.stack([rel, -rel])
    intra = jnp.where(rel[:, None] >= 0, jnp.exp(jnp.maximum(rel[:, None], 0.0) * lg[:, :, None, None]), 0.0)
    kdec_tm = jnp.tile(kdec, (TM // CHUNK, 1))

    n_tiles = n // TM
    row = lambda i: (i, 0)
    q, k, kf, kb, v, g = pl.pallas_call(
        functools.partial(_proj_ret_kernel, dk=dk),
        grid=(n_tiles,),
        in_specs=[
            pl.BlockSpec((TM, d_model), row),
            pl.BlockSpec((None, 6, d_model), lambda i: (lay.mod_row(i, TM), 0, 0)),
            pl.BlockSpec((TM, dk // 2), lambda i: (lay.pos_block(i, TM), 0)),
            pl.BlockSpec((TM, dk // 2), lambda i: (lay.pos_block(i, TM), 0)),
            _const_spec((TM, 2 * RET_HEADS)),
            _const_spec((d_model, qk)), _const_spec((d_model, qk)),
            _const_spec((d_model, vw)), _const_spec((d_model, vw)),
        ],
        out_specs=[pl.BlockSpec((TM, qk), row)] * 4 + [pl.BlockSpec((TM, vw), row)] * 2,
        out_shape=[jax.ShapeDtypeStruct((n, qk), BF16)] * 4
        + [jax.ShapeDtypeStruct((n, vw), BF16), jax.ShapeDtypeStruct((n, vw), F32)],
        compiler_params=_params("parallel"),
        name="ret_proj",
    )(h, mod, cos_t, sin_t, kdec_tm, wq, wk, wv, wg)

    def chunk_fwd(bi, c):
        return (jnp.where(c == 0, bi, b + bi * nc + c - 1), 0)

    def chunk_bwd(bi, c):
        return (jnp.where(c == 0, bi, b + bi * nc + nc - c), 0)

    def scan(direction, chunk_map, kd, extra_in, extra_specs, out_dtype, name):
        return pl.pallas_call(
            functools.partial(_ret_scan_kernel, dk=dk, dv=dv, direction=direction),
            grid=(b, nc + 1),
            in_specs=[
                pl.BlockSpec(memory_space=pltpu.SMEM),
                pl.BlockSpec((CHUNK, qk), chunk_map),
                pl.BlockSpec((CHUNK, qk), chunk_map),
                pl.BlockSpec((CHUNK, qk), chunk_map),
                pl.BlockSpec((CHUNK, vw), chunk_map),
                _const_spec((RET_HEADS, CHUNK, CHUNK)),
                _const_spec((CHUNK, 2 * RET_HEADS)),
            ] + extra_specs,
            out_specs=pl.BlockSpec((CHUNK, vw), chunk_map),
            out_shape=jax.ShapeDtypeStruct((n, vw), out_dtype),
            scratch_shapes=[pltpu.VMEM((RET_HEADS, dk, dv), F32)],
            compiler_params=_params("parallel", "arbitrary"),
            name=name,
        )(cdec, q, k, kd, v, intra[direction], qdec, *extra_in)

    o_f = scan(0, chunk_fwd, kf, [], [], F32, "ret_scan_fwd")
    return scan(1, chunk_bwd, kb, [o_f, g, gn_g.reshape(1, vw).astype(F32)],
                [pl.BlockSpec((CHUNK, vw), chunk_bwd), pl.BlockSpec((CHUNK, vw), chunk_bwd),
                 _const_spec((1, vw))], BF16, "ret_scan_bwd")


def _proj_mla_kernel(h_ref, mod_ref, ct_ref, st_ref, wdq_ref, wdkv_ref, wkr_ref, qn_ref, kvn_ref,
                     wuq_ref, wukn_ref, wuv_ref, q_ref, k_ref, vt_ref):
    u = (h_ref[...] * (1.0 + mod_ref[1:2, :]) + mod_ref[0:1, :]).astype(BF16)
    ct = ct_ref[...]
    st = st_ref[...]
    half = MLA_QK // 2

    def rope(x):
        return x * ct + pltpu.roll(x, half // 2, 1) * st

    cq = _rms_norm(_dot(u, wdq_ref[...]), qn_ref[...]).astype(BF16)
    yq = _dot(cq, wuq_ref[...])
    ckv = _rms_norm(_dot(u, wdkv_ref[...]), kvn_ref[...]).astype(BF16)
    kn = _dot(ckv, wukn_ref[...])
    kr = rope(_dot(u, wkr_ref[...])).astype(BF16)
    for hd in range(MLA_HEADS):
        q_ref[:, hd * MLA_QK:hd * MLA_QK + half] = yq[:, hd * MLA_QK:hd * MLA_QK + half].astype(BF16)
        q_ref[:, hd * MLA_QK + half:(hd + 1) * MLA_QK] = rope(
            yq[:, hd * MLA_QK + half:(hd + 1) * MLA_QK]).astype(BF16)
        k_ref[:, hd * MLA_QK:hd * MLA_QK + half] = kn[:, hd * half:(hd + 1) * half].astype(BF16)
        k_ref[:, hd * MLA_QK + half:(hd + 1) * MLA_QK] = kr
    vt_ref[...] = _dot(ckv, wuv_ref[...]).T.astype(BF16)


KV_CHUNK = 512


def _attn_kernel(q_ref, *rest, scale, n_kv):
    kv_refs, o_ref, s_ref = rest[:2 * n_kv], rest[-2], rest[-1]
    q = q_ref[...]
    chunks = []
    row = 0
    for j in range(n_kv):
        keys = kv_refs[2 * j].shape[0]
        step = min(keys, KV_CHUNK)
        for off in range(0, keys, step):
            chunks.append((j, off, row, step))
            row += step
    m = None
    for j, off, row, w in chunks:
        s = _dot_nt(kv_refs[2 * j][off:off + w, :], q)
        s_ref[row:row + w, :] = s
        cm = jnp.max(s, axis=0, keepdims=True)
        m = cm if m is None else jnp.maximum(m, cm)
    acc = None
    den = None
    for j, off, row, w in chunks:
        p = jnp.exp((s_ref[row:row + w, :] - m) * scale)
        cs = jnp.sum(p, axis=0, keepdims=True)
        den = cs if den is None else den + cs
        pv = _dot(kv_refs[2 * j + 1][:, off:off + w], p.astype(BF16))
        acc = pv if acc is None else acc + pv
    o_ref[...] = (acc / den).T.astype(BF16)


def _pad_rope_cols(w):
    z = jnp.zeros((w.shape[0], MLA_ROPE // 2), w.dtype)
    return jnp.concatenate([w[:, :MLA_ROPE // 2], z, w[:, MLA_ROPE // 2:], z], axis=1)


def _mla(lay, h, mod, w_down, q_norm, kv_norm, w_uq, w_ukv, d_model):
    n, b, s = lay.n, lay.batch, lay.seq
    q_lora, kv_lora = q_norm.shape[0], kv_norm.shape[0]
    w_dq = w_down[:, :q_lora].astype(BF16)
    w_dkv = w_down[:, q_lora:q_lora + kv_lora].astype(BF16)
    w_kr = _pad_rope_cols(w_down[:, q_lora + kv_lora:]).astype(BF16)
    w_uq = w_uq.reshape(q_lora, MLA_HEADS, MLA_NOPE + MLA_ROPE)
    w_uq = jnp.concatenate(
        [w_uq[:, :, :MLA_NOPE], jax.vmap(_pad_rope_cols, 1, 1)(w_uq[:, :, MLA_NOPE:])], axis=2)
    w_uq = w_uq.reshape(q_lora, MLA_HEADS * MLA_QK).astype(BF16)
    w_ukv = w_ukv.reshape(kv_lora, MLA_HEADS, MLA_NOPE + MLA_V)
    w_ukn = w_ukv[:, :, :MLA_NOPE].reshape(kv_lora, MLA_HEADS * MLA_NOPE).astype(BF16)
    w_uv = w_ukv[:, :, MLA_NOPE:].reshape(kv_lora, MLA_HEADS * MLA_V).astype(BF16)

    ang = _rope_angles(s, MLA_ROPE)
    z = jnp.zeros_like(ang)
    ct = _with_identity_block(jnp.concatenate([jnp.cos(ang), z, jnp.cos(ang), z], axis=1), 1.0, TM)
    st = _with_identity_block(jnp.concatenate([-jnp.sin(ang), z, jnp.sin(ang), z], axis=1), 0.0, TM)

    row = lambda i: (i, 0)
    qw, vw = MLA_HEADS * MLA_QK, MLA_HEADS * MLA_V
    q, k, vt = pl.pallas_call(
        _proj_mla_kernel,
        grid=(n // TM,),
        in_specs=[
            pl.BlockSpec((TM, d_model), row),
            pl.BlockSpec((None, 6, d_model), lambda i: (lay.mod_row(i, TM), 0, 0)),
            pl.BlockSpec((TM, MLA_QK // 2), lambda i: (lay.pos_block(i, TM), 0)),
            pl.BlockSpec((TM, MLA_QK // 2), lambda i: (lay.pos_block(i, TM), 0)),
            _const_spec((d_model, q_lora)), _const_spec((d_model, kv_lora)), _const_spec((d_model, MLA_QK // 2)),
            _const_spec((1, q_lora)), _const_spec((1, kv_lora)),
            _const_spec((q_lora, qw)), _const_spec((kv_lora, vw)), _const_spec((kv_lora, vw)),
        ],
        out_specs=[pl.BlockSpec((TM, qw), row), pl.BlockSpec((TM, qw), row),
                   pl.BlockSpec((vw, TM), lambda i: (0, i))],
        out_shape=[jax.ShapeDtypeStruct((n, qw), BF16), jax.ShapeDtypeStruct((n, qw), BF16),
                   jax.ShapeDtypeStruct((vw, n), BF16)],
        compiler_params=_params("parallel"),
        name="mla_proj",
    )(h, mod, ct, st, w_dq, w_dkv, w_kr, q_norm.reshape(1, -1).astype(F32), kv_norm.reshape(1, -1).astype(F32),
      w_uq, w_ukn, w_uv)

    scale = (MLA_NOPE + MLA_ROPE) ** -0.5
    ctx_q = lambda bi, hd: (bi, hd)
    att = pl.pallas_call(
        functools.partial(_attn_kernel, scale=scale, n_kv=1),
        grid=(b, MLA_HEADS),
        in_specs=[pl.BlockSpec((CHUNK, MLA_QK), ctx_q), pl.BlockSpec((CHUNK, MLA_QK), ctx_q),
                  pl.BlockSpec((MLA_V, CHUNK), lambda bi, hd: (hd, bi))],
        out_specs=pl.BlockSpec((CHUNK, MLA_V), ctx_q),
        out_shape=jax.ShapeDtypeStruct((n, vw), BF16),
        scratch_shapes=[pltpu.VMEM((CHUNK, CHUNK), F32)],
        compiler_params=_params("parallel", "parallel"),
        name="mla_attn_ctx",
    )(q, k, vt)

    lat_blk = lay.n_ctx // s
    q_tiles = s // TQ
    lat_q = lambda bi, hd, t: (lay.n_ctx // TQ + bi * q_tiles + t, hd)
    ctx_kv = lambda bi, hd, t: (bi, hd)
    lat_kv = lambda bi, hd, t: (lat_blk + bi, hd)
    return pl.pallas_call(
        functools.partial(_attn_kernel, scale=scale, n_kv=2),
        grid=(b, MLA_HEADS, q_tiles),
        in_specs=[pl.BlockSpec((TQ, MLA_QK), lat_q),
                  pl.BlockSpec((CHUNK, MLA_QK), ctx_kv),
                  pl.BlockSpec((MLA_V, CHUNK), lambda bi, hd, t: (hd, bi)),
                  pl.BlockSpec((s, MLA_QK), lat_kv),
                  pl.BlockSpec((MLA_V, s), lambda bi, hd, t: (hd, lat_blk + bi)),
                  pl.BlockSpec(memory_space=pl.ANY)],
        out_specs=pl.BlockSpec((TQ, MLA_V), lat_q),
        out_shape=jax.ShapeDtypeStruct((n, vw), BF16),
        input_output_aliases={5: 0},
        scratch_shapes=[pltpu.VMEM((CHUNK + s, TQ), F32)],
        compiler_params=_params("parallel", "parallel", "arbitrary"),
        name="mla_attn_lat",
    )(q, k, vt, k, vt, att)


def _post_tail(y, h, mod_ref, ln_ref, wr_ref, br_ref, h_out, t_out, lg_out, alpha):
    h1 = _layer_norm(alpha * h + mod_ref[2:3, :] * y, ln_ref[0:1, :], ln_ref[1:2, :])
    t = h1 * (1.0 + mod_ref[4:5, :]) + mod_ref[3:4, :]
    h_out[...] = h1
    t_out[...] = t
    lg_out[...] = _dot(t.astype(BF16), wr_ref[...]) + br_ref[...]


def _post_kernel(a_ref, h_ref, mod_ref, w_ref, ln_ref, wr_ref, br_ref, h_out, t_out, lg_out, *, alpha):
    y = _dot(a_ref[...], w_ref[...])
    _post_tail(y, h_ref[...], mod_ref, ln_ref, wr_ref, br_ref, h_out, t_out, lg_out, alpha)


def _post(lay, a, h, mod, w_out, ln, w_router, b_router, alpha):
    n, d = h.shape
    ka = a.shape[1]
    row = lambda i: (i, 0)
    return pl.pallas_call(
        functools.partial(_post_kernel, alpha=alpha),
        grid=(n // TM,),
        in_specs=[
            pl.BlockSpec((TM, ka), row),
            pl.BlockSpec((TM, d), row),
            pl.BlockSpec((None, 6, d), lambda i: (lay.mod_row(i, TM), 0, 0)),
            _const_spec((ka, d)), _const_spec((2, d)), _const_spec((d, ROUTER_W)), _const_spec((1, ROUTER_W)),
        ],
        out_specs=[pl.BlockSpec((TM, d), row), pl.BlockSpec((TM, d), row), pl.BlockSpec((TM, ROUTER_W), row)],
        out_shape=[jax.ShapeDtypeStruct((n, d), F32), jax.ShapeDtypeStruct((n, d), F32),
                   jax.ShapeDtypeStruct((n, ROUTER_W), F32)],
        compiler_params=_params("parallel"),
        name="post_mixer",
    )(a, h, mod, w_out.astype(BF16), ln, w_router, b_router)


HALO = 8


def _conv_kernel(h_ref, hp_ref, hn_ref, mod_ref, win_ref, cw_ref, cb_ref, w_ref, ln_ref, wr_ref, br_ref,
                 h_out, t_out, lg_out, *, alpha, tiles_per_seq, ctx_tiles):
    i = pl.program_id(0)
    d = h_ref.shape[1]
    h = h_ref[...]
    hx = jnp.concatenate([hp_ref[...], h, hn_ref[...]], axis=0)
    u = (hx * (1.0 + mod_ref[1:2, :]) + mod_ref[0:1, :]).astype(BF16)
    y = _dot(u, win_ref[...])
    gate_b = y[HALO:HALO + CHUNK, :d]
    zx = y[:, d:2 * d] * y[:, 2 * d:]
    z = zx[HALO:HALO + CHUNK]
    pos = jnp.where(i < ctx_tiles, 0, (i - ctx_tiles) % tiles_per_seq)
    last = jnp.where(i < ctx_tiles, 0, tiles_per_seq - 1)
    z_before = jnp.where(pos == 0, 0.0, zx[HALO - 1:HALO])
    z_after = jnp.where(pos == last, 0.0, zx[HALO + CHUNK:HALO + CHUNK + 1])
    r = lax.broadcasted_iota(jnp.int32, (CHUNK, 1), 0)
    z_prev = jnp.where(r == 0, z_before, pltpu.roll(z, 1, 0))
    z_next = jnp.where(r == CHUNK - 1, z_after, pltpu.roll(z, CHUNK - 1, 0))
    conv = cw_ref[0:1, :] * z_prev + cw_ref[1:2, :] * z + cw_ref[2:3, :] * z_next + cb_ref[...]
    yo = _dot((gate_b * conv).astype(BF16), w_ref[...])
    _post_tail(yo, h, mod_ref, ln_ref, wr_ref, br_ref, h_out, t_out, lg_out, alpha)


def _conv_layer(lay, h, mod, w_in, cw, cb, w_out, ln, w_router, b_router, alpha):
    n, d = h.shape
    n_tiles = n // CHUNK
    per = CHUNK // HALO
    row = lambda i: (i, 0)
    return pl.pallas_call(
        functools.partial(_conv_kernel, alpha=alpha, tiles_per_seq=lay.seq // CHUNK, ctx_tiles=lay.n_ctx // CHUNK),
        grid=(n_tiles,),
        in_specs=[
            pl.BlockSpec((CHUNK, d), row),
            pl.BlockSpec((HALO, d), lambda i: (jnp.maximum(i * per - 1, 0), 0)),
            pl.BlockSpec((HALO, d), lambda i: (jnp.minimum((i + 1) * per, n_tiles * per - 1), 0)),
            pl.BlockSpec((None, 6, d), lambda i: (lay.mod_row(i, CHUNK), 0, 0)),
            _const_spec((d, 3 * d)), _const_spec((3, d)), _const_spec((1, d)), _const_spec((d, d)),
            _const_spec((2, d)), _const_spec((d, ROUTER_W)), _const_spec((1, ROUTER_W)),
        ],
        out_specs=[pl.BlockSpec((CHUNK, d), row), pl.BlockSpec((CHUNK, d), row),
                   pl.BlockSpec((CHUNK, ROUTER_W), row)],
        out_shape=[jax.ShapeDtypeStruct((n, d), F32), jax.ShapeDtypeStruct((n, d), F32),
                   jax.ShapeDtypeStruct((n, ROUTER_W), F32)],
        compiler_params=_params("parallel"),
        name="conv_layer",
    )(h, h, h, mod, w_in.astype(BF16), cw.astype(F32), cb.reshape(1, d).astype(F32), w_out.astype(BF16),
      ln, w_router, b_router)


def _route_kernel(lg_ref, tri_ref, ri_ref, rf_ref, cnt_ref, base_ref):
    @pl.when(pl.program_id(0) == 0)
    def _():
        base_ref[...] = jnp.zeros_like(base_ref)

    x = lg_ref[...]
    lane = lax.broadcasted_iota(jnp.int32, x.shape, 1)

    def softmax(mask):
        m = jnp.max(jnp.where(mask, x, -jnp.inf), axis=-1, keepdims=True)
        e = jnp.where(mask, jnp.exp(x - m), 0.0)
        return e / jnp.sum(e, axis=-1, keepdims=True)

    def top1(prob, mask):
        p = jnp.max(jnp.where(mask, prob, -1.0), axis=-1, keepdims=True)
        i = jnp.min(jnp.where(mask & (prob == p), lane, ROUTER_W), axis=-1, keepdims=True)
        return p, i

    g_mask = lane < MOE_GROUPS
    g_p, g_idx = top1(softmax(g_mask), g_mask)
    e_lo = MOE_GROUPS + MOE_EPG * g_idx
    e_mask = (lane >= e_lo) & (lane < e_lo + MOE_EPG)
    e_prob = softmax(e_mask)
    p1, i1 = top1(e_prob, e_mask)
    p2, i2 = top1(e_prob, e_mask & (lane != i1))
    denom = p1 + p2
    gate1 = g_p * p1 / denom
    gate2 = g_p * p2 / denom

    sel1 = lane == i1
    sel2 = lane == i2
    cnt = (sel1 | sel2).astype(F32)
    before = base_ref[...] + _dot(tri_ref[...], cnt.astype(BF16))
    rank1 = jnp.sum(jnp.where(sel1, before, 0.0), axis=-1, keepdims=True).astype(jnp.int32)
    rank2 = jnp.sum(jnp.where(sel2, before, 0.0), axis=-1, keepdims=True).astype(jnp.int32)
    total = base_ref[...] + jnp.sum(cnt, axis=0, keepdims=True)
    base_ref[...] = total
    cnt_ref[...] = jnp.broadcast_to(total, cnt_ref.shape)

    ri_ref[...] = jnp.where(lane == 0, i1 - MOE_GROUPS, jnp.where(lane == 1, i2 - MOE_GROUPS,
                            jnp.where(lane == 2, rank1, jnp.where(lane == 3, rank2, 0))))
    rf_ref[...] = jnp.where(lane == 0, gate1, jnp.where(lane == 1, gate2, 0.0))


def _route(logits):
    n = logits.shape[0]
    r = lax.broadcasted_iota(jnp.int32, (TOK_TILE, TOK_TILE), 0)
    c = lax.broadcasted_iota(jnp.int32, (TOK_TILE, TOK_TILE), 1)
    tri = (c < r).astype(BF16)
    row = lambda i: (i, 0)
    return pl.pallas_call(
        _route_kernel,
        grid=(n // TOK_TILE,),
        in_specs=[pl.BlockSpec((TOK_TILE, ROUTER_W), row), _const_spec((TOK_TILE, TOK_TILE))],
        out_specs=[pl.BlockSpec((TOK_TILE, ROUTER_W), row), pl.BlockSpec((TOK_TILE, ROUTER_W), row),
                   pl.BlockSpec((8, ROUTER_W), lambda i: (0, 0))],
        out_shape=[jax.ShapeDtypeStruct((n, ROUTER_W), jnp.int32), jax.ShapeDtypeStruct((n, ROUTER_W), F32),
                   jax.ShapeDtypeStruct((8, ROUTER_W), F32)],
        scratch_shapes=[pltpu.VMEM((1, ROUTER_W), F32)],
        compiler_params=_params("arbitrary"),
        name="moe_route",
    )(logits, tri)


def _plan(route_i, counts, n_blocks):
    n = route_i.shape[0]
    counts = counts[0, MOE_GROUPS:MOE_GROUPS + MOE_EXPERTS].astype(jnp.int32)
    padded = (counts + EXP_BLOCK - 1) // EXP_BLOCK * EXP_BLOCK
    pad_ends = jnp.cumsum(padded)
    pad_starts = pad_ends - padded
    expert, rank = route_i[:, :MOE_TOPK], route_i[:, MOE_TOPK:2 * MOE_TOPK]
    onehot = expert[:, :, None] == jnp.arange(MOE_EXPERTS, dtype=jnp.int32)
    dest = rank + jnp.sum(jnp.where(onehot, pad_starts, 0), axis=-1)
    first_row = jnp.arange(n_blocks, dtype=jnp.int32) * EXP_BLOCK
    block_e = jnp.minimum(jnp.sum(first_row[:, None] >= pad_ends[None, :], axis=1),
                          MOE_EXPERTS - 1).astype(jnp.int32)
    n_valid = (pad_ends[-1] // EXP_BLOCK).astype(jnp.int32).reshape(1)
    dest = dest.reshape(n // TOK_TILE, TOK_TILE, MOE_TOPK).transpose(0, 2, 1).reshape(-1)
    return dest, block_e, n_valid


def _row_copy(src, s, dst, d, sem):
    return pltpu.make_async_copy(src.at[pl.ds(s, 1)], dst.at[pl.ds(d, 1)], sem)


def _load_indices(idx_hbm, idx_smem, sem):
    n_idx = idx_smem.shape[0]
    cp = pltpu.make_async_copy(idx_hbm.at[pl.ds(pl.program_id(0) * n_idx, n_idx)], idx_smem, sem)
    cp.start()
    cp.wait()


def _dispatch_kernel(idx_hbm, t_ref, xs_in, xs_out, idx_smem, sem_idx, sem):
    del xs_in
    _load_indices(idx_hbm, idx_smem, sem_idx)
    for kk in range(MOE_TOPK):
        def issue(r, c, kk=kk):
            _row_copy(t_ref, r, xs_out, idx_smem[kk * TOK_TILE + r], sem).start()
            return c
        lax.fori_loop(0, TOK_TILE, issue, 0, unroll=8)

    def drain(j, c):
        _row_copy(t_ref, 0, xs_out, 0, sem).wait()
        return c
    lax.fori_loop(0, MOE_TOPK * TOK_TILE, drain, 0, unroll=8)


def _expert_kernel(be_ref, nv_ref, x_ref, w1_ref, w3_ref, w2_ref, o_ref):
    del be_ref
    valid = pl.program_id(0) < nv_ref[0]

    @pl.when(valid)
    def _():
        x = x_ref[...].astype(BF16)
        hdn = _silu(_dot(x, w1_ref[...])) * _dot(x, w3_ref[...])
        o_ref[...] = _dot(hdn.astype(BF16), w2_ref[...])

    @pl.when(jnp.logical_not(valid))
    def _():
        o_ref[...] = jnp.zeros_like(o_ref)


def _combine_kernel(idx_hbm, ys_hbm, gate_ref, h_ref, mod_ref, ln_ref, o_ref, idx_smem, buf, sem_idx, sem,
                    *, alpha):
    _load_indices(idx_hbm, idx_smem, sem_idx)

    def issue(j, c):
        _row_copy(ys_hbm, idx_smem[j], buf, j, sem).start()
        return c
    lax.fori_loop(0, MOE_TOPK * TOK_TILE, issue, 0, unroll=8)

    def drain(j, c):
        _row_copy(ys_hbm, 0, buf, 0, sem).wait()
        return c
    lax.fori_loop(0, MOE_TOPK * TOK_TILE, drain, 0, unroll=8)

    f = gate_ref[:, 0:1] * buf[0:TOK_TILE, :] + gate_ref[:, 1:2] * buf[TOK_TILE:2 * TOK_TILE, :]
    o_ref[...] = _layer_norm(alpha * h_ref[...] + mod_ref[5:6, :] * f, ln_ref[0:1, :], ln_ref[1:2, :])


def _moe(lay, t, logits, h, mod, ln, w1, w3, w2, alpha):
    n, d = t.shape
    hid = w1.shape[2]
    nk = n * MOE_TOPK
    n_blocks = -(-nk // EXP_BLOCK) + MOE_EXPERTS
    n_rows = n_blocks * EXP_BLOCK
    route_i, gate, counts = _route(logits)
    dest, block_e, n_valid = _plan(route_i, counts, n_blocks)
    n_tiles = n // TOK_TILE
    n_idx = MOE_TOPK * TOK_TILE
    row = lambda i: (i, 0)

    xs = pl.pallas_call(
        _dispatch_kernel,
        grid=(n_tiles,),
        in_specs=[pl.BlockSpec(memory_space=pl.ANY), pl.BlockSpec((TOK_TILE, d), row),
                  pl.BlockSpec(memory_space=pl.ANY)],
        out_specs=pl.BlockSpec(memory_space=pl.ANY),
        out_shape=jax.ShapeDtypeStruct((n_rows, d), F32),
        scratch_shapes=[pltpu.SMEM((n_idx,), jnp.int32), pltpu.SemaphoreType.DMA, pltpu.SemaphoreType.DMA],
        input_output_aliases={2: 0},
        compiler_params=_params("arbitrary"),
        name="moe_dispatch",
    )(dest, t, jnp.zeros((n_rows, d), F32))

    blk = lambda i, be, nv: (jnp.minimum(i, nv[0] - 1), 0)
    ys = pl.pallas_call(
        _expert_kernel,
        grid_spec=pltpu.PrefetchScalarGridSpec(
            num_scalar_prefetch=2,
            grid=(n_blocks,),
            in_specs=[
                pl.BlockSpec((EXP_BLOCK, d), blk),
                pl.BlockSpec((None, d, hid), lambda i, be, nv: (be[i], 0, 0)),
                pl.BlockSpec((None, d, hid), lambda i, be, nv: (be[i], 0, 0)),
                pl.BlockSpec((None, hid, d), lambda i, be, nv: (be[i], 0, 0)),
            ],
            out_specs=pl.BlockSpec((EXP_BLOCK, d), lambda i, be, nv: (i, 0)),
        ),
        out_shape=jax.ShapeDtypeStruct((n_rows, d), F32),
        compiler_params=_params("arbitrary"),
        name="moe_experts",
    )(block_e, n_valid, xs, w1.astype(BF16), w3.astype(BF16), w2.astype(BF16))

    return pl.pallas_call(
        functools.partial(_combine_kernel, alpha=alpha),
        grid=(n_tiles,),
        in_specs=[pl.BlockSpec(memory_space=pl.ANY), pl.BlockSpec(memory_space=pl.ANY),
                  pl.BlockSpec((TOK_TILE, ROUTER_W), row), pl.BlockSpec((TOK_TILE, d), row),
                  pl.BlockSpec((None, 6, d), lambda i: (lay.mod_row(i, TOK_TILE), 0, 0)),
                  _const_spec((2, d))],
        out_specs=pl.BlockSpec((TOK_TILE, d), row),
        out_shape=jax.ShapeDtypeStruct((n, d), F32),
        scratch_shapes=[pltpu.SMEM((n_idx,), jnp.int32), pltpu.VMEM((n_idx, d), F32),
                        pltpu.SemaphoreType.DMA, pltpu.SemaphoreType.DMA],
        compiler_params=_params("arbitrary"),
        name="moe_combine",
    )(dest, ys, gate, h, mod, ln)


def kernel(x, c, ctx, c_ctx, ada_w, ada_b, ln_g, ln_b, ret_w_in, ret_decay, ret_gn_g, ret_w_out, mla_w_down, mla_q_norm, mla_kv_norm, mla_w_uq, mla_w_ukv, mla_w_out, conv_w_in, conv_w, conv_b, conv_w_out, moe_w_group, moe_b_group, moe_w_expert, moe_b_expert, moe_w1, moe_w3, moe_w2):
    b, s, d = x.shape
    depth = ada_w.shape[0]
    lay = _Layout(b, s, ctx.shape[1])
    alpha = (2.0 * depth) ** 0.25

    mod_rows = -(-(b + 1) // 8) * 8
    cc = jnp.concatenate([c, c_ctx[None, :], jnp.zeros((mod_rows - b - 1, d), F32)], axis=0)
    mod_all = _ada(cc, ada_w, ada_b).reshape(depth, mod_rows, 6, d)

    h = jnp.concatenate([ctx.reshape(lay.n_ctx, d), x.reshape(b * s, d)], axis=0)
    for i in range(depth):
        kind, j = i % N_MIXERS, i // N_MIXERS
        mod = mod_all[i]
        ln1 = jnp.stack([ln_g[i, 0], ln_b[i, 0]])
        ln2 = jnp.stack([ln_g[i, 1], ln_b[i, 1]])
        pad = jnp.zeros((d, ROUTER_W - MOE_GROUPS - MOE_EXPERTS), F32)
        w_router = jnp.concatenate([moe_w_group[i], moe_w_expert[i], pad], axis=1).astype(BF16)
        b_router = jnp.concatenate([moe_b_group[i], moe_b_expert[i], pad[0]]).reshape(1, ROUTER_W)
        if kind == 0:
            a = _retention(lay, h, mod, ret_w_in[j], ret_decay[j], ret_gn_g[j], d)
            h1, t, logits = _post(lay, a, h, mod, ret_w_out[j], ln1, w_router, b_router, alpha)
        elif kind == 1:
            a = _mla(lay, h, mod, mla_w_down[j], mla_q_norm[j], mla_kv_norm[j], mla_w_uq[j], mla_w_ukv[j], d)
            h1, t, logits = _post(lay, a, h, mod, mla_w_out[j], ln1, w_router, b_router, alpha)
        else:
            h1, t, logits = _conv_layer(lay, h, mod, conv_w_in[j], conv_w[j], conv_b[j], conv_w_out[j],
                                        ln1, w_router, b_router, alpha)
        h = _moe(lay, t, logits, h1, mod, ln2, moe_w1[i], moe_w3[i], moe_w2[i], alpha)
    return h[lay.n_ctx:].reshape(b, s, d)
```

```python
import functools

import jax
import jax.numpy as jnp
from jax import lax
from jax.experimental import pallas as pl
from jax.experimental.pallas import tpu as pltpu

F32 = jnp.float32
BF16 = jnp.bfloat16

GRID_W = 64
LN_EPS = 1e-5
RMS_EPS = 1e-6
ROPE_BASE = 10000.0
N_MIXERS = 3
RET_HEADS = 4
MLA_HEADS = 8
MLA_NOPE = 128
MLA_ROPE = 64
MLA_V = 128
MLA_QK = 256
MOE_GROUPS = 4
MOE_EPG = 8
MOE_EXPERTS = MOE_GROUPS * MOE_EPG
MOE_TOPK = 2
ROUTER_W = 128

CHUNK = 256
TM = 512
TQ = 512
TOK_TILE = 512
EXP_BLOCK = 512
RUN_ALIGN = 8
VMEM_LIMIT = 56 * 1024 * 1024


def _params(*sem):
    return pltpu.CompilerParams(dimension_semantics=sem, vmem_limit_bytes=VMEM_LIMIT)


def _const_spec(shape):
    nd = len(shape)
    return pl.BlockSpec(shape, lambda *_: (0,) * nd, pipeline_mode=pl.Buffered(1))


def _dot(a, b):
    return jnp.dot(a, b, preferred_element_type=F32)


def _dot_nt(a, b):
    return lax.dot_general(a, b, (((1,), (1,)), ((), ())), preferred_element_type=F32)


def _dot_tn(a, b):
    return lax.dot_general(a, b, (((0,), (0,)), ((), ())), preferred_element_type=F32)


def _silu(x):
    return x * jax.nn.sigmoid(x)


def _layer_norm(x, g, b):
    mu = jnp.mean(x, axis=-1, keepdims=True)
    xc = x - mu
    var = jnp.mean(xc * xc, axis=-1, keepdims=True)
    return xc * lax.rsqrt(var + LN_EPS) * g + b


def _rms_norm(x, g):
    return x * lax.rsqrt(jnp.mean(x * x, axis=-1, keepdims=True) + RMS_EPS) * g


def _ada_kernel(c_ref, w_ref, b_ref, o_ref):
    a = _silu(c_ref[...]).astype(BF16)
    o_ref[...] = _dot(a, w_ref[...].astype(BF16)) + b_ref[...]


def _ada(cc, ada_w, ada_b):
    depth, d, n6 = ada_w.shape
    rows = cc.shape[0]
    tn = 1536
    return pl.pallas_call(
        _ada_kernel,
        grid=(depth, n6 // tn),
        in_specs=[
            pl.BlockSpec((rows, d), lambda l, j: (0, 0)),
            pl.BlockSpec((None, d, tn), lambda l, j: (l, 0, j)),
            pl.BlockSpec((None, 1, tn), lambda l, j: (l, 0, j)),
        ],
        out_specs=pl.BlockSpec((None, rows, tn), lambda l, j: (l, 0, j)),
        out_shape=jax.ShapeDtypeStruct((depth, rows, n6), F32),
        compiler_params=_params("parallel", "parallel"),
        name="ada_mod",
    )(cc, ada_w, ada_b.reshape(depth, 1, n6))


class _Layout:
    def __init__(self, batch, seq, ctx_len):
        assert ctx_len == CHUNK and seq % TM == 0 and seq % CHUNK == 0
        self.batch, self.seq, self.ctx = batch, seq, ctx_len
        self.n_ctx = batch * ctx_len
        self.n = self.n_ctx + batch * seq
        assert self.n_ctx % TM == 0 and self.n_ctx % seq == 0 and self.n % TOK_TILE == 0

    def mod_row(self, tile, rows_per_tile):
        ctx_tiles = self.n_ctx // rows_per_tile
        per_batch = self.seq // rows_per_tile
        return jnp.where(tile < ctx_tiles, self.batch, (tile - ctx_tiles) // per_batch)

    def pos_block(self, tile, rows_per_tile):
        ctx_tiles = self.n_ctx // rows_per_tile
        per_batch = self.seq // rows_per_tile
        return jnp.where(tile < ctx_tiles, 0, 1 + (tile - ctx_tiles) % per_batch)


def _rope_angles(seq, dim):
    n_rows = seq // GRID_W
    rows = jnp.repeat(jnp.arange(n_rows, dtype=F32), GRID_W)
    cols = jnp.tile(jnp.arange(GRID_W, dtype=F32), n_rows)
    quarter = dim // 4
    inv_freq = ROPE_BASE ** (-jnp.arange(quarter, dtype=F32) / quarter)
    return jnp.concatenate([rows[:, None] * inv_freq, cols[:, None] * inv_freq], axis=-1)


def _with_identity_block(table, fill, rows):
    ident = jnp.full((rows, table.shape[1]), fill, F32)
    return jnp.concatenate([ident, table], axis=0)


def _proj_ret_kernel(h_ref, mod_ref, cos_ref, sin_ref, kdec_ref, wq_ref, wk_ref, wv_ref, wg_ref,
                     q_ref, k_ref, kf_ref, kb_ref, v_ref, g_ref, *, dk):
    u = (h_ref[...] * (1.0 + mod_ref[1:2, :]) + mod_ref[0:1, :]).astype(BF16)
    cos = cos_ref[...]
    sin = sin_ref[...]
    half = dk // 2
    yq = _dot(u, wq_ref[...])
    for hd in range(RET_HEADS):
        x1 = yq[:, hd * dk:hd * dk + half]
        x2 = yq[:, hd * dk + half:(hd + 1) * dk]
        q_ref[:, hd * dk:hd * dk + half] = (x1 * cos - x2 * sin).astype(BF16)
        q_ref[:, hd * dk + half:(hd + 1) * dk] = (x1 * sin + x2 * cos).astype(BF16)
    yk = _dot(u, wk_ref[...])
    k_scale = dk ** -0.5
    for hd in range(RET_HEADS):
        x1 = yk[:, hd * dk:hd * dk + half]
        x2 = yk[:, hd * dk + half:(hd + 1) * dk]
        df = kdec_ref[:, hd:hd + 1]
        db = kdec_ref[:, RET_HEADS + hd:RET_HEADS + hd + 1]
        for part, o in ((0, (x1 * cos - x2 * sin) * k_scale), (1, (x1 * sin + x2 * cos) * k_scale)):
            sl = slice(hd * dk + part * half, hd * dk + (part + 1) * half)
            k_ref[:, sl] = o.astype(BF16)
            kf_ref[:, sl] = (o * df).astype(BF16)
            kb_ref[:, sl] = (o * db).astype(BF16)
    v_ref[...] = _dot(u, wv_ref[...]).astype(BF16)
    g_ref[...] = _silu(_dot(u, wg_ref[...]))


def _ret_scan_kernel(cdec_ref, q_ref, k_ref, kd_ref, v_ref, intra_ref, qdec_ref, *rest, dk, dv, direction):
    if direction == 0:
        o_ref, state_ref = rest
    else:
        of_ref, g_ref, gn_ref, o_ref, state_ref = rest

    @pl.when(pl.program_id(1) == 0)
    def _():
        state_ref[...] = jnp.zeros_like(state_ref)

    for hd in range(RET_HEADS):
        q = q_ref[:, hd * dk:(hd + 1) * dk]
        k = k_ref[:, hd * dk:(hd + 1) * dk]
        kd = kd_ref[:, hd * dk:(hd + 1) * dk]
        v = v_ref[:, hd * dv:(hd + 1) * dv]
        p = (_dot_nt(q, k) * intra_ref[hd]).astype(BF16)
        st = state_ref[hd]
        col = direction * RET_HEADS + hd
        o = _dot(p, v) + qdec_ref[:, col:col + 1] * _dot(q, st.astype(BF16))
        state_ref[hd] = st * cdec_ref[col] + _dot_tn(kd, v)
        if direction == 0:
            o_ref[:, hd * dv:(hd + 1) * dv] = o
        else:
            o = o + of_ref[:, hd * dv:(hd + 1) * dv]
            mu = jnp.mean(o, axis=-1, keepdims=True)
            oc = o - mu
            var = jnp.mean(oc * oc, axis=-1, keepdims=True)
            on = oc * lax.rsqrt(var + LN_EPS) * gn_ref[:, hd * dv:(hd + 1) * dv]
            o_ref[:, hd * dv:(hd + 1) * dv] = (g_ref[:, hd * dv:(hd + 1) * dv] * on).astype(BF16)


def _retention(lay, h, mod, w_in, decay_logit, gn_g, d_model):
    n, b, nc = lay.n, lay.batch, lay.seq // CHUNK
    dk = d_model // RET_HEADS
    dv = 2 * dk
    qk, vw = RET_HEADS * dk, RET_HEADS * dv
    w_in = w_in.astype(BF16)
    wq, wk, wv, wg = w_in[:, :qk], w_in[:, qk:2 * qk], w_in[:, 2 * qk:2 * qk + vw], w_in[:, 2 * qk + vw:]

    ang = _rope_angles(lay.seq, dk)
    cos_t = _with_identity_block(jnp.cos(ang), 1.0, TM)
    sin_t = _with_identity_block(jnp.sin(ang), 0.0, TM)

    lg = jax.nn.log_sigmoid(decay_logit.astype(F32))
    idx = jnp.arange(CHUNK, dtype=F32)
    k_pow = jnp.stack([CHUNK - 1.0 - idx, idx])
    q_pow = jnp.stack([idx + 1.0, CHUNK - idx])
    kdec = jnp.exp(k_pow[:, :, None] * lg[:, None, :])
    qdec = jnp.exp(q_pow[:, :, None] * lg[:, None, :])
    kdec = jnp.moveaxis(kdec, 0, 1).reshape(CHUNK, 2 * RET_HEADS)
    qdec = jnp.moveaxis(qdec, 0, 1).reshape(CHUNK, 2 * RET_HEADS)
    cdec = jnp.exp(CHUNK * lg).reshape(2 * RET_HEADS)
    rel = idx[:, None] - idx[None, :]
    rel = jnp.stack([rel, -rel])
    intra = jnp.where(rel[:, None] >= 0, jnp.exp(jnp.maximum(rel[:, None], 0.0) * lg[:, :, None, None]), 0.0)
    kdec_tm = jnp.tile(kdec, (TM // CHUNK, 1))

    n_tiles = n // TM
    row = lambda i: (i, 0)
    q, k, kf, kb, v, g = pl.pallas_call(
        functools.partial(_proj_ret_kernel, dk=dk),
        grid=(n_tiles,),
        in_specs=[
            pl.BlockSpec((TM, d_model), row),
            pl.BlockSpec((None, 6, d_model), lambda i: (lay.mod_row(i, TM), 0, 0)),
            pl.BlockSpec((TM, dk // 2), lambda i: (lay.pos_block(i, TM), 0)),
            pl.BlockSpec((TM, dk // 2), lambda i: (lay.pos_block(i, TM), 0)),
            _const_spec((TM, 2 * RET_HEADS)),
            _const_spec((d_model, qk)), _const_spec((d_model, qk)),
            _const_spec((d_model, vw)), _const_spec((d_model, vw)),
        ],
        out_specs=[pl.BlockSpec((TM, qk), row)] * 4 + [pl.BlockSpec((TM, vw), row)] * 2,
        out_shape=[jax.ShapeDtypeStruct((n, qk), BF16)] * 4
        + [jax.ShapeDtypeStruct((n, vw), BF16), jax.ShapeDtypeStruct((n, vw), F32)],
        compiler_params=_params("parallel"),
        name="ret_proj",
    )(h, mod, cos_t, sin_t, kdec_tm, wq, wk, wv, wg)

    def chunk_fwd(bi, c):
        return (jnp.where(c == 0, bi, b + bi * nc + c - 1), 0)

    def chunk_bwd(bi, c):
        return (jnp.where(c == 0, bi, b + bi * nc + nc - c), 0)

    def scan(direction, chunk_map, kd, extra_in, extra_specs, out_dtype, name):
        return pl.pallas_call(
            functools.partial(_ret_scan_kernel, dk=dk, dv=dv, direction=direction),
            grid=(b, nc + 1),
            in_specs=[
                pl.BlockSpec(memory_space=pltpu.SMEM),
                pl.BlockSpec((CHUNK, qk), chunk_map),
                pl.BlockSpec((CHUNK, qk), chunk_map),
                pl.BlockSpec((CHUNK, qk), chunk_map),
                pl.BlockSpec((CHUNK, vw), chunk_map),
                _const_spec((RET_HEADS, CHUNK, CHUNK)),
                _const_spec((CHUNK, 2 * RET_HEADS)),
            ] + extra_specs,
            out_specs=pl.BlockSpec((CHUNK, vw), chunk_map),
            out_shape=jax.ShapeDtypeStruct((n, vw), out_dtype),
            scratch_shapes=[pltpu.VMEM((RET_HEADS, dk, dv), F32)],
            compiler_params=_params("parallel", "arbitrary"),
            name=name,
        )(cdec, q, k, kd, v, intra[direction], qdec, *extra_in)

    o_f = scan(0, chunk_fwd, kf, [], [], F32, "ret_scan_fwd")
    return scan(1, chunk_bwd, kb, [o_f, g, gn_g.reshape(1, vw).astype(F32)],
                [pl.BlockSpec((CHUNK, vw), chunk_bwd), pl.BlockSpec((CHUNK, vw), chunk_bwd),
                 _const_spec((1, vw))], BF16, "ret_scan_bwd")


def _proj_mla_kernel(h_ref, mod_ref, ct_ref, st_ref, wdq_ref, wdkv_ref, wkr_ref, qn_ref, kvn_ref,
                     wuq_ref, wukn_ref, wuv_ref, q_ref, k_ref, vt_ref):
    u = (h_ref[...] * (1.0 + mod_ref[1:2, :]) + mod_ref[0:1, :]).astype(BF16)
    ct = ct_ref[...]
    st = st_ref[...]
    half = MLA_QK // 2

    def rope(x):
        return x * ct + pltpu.roll(x, half // 2, 1) * st

    cq = _rms_norm(_dot(u, wdq_ref[...]), qn_ref[...]).astype(BF16)
    yq = _dot(cq, wuq_ref[...])
    ckv = _rms_norm(_dot(u, wdkv_ref[...]), kvn_ref[...]).astype(BF16)
    kn = _dot(ckv, wukn_ref[...])
    kr = rope(_dot(u, wkr_ref[...])).astype(BF16)
    for hd in range(MLA_HEADS):
        q_ref[:, hd * MLA_QK:hd * MLA_QK + half] = yq[:, hd * MLA_QK:hd * MLA_QK + half].astype(BF16)
        q_ref[:, hd * MLA_QK + half:(hd + 1) * MLA_QK] = rope(
            yq[:, hd * MLA_QK + half:(hd + 1) * MLA_QK]).astype(BF16)
        k_ref[:, hd * MLA_QK:hd * MLA_QK + half] = kn[:, hd * half:(hd + 1) * half].astype(BF16)
        k_ref[:, hd * MLA_QK + half:(hd + 1) * MLA_QK] = kr
    vt_ref[...] = _dot(ckv, wuv_ref[...]).T.astype(BF16)


KV_CHUNK = 512


def _attn_kernel(q_ref, *rest, scale, n_kv):
    kv_refs, o_ref, s_ref = rest[:2 * n_kv], rest[-2], rest[-1]
    q = q_ref[...]
    chunks = []
    row = 0
    for j in range(n_kv):
        keys = kv_refs[2 * j].shape[0]
        step = min(keys, KV_CHUNK)
        for off in range(0, keys, step):
            chunks.append((j, off, row, step))
            row += step
    m = None
    for j, off, row, w in chunks:
        s = _dot_nt(kv_refs[2 * j][off:off + w, :], q)
        s_ref[row:row + w, :] = s
        cm = jnp.max(s, axis=0, keepdims=True)
        m = cm if m is None else jnp.maximum(m, cm)
    acc = None
    den = None
    for j, off, row, w in chunks:
        p = jnp.exp((s_ref[row:row + w, :] - m) * scale)
        cs = jnp.sum(p, axis=0, keepdims=True)
        den = cs if den is None else den + cs
        pv = _dot(kv_refs[2 * j + 1][:, off:off + w], p.astype(BF16))
        acc = pv if acc is None else acc + pv
    o_ref[...] = (acc / den).T.astype(BF16)


def _pad_rope_cols(w):
    z = jnp.zeros((w.shape[0], MLA_ROPE // 2), w.dtype)
    return jnp.concatenate([w[:, :MLA_ROPE // 2], z, w[:, MLA_ROPE // 2:], z], axis=1)


def _mla(lay, h, mod, w_down, q_norm, kv_norm, w_uq, w_ukv, d_model):
    n, b, s = lay.n, lay.batch, lay.seq
    q_lora, kv_lora = q_norm.shape[0], kv_norm.shape[0]
    w_dq = w_down[:, :q_lora].astype(BF16)
    w_dkv = w_down[:, q_lora:q_lora + kv_lora].astype(BF16)
    w_kr = _pad_rope_cols(w_down[:, q_lora + kv_lora:]).astype(BF16)
    w_uq = w_uq.reshape(q_lora, MLA_HEADS, MLA_NOPE + MLA_ROPE)
    w_uq = jnp.concatenate(
        [w_uq[:, :, :MLA_NOPE], jax.vmap(_pad_rope_cols, 1, 1)(w_uq[:, :, MLA_NOPE:])], axis=2)
    w_uq = w_uq.reshape(q_lora, MLA_HEADS * MLA_QK).astype(BF16)
    w_ukv = w_ukv.reshape(kv_lora, MLA_HEADS, MLA_NOPE + MLA_V)
    w_ukn = w_ukv[:, :, :MLA_NOPE].reshape(kv_lora, MLA_HEADS * MLA_NOPE).astype(BF16)
    w_uv = w_ukv[:, :, MLA_NOPE:].reshape(kv_lora, MLA_HEADS * MLA_V).astype(BF16)

    ang = _rope_angles(s, MLA_ROPE)
    z = jnp.zeros_like(ang)
    ct = _with_identity_block(jnp.concatenate([jnp.cos(ang), z, jnp.cos(ang), z], axis=1), 1.0, TM)
    st = _with_identity_block(jnp.concatenate([-jnp.sin(ang), z, jnp.sin(ang), z], axis=1), 0.0, TM)

    row = lambda i: (i, 0)
    qw, vw = MLA_HEADS * MLA_QK, MLA_HEADS * MLA_V
    q, k, vt = pl.pallas_call(
        _proj_mla_kernel,
        grid=(n // TM,),
        in_specs=[
            pl.BlockSpec((TM, d_model), row),
            pl.BlockSpec((None, 6, d_model), lambda i: (lay.mod_row(i, TM), 0, 0)),
            pl.BlockSpec((TM, MLA_QK // 2), lambda i: (lay.pos_block(i, TM), 0)),
            pl.BlockSpec((TM, MLA_QK // 2), lambda i: (lay.pos_block(i, TM), 0)),
            _const_spec((d_model, q_lora)), _const_spec((d_model, kv_lora)), _const_spec((d_model, MLA_QK // 2)),
            _const_spec((1, q_lora)), _const_spec((1, kv_lora)),
            _const_spec((q_lora, qw)), _const_spec((kv_lora, vw)), _const_spec((kv_lora, vw)),
        ],
        out_specs=[pl.BlockSpec((TM, qw), row), pl.BlockSpec((TM, qw), row),
                   pl.BlockSpec((vw, TM), lambda i: (0, i))],
        out_shape=[jax.ShapeDtypeStruct((n, qw), BF16), jax.ShapeDtypeStruct((n, qw), BF16),
                   jax.ShapeDtypeStruct((vw, n), BF16)],
        compiler_params=_params("parallel"),
        name="mla_proj",
    )(h, mod, ct, st, w_dq, w_dkv, w_kr, q_norm.reshape(1, -1).astype(F32), kv_norm.reshape(1, -1).astype(F32),
      w_uq, w_ukn, w_uv)

    scale = (MLA_NOPE + MLA_ROPE) ** -0.5
    ctx_q = lambda bi, hd: (bi, hd)
    att_ctx = pl.pallas_call(
        functools.partial(_attn_kernel, scale=scale, n_kv=1),
        grid=(b, MLA_HEADS),
        in_specs=[pl.BlockSpec((CHUNK, MLA_QK), ctx_q), pl.BlockSpec((CHUNK, MLA_QK), ctx_q),
                  pl.BlockSpec((MLA_V, CHUNK), lambda bi, hd: (hd, bi))],
        out_specs=pl.BlockSpec((CHUNK, MLA_V), ctx_q),
        out_shape=jax.ShapeDtypeStruct((lay.n_ctx, vw), BF16),
        scratch_shapes=[pltpu.VMEM((CHUNK, CHUNK), F32)],
        compiler_params=_params("parallel", "parallel"),
        name="mla_attn_ctx",
    )(q, k, vt)

    lat_blk = lay.n_ctx // s
    q_tiles = s // TQ
    lat_q = lambda bi, hd, t: (lay.n_ctx // TQ + bi * q_tiles + t, hd)
    ctx_kv = lambda bi, hd, t: (bi, hd)
    lat_kv = lambda bi, hd, t: (lat_blk + bi, hd)
    att_lat = pl.pallas_call(
        functools.partial(_attn_kernel, scale=scale, n_kv=2),
        grid=(b, MLA_HEADS, q_tiles),
        in_specs=[pl.BlockSpec((TQ, MLA_QK), lat_q),
                  pl.BlockSpec((CHUNK, MLA_QK), ctx_kv),
                  pl.BlockSpec((MLA_V, CHUNK), lambda bi, hd, t: (hd, bi)),
                  pl.BlockSpec((s, MLA_QK), lat_kv),
                  pl.BlockSpec((MLA_V, s), lambda bi, hd, t: (hd, lat_blk + bi))],
        out_specs=pl.BlockSpec((TQ, MLA_V), lambda bi, hd, t: (bi * q_tiles + t, hd)),
        out_shape=jax.ShapeDtypeStruct((n - lay.n_ctx, vw), BF16),
        scratch_shapes=[pltpu.VMEM((CHUNK + s, TQ), F32)],
        compiler_params=_params("parallel", "parallel", "arbitrary"),
        name="mla_attn_lat",
    )(q, k, vt, k, vt)
    return att_ctx, att_lat


def _post_tail(y, h, mod_ref, ln_ref, wr_ref, br_ref, h_out, t_out, lg_out, alpha):
    h1 = _layer_norm(alpha * h + mod_ref[2:3, :] * y, ln_ref[0:1, :], ln_ref[1:2, :])
    t = h1 * (1.0 + mod_ref[4:5, :]) + mod_ref[3:4, :]
    h_out[...] = h1
    t_out[...] = t
    lg_out[...] = _dot(t.astype(BF16), wr_ref[...]) + br_ref[...]


def _post_kernel(*refs, alpha, ctx_tiles):
    a_refs, (h_ref, mod_ref, w_ref, ln_ref, wr_ref, br_ref, h_out, t_out, lg_out) = refs[:-9], refs[-9:]
    if len(a_refs) == 1:
        a = a_refs[0][...]
    else:
        a = jnp.where(pl.program_id(0) < ctx_tiles, a_refs[0][...], a_refs[1][...])
    y = _dot(a, w_ref[...])
    _post_tail(y, h_ref[...], mod_ref, ln_ref, wr_ref, br_ref, h_out, t_out, lg_out, alpha)


def _post(lay, a, h, mod, w_out, ln, w_router, b_router, alpha):
    n, d = h.shape
    row = lambda i: (i, 0)
    ctx_tiles = lay.n_ctx // TM
    if isinstance(a, tuple):
        ka = a[0].shape[1]
        a_specs = [pl.BlockSpec((TM, ka), lambda i: (jnp.minimum(i, ctx_tiles - 1), 0)),
                   pl.BlockSpec((TM, ka), lambda i: (jnp.maximum(i - ctx_tiles, 0), 0))]
    else:
        ka = a.shape[1]
        a_specs = [pl.BlockSpec((TM, ka), row)]
        a = (a,)
    return pl.pallas_call(
        functools.partial(_post_kernel, alpha=alpha, ctx_tiles=ctx_tiles),
        grid=(n // TM,),
        in_specs=a_specs + [
            pl.BlockSpec((TM, d), row),
            pl.BlockSpec((None, 6, d), lambda i: (lay.mod_row(i, TM), 0, 0)),
            _const_spec((ka, d)), _const_spec((2, d)), _const_spec((d, ROUTER_W)), _const_spec((1, ROUTER_W)),
        ],
        out_specs=[pl.BlockSpec((TM, d), row), pl.BlockSpec((TM, d), row), pl.BlockSpec((TM, ROUTER_W), row)],
        out_shape=[jax.ShapeDtypeStruct((n, d), F32), jax.ShapeDtypeStruct((n, d), F32),
                   jax.ShapeDtypeStruct((n, ROUTER_W), F32)],
        compiler_params=_params("parallel"),
        name="post_mixer",
    )(*a, h, mod, w_out.astype(BF16), ln, w_router, b_router)


HALO = 8


def _conv_kernel(h_ref, hp_ref, hn_ref, mod_ref, win_ref, cw_ref, cb_ref, w_ref, ln_ref, wr_ref, br_ref,
                 h_out, t_out, lg_out, *, alpha, tiles_per_seq, ctx_tiles):
    i = pl.program_id(0)
    d = h_ref.shape[1]
    h = h_ref[...]
    hx = jnp.concatenate([hp_ref[...], h, hn_ref[...]], axis=0)
    u = (hx * (1.0 + mod_ref[1:2, :]) + mod_ref[0:1, :]).astype(BF16)
    y = _dot(u, win_ref[...])
    gate_b = y[HALO:HALO + CHUNK, :d]
    zx = y[:, d:2 * d] * y[:, 2 * d:]
    z = zx[HALO:HALO + CHUNK]
    pos = jnp.where(i < ctx_tiles, 0, (i - ctx_tiles) % tiles_per_seq)
    last = jnp.where(i < ctx_tiles, 0, tiles_per_seq - 1)
    z_before = jnp.where(pos == 0, 0.0, zx[HALO - 1:HALO])
    z_after = jnp.where(pos == last, 0.0, zx[HALO + CHUNK:HALO + CHUNK + 1])
    r = lax.broadcasted_iota(jnp.int32, (CHUNK, 1), 0)
    z_prev = jnp.where(r == 0, z_before, pltpu.roll(z, 1, 0))
    z_next = jnp.where(r == CHUNK - 1, z_after, pltpu.roll(z, CHUNK - 1, 0))
    conv = cw_ref[0:1, :] * z_prev + cw_ref[1:2, :] * z + cw_ref[2:3, :] * z_next + cb_ref[...]
    yo = _dot((gate_b * conv).astype(BF16), w_ref[...])
    _post_tail(yo, h, mod_ref, ln_ref, wr_ref, br_ref, h_out, t_out, lg_out, alpha)


def _conv_layer(lay, h, mod, w_in, cw, cb, w_out, ln, w_router, b_router, alpha):
    n, d = h.shape
    n_tiles = n // CHUNK
    per = CHUNK // HALO
    row = lambda i: (i, 0)
    return pl.pallas_call(
        functools.partial(_conv_kernel, alpha=alpha, tiles_per_seq=lay.seq // CHUNK, ctx_tiles=lay.n_ctx // CHUNK),
        grid=(n_tiles,),
        in_specs=[
            pl.BlockSpec((CHUNK, d), row),
            pl.BlockSpec((HALO, d), lambda i: (jnp.maximum(i * per - 1, 0), 0)),
            pl.BlockSpec((HALO, d), lambda i: (jnp.minimum((i + 1) * per, n_tiles * per - 1), 0)),
            pl.BlockSpec((None, 6, d), lambda i: (lay.mod_row(i, CHUNK), 0, 0)),
            _const_spec((d, 3 * d)), _const_spec((3, d)), _const_spec((1, d)), _const_spec((d, d)),
            _const_spec((2, d)), _const_spec((d, ROUTER_W)), _const_spec((1, ROUTER_W)),
        ],
        out_specs=[pl.BlockSpec((CHUNK, d), row), pl.BlockSpec((CHUNK, d), row),
                   pl.BlockSpec((CHUNK, ROUTER_W), row)],
        out_shape=[jax.ShapeDtypeStruct((n, d), F32), jax.ShapeDtypeStruct((n, d), F32),
                   jax.ShapeDtypeStruct((n, ROUTER_W), F32)],
        compiler_params=_params("parallel"),
        name="conv_layer",
    )(h, h, h, mod, w_in.astype(BF16), cw.astype(F32), cb.reshape(1, d).astype(F32), w_out.astype(BF16),
      ln, w_router, b_router)


def _route_kernel(lg_ref, tri_ref, upper_ref, ext_ref, lp_ref, lpt_ref, tab_ref, cnt_ref, base_ref):
    @pl.when(pl.program_id(0) == 0)
    def _():
        base_ref[...] = jnp.zeros_like(base_ref)

    x = lg_ref[...]
    lane = lax.broadcasted_iota(jnp.int32, x.shape, 1)

    def softmax(mask):
        m = jnp.max(jnp.where(mask, x, -jnp.inf), axis=-1, keepdims=True)
        e = jnp.where(mask, jnp.exp(x - m), 0.0)
        return e / jnp.sum(e, axis=-1, keepdims=True)

    def top1(prob, mask):
        p = jnp.max(jnp.where(mask, prob, -1.0), axis=-1, keepdims=True)
        i = jnp.min(jnp.where(mask & (prob == p), lane, ROUTER_W), axis=-1, keepdims=True)
        return p, i

    g_mask = lane < MOE_GROUPS
    g_p, g_idx = top1(softmax(g_mask), g_mask)
    e_lo = MOE_GROUPS + MOE_EPG * g_idx
    e_mask = (lane >= e_lo) & (lane < e_lo + MOE_EPG)
    e_prob = softmax(e_mask)
    p1, i1 = top1(e_prob, e_mask)
    p2, i2 = top1(e_prob, e_mask & (lane != i1))
    denom = p1 + p2
    gate1 = g_p * p1 / denom
    gate2 = g_p * p2 / denom

    sel1 = lane == i1
    sel2 = lane == i2
    cnt = (sel1 | sel2).astype(F32)
    within = _dot(tri_ref[...], cnt.astype(BF16))
    cnt_tile = jnp.sum(cnt, axis=0, keepdims=True)
    ci = jnp.broadcast_to(cnt_tile, (8, ROUTER_W)).astype(jnp.int32)
    ci = (ci + (RUN_ALIGN - 1)) & ~(RUN_ALIGN - 1)
    run_tile = ci[0:1].astype(F32)
    start = (_dot((ci >> 4).astype(F32).astype(BF16), upper_ref[...]) * 16.0
             + _dot((ci & 15).astype(F32).astype(BF16), upper_ref[...]))[0:1]
    where_to = within + start
    lp1 = jnp.sum(jnp.where(sel1, where_to, 0.0), axis=-1, keepdims=True)
    lp2 = jnp.sum(jnp.where(sel2, where_to, 0.0), axis=-1, keepdims=True)

    before = base_ref[...]
    total = before + run_tile
    base_ref[...] = total
    cnt_ref[...] = jnp.broadcast_to(total, cnt_ref.shape)
    row = lax.broadcasted_iota(jnp.int32, tab_ref.shape, 0)
    tab_ref[...] = jnp.where(row == 0, run_tile, jnp.where(row == 1, before, jnp.where(row == 2, start, 0.0)))

    def pieces(g):
        hi = g.astype(BF16).astype(F32)
        mid = (g - hi).astype(BF16).astype(F32)
        return hi, mid, g - hi - mid

    a1, b1, c1 = pieces(gate1)
    a2, b2, c2 = pieces(gate2)
    cols = (a1, b1, c1, a2, b2, c2, (i1 - MOE_GROUPS).astype(F32))
    ext = jnp.zeros(x.shape, F32)
    for j, col in enumerate(cols):
        ext = jnp.where(lane == j, col, ext)
    ext_ref[...] = ext.astype(BF16)
    lp = jnp.where(lane == 0, lp1, jnp.where(lane == 1, lp2, 0.0))
    lp_ref[...] = lp
    lpt_ref[...] = lp.T[0:8, :]


def _route(logits):
    n = logits.shape[0]
    n_tiles = n // TOK_TILE
    r = lax.broadcasted_iota(jnp.int32, (TOK_TILE, TOK_TILE), 0)
    c = lax.broadcasted_iota(jnp.int32, (TOK_TILE, TOK_TILE), 1)
    tri = (c < r).astype(BF16)
    r = lax.broadcasted_iota(jnp.int32, (ROUTER_W, ROUTER_W), 0)
    c = lax.broadcasted_iota(jnp.int32, (ROUTER_W, ROUTER_W), 1)
    upper = (r < c).astype(BF16)
    row = lambda i: (i, 0)
    return pl.pallas_call(
        _route_kernel,
        grid=(n_tiles,),
        in_specs=[pl.BlockSpec((TOK_TILE, ROUTER_W), row), _const_spec((TOK_TILE, TOK_TILE)),
                  _const_spec((ROUTER_W, ROUTER_W))],
        out_specs=[pl.BlockSpec((TOK_TILE, ROUTER_W), row), pl.BlockSpec((TOK_TILE, ROUTER_W), row),
                   pl.BlockSpec((8, TOK_TILE), row), pl.BlockSpec((8, ROUTER_W), row),
                   pl.BlockSpec((8, ROUTER_W), lambda i: (0, 0))],
        out_shape=[jax.ShapeDtypeStruct((n, ROUTER_W), BF16), jax.ShapeDtypeStruct((n, ROUTER_W), F32),
                   jax.ShapeDtypeStruct((n_tiles * 8, TOK_TILE), F32),
                   jax.ShapeDtypeStruct((n_tiles * 8, ROUTER_W), F32),
                   jax.ShapeDtypeStruct((8, ROUTER_W), F32)],
        scratch_shapes=[pltpu.VMEM((1, ROUTER_W), F32)],
        compiler_params=_params("arbitrary"),
        name="moe_route",
    )(logits, tri, upper)


def _plan(tables, counts, n_blocks):
    n_tiles = tables.shape[0] // 8
    tables = tables.reshape(n_tiles, 8, ROUTER_W)[:, :3, MOE_GROUPS:MOE_GROUPS + MOE_EXPERTS].astype(jnp.int32)
    counts = counts[0, MOE_GROUPS:MOE_GROUPS + MOE_EXPERTS].astype(jnp.int32)
    padded = (counts + EXP_BLOCK - 1) // EXP_BLOCK * EXP_BLOCK
    pad_ends = jnp.cumsum(padded)
    pad_starts = pad_ends - padded
    run_len = tables[:, 0].reshape(-1)
    run_src = tables[:, 2].reshape(-1)
    run_dst = (tables[:, 1] + pad_starts[None, :]).reshape(-1)
    first_row = jnp.arange(n_blocks, dtype=jnp.int32) * EXP_BLOCK
    block_e = jnp.minimum(jnp.sum(first_row[:, None] >= pad_ends[None, :], axis=1),
                          MOE_EXPERTS - 1).astype(jnp.int32)
    n_valid = (pad_ends[-1] // EXP_BLOCK).astype(jnp.int32).reshape(1)
    clear_row = jnp.concatenate([pad_starts + counts, pad_ends[-1:]])
    clear_len = jnp.concatenate([padded - counts, n_blocks - n_valid])
    return run_len, run_src, run_dst, clear_row, clear_len, block_e, n_valid


RUN_BITS = tuple(b for b in (1 << k for k in range(TOK_TILE.bit_length() - 1, -1, -1)) if b >= RUN_ALIGN)
RARE_BITS = 3
SORT_ROWS = MOE_TOPK * TOK_TILE + MOE_EXPERTS * RUN_ALIGN


def _for_each_piece(length, src, dst, fn):
    def pieces(bits):
        for bit in bits:
            above = length & ~(2 * bit - 1)

            @pl.when((length & bit) != 0)
            def _(bit=bit, above=above):
                fn(pl.multiple_of(src + above, RUN_ALIGN), pl.multiple_of(dst + above, RUN_ALIGN), bit)

    @pl.when(length >= RUN_BITS[RARE_BITS - 1])
    def _():
        pieces(RUN_BITS[:RARE_BITS])
    pieces(RUN_BITS[RARE_BITS:])


def _dispatch_kernel(len_ref, src_ref, dst_ref, zrow_ref, zlen_ref, t_ref, ext_ref, lpt_ref, xs_ref,
                     buf, zeros, sem, zsem):
    i = pl.program_id(0)
    half = t_ref.shape[1] // 2

    @pl.when(i == 0)
    def _():
        zeros[...] = jnp.zeros_like(zeros)

        def clear(e, c):
            _for_each_piece(zlen_ref[e], 0, zrow_ref[e], lambda s, d, rows: pltpu.make_async_copy(
                zeros.at[pl.ds(0, rows)], xs_ref.at[pl.ds(d, rows)], zsem).start())
            return c
        lax.fori_loop(0, MOE_EXPERTS, clear, 0)

        def drain(e, c):
            _for_each_piece(zlen_ref[e], 0, zrow_ref[e], lambda s, d, rows: pltpu.make_async_copy(
                zeros.at[pl.ds(0, rows)], xs_ref.at[pl.ds(d, rows)], zsem).wait())
            return c
        lax.fori_loop(0, MOE_EXPERTS, drain, 0)

        def tail_copy(bk):
            row = pl.multiple_of(zrow_ref[MOE_EXPERTS] + bk * EXP_BLOCK, EXP_BLOCK)
            return pltpu.make_async_copy(zeros.at[pl.ds(0, EXP_BLOCK)], xs_ref.at[pl.ds(row, EXP_BLOCK)], zsem)

        def clear_tail(bk, c):
            tail_copy(bk).start()
            return c
        lax.fori_loop(0, zlen_ref[MOE_EXPERTS], clear_tail, 0)

        def drain_tail(bk, c):
            tail_copy(bk).wait()
            return c
        lax.fori_loop(0, zlen_ref[MOE_EXPERTS], drain_tail, 0)

    lp = lpt_ref[...].astype(jnp.int32)
    j = lax.broadcasted_iota(jnp.int32, (SORT_ROWS, TOK_TILE), 0)
    perm = ((j == lp[0:1, :]) | (j == lp[1:2, :])).astype(F32).astype(BF16)
    rhs = jnp.concatenate([t_ref[...].astype(BF16), ext_ref[...]], axis=1)
    srt = _dot(perm, rhs)
    bits = lax.bitcast_convert_type(srt, jnp.uint32)
    buf[:, :half] = (bits[:, :half] >> 16) | (bits[:, half:2 * half] & jnp.uint32(0xFFFF0000))
    buf[:, half:] = bits[:, 2 * half:]

    def copy(s, d, rows):
        return pltpu.make_async_copy(buf.at[pl.ds(s, rows)], xs_ref.at[pl.ds(d, rows)], sem)

    def issue(e, c):
        k = i * MOE_EXPERTS + e
        _for_each_piece(len_ref[k], src_ref[k], dst_ref[k], lambda s, d, rows: copy(s, d, rows).start())
        return c
    lax.fori_loop(0, MOE_EXPERTS, issue, 0)

    def drain(e, c):
        k = i * MOE_EXPERTS + e
        _for_each_piece(len_ref[k], src_ref[k], dst_ref[k], lambda s, d, rows: copy(s, d, rows).wait())
        return c
    lax.fori_loop(0, MOE_EXPERTS, drain, 0)


def _expert_kernel(be_ref, nv_ref, x_ref, w1_ref, w3_ref, w2_ref, o_ref):
    i = pl.program_id(0)
    valid = i < nv_ref[0]
    half = o_ref.shape[1] // 2

    @pl.when(valid)
    def _():
        packed = x_ref[:, :half]
        lo = lax.bitcast_convert_type(packed << 16, F32)
        hi = lax.bitcast_convert_type(packed & jnp.uint32(0xFFFF0000), F32)
        x = jnp.concatenate([lo, hi], axis=1).astype(BF16)
        ext = lax.bitcast_convert_type(x_ref[:, half:], F32)
        g1 = ext[:, 0:1] + ext[:, 1:2] + ext[:, 2:3]
        g2 = ext[:, 3:4] + ext[:, 4:5] + ext[:, 5:6]
        gate = jnp.where(ext[:, 6:7] == be_ref[i].astype(F32), g1, g2)
        hdn = _silu(_dot(x, w1_ref[...])) * _dot(x, w3_ref[...])
        o_ref[...] = _dot(hdn.astype(BF16), w2_ref[...]) * gate

    @pl.when(jnp.logical_not(valid))
    def _():
        o_ref[...] = jnp.zeros_like(o_ref)


def _combine_kernel(len_ref, src_ref, dst_ref, ys_ref, lp_ref, h_ref, mod_ref, ln_ref, o_ref, buf, sem,
                    *, alpha):
    i = pl.program_id(0)

    @pl.when(i == 0)
    def _():
        buf[...] = jnp.zeros_like(buf)

    def copy(s, d, rows):
        return pltpu.make_async_copy(ys_ref.at[pl.ds(d, rows)], buf.at[pl.ds(s, rows)], sem)

    def issue(e, c):
        k = i * MOE_EXPERTS + e
        _for_each_piece(len_ref[k], src_ref[k], dst_ref[k], lambda s, d, rows: copy(s, d, rows).start())
        return c
    lax.fori_loop(0, MOE_EXPERTS, issue, 0)

    def drain(e, c):
        k = i * MOE_EXPERTS + e
        _for_each_piece(len_ref[k], src_ref[k], dst_ref[k], lambda s, d, rows: copy(s, d, rows).wait())
        return c
    lax.fori_loop(0, MOE_EXPERTS, drain, 0)

    lp = lp_ref[...].astype(jnp.int32)
    j = lax.broadcasted_iota(jnp.int32, (TOK_TILE, SORT_ROWS), 1)
    pick = ((j == lp[:, 0:1]) | (j == lp[:, 1:2])).astype(F32).astype(BF16)
    y = buf[...]
    hi = y.astype(BF16)
    r1 = y - hi.astype(F32)
    mid = r1.astype(BF16)
    lo = (r1 - mid.astype(F32)).astype(BF16)
    f = _dot(pick, hi) + _dot(pick, mid) + _dot(pick, lo)
    o_ref[...] = _layer_norm(alpha * h_ref[...] + mod_ref[5:6, :] * f, ln_ref[0:1, :], ln_ref[1:2, :])


def _moe(lay, t, logits, h, mod, ln, w1, w3, w2, alpha):
    n, d = t.shape
    hid = w1.shape[2]
    n_tiles = n // TOK_TILE
    max_rows = n * MOE_TOPK + n_tiles * MOE_EXPERTS * (RUN_ALIGN - 1)
    n_blocks = -(-max_rows // EXP_BLOCK) + MOE_EXPERTS
    n_rows = n_blocks * EXP_BLOCK
    xs_w = d // 2 + ROUTER_W
    ext, lp, lpt, tables, counts = _route(logits)
    run_len, run_src, run_dst, zrow, zlen, block_e, n_valid = _plan(tables, counts, n_blocks)

    xs = pl.pallas_call(
        _dispatch_kernel,
        grid_spec=pltpu.PrefetchScalarGridSpec(
            num_scalar_prefetch=5,
            grid=(n_tiles,),
            in_specs=[pl.BlockSpec((TOK_TILE, d), lambda i, *_: (i, 0)),
                      pl.BlockSpec((TOK_TILE, ROUTER_W), lambda i, *_: (i, 0)),
                      pl.BlockSpec((8, TOK_TILE), lambda i, *_: (i, 0))],
            out_specs=pl.BlockSpec(memory_space=pl.ANY),
            scratch_shapes=[pltpu.VMEM((SORT_ROWS, xs_w), jnp.uint32),
                            pltpu.VMEM((max(TOK_TILE, EXP_BLOCK), xs_w), jnp.uint32),
                            pltpu.SemaphoreType.DMA, pltpu.SemaphoreType.DMA],
        ),
        out_shape=jax.ShapeDtypeStruct((n_rows, xs_w), jnp.uint32),
        compiler_params=_params("arbitrary"),
        name="moe_dispatch",
    )(run_len, run_src, run_dst, zrow, zlen, t, ext, lpt)

    blk = lambda i, be, nv: (jnp.minimum(i, nv[0] - 1), 0)
    ys = pl.pallas_call(
        _expert_kernel,
        grid_spec=pltpu.PrefetchScalarGridSpec(
            num_scalar_prefetch=2,
            grid=(n_blocks,),
            in_specs=[
                pl.BlockSpec((EXP_BLOCK, xs_w), blk),
                pl.BlockSpec((None, d, hid), lambda i, be, nv: (be[i], 0, 0)),
                pl.BlockSpec((None, d, hid), lambda i, be, nv: (be[i], 0, 0)),
                pl.BlockSpec((None, hid, d), lambda i, be, nv: (be[i], 0, 0)),
            ],
            out_specs=pl.BlockSpec((EXP_BLOCK, d), lambda i, be, nv: (i, 0)),
        ),
        out_shape=jax.ShapeDtypeStruct((n_rows, d), F32),
        compiler_params=_params("arbitrary"),
        name="moe_experts",
    )(block_e, n_valid, xs, w1.astype(BF16), w3.astype(BF16), w2.astype(BF16))

    return pl.pallas_call(
        functools.partial(_combine_kernel, alpha=alpha),
        grid_spec=pltpu.PrefetchScalarGridSpec(
            num_scalar_prefetch=3,
            grid=(n_tiles,),
            in_specs=[pl.BlockSpec(memory_space=pl.ANY),
                      pl.BlockSpec((TOK_TILE, ROUTER_W), lambda i, *_: (i, 0)),
                      pl.BlockSpec((TOK_TILE, d), lambda i, *_: (i, 0)),
                      pl.BlockSpec((None, 6, d), lambda i, *_: (lay.mod_row(i, TOK_TILE), 0, 0)),
                      pl.BlockSpec((2, d), lambda i, *_: (0, 0))],
            out_specs=pl.BlockSpec((TOK_TILE, d), lambda i, *_: (i, 0)),
            scratch_shapes=[pltpu.VMEM((SORT_ROWS, d), F32), pltpu.SemaphoreType.DMA],
        ),
        out_shape=jax.ShapeDtypeStruct((n, d), F32),
        compiler_params=_params("arbitrary"),
        name="moe_combine",
    )(run_len, run_src, run_dst, ys, lp, h, mod, ln)


def kernel(x, c, ctx, c_ctx, ada_w, ada_b, ln_g, ln_b, ret_w_in, ret_decay, ret_gn_g, ret_w_out, mla_w_down, mla_q_norm, mla_kv_norm, mla_w_uq, mla_w_ukv, mla_w_out, conv_w_in, conv_w, conv_b, conv_w_out, moe_w_group, moe_b_group, moe_w_expert, moe_b_expert, moe_w1, moe_w3, moe_w2):
    b, s, d = x.shape
    depth = ada_w.shape[0]
    lay = _Layout(b, s, ctx.shape[1])
    alpha = (2.0 * depth) ** 0.25

    mod_rows = -(-(b + 1) // 8) * 8
    cc = jnp.concatenate([c, c_ctx[None, :], jnp.zeros((mod_rows - b - 1, d), F32)], axis=0)
    mod_all = _ada(cc, ada_w, ada_b).reshape(depth, mod_rows, 6, d)

    h = jnp.concatenate([ctx.reshape(lay.n_ctx, d), x.reshape(b * s, d)], axis=0)
    for i in range(depth):
        kind, j = i % N_MIXERS, i // N_MIXERS
        mod = mod_all[i]
        ln1 = jnp.stack([ln_g[i, 0], ln_b[i, 0]])
        ln2 = jnp.stack([ln_g[i, 1], ln_b[i, 1]])
        pad = jnp.zeros((d, ROUTER_W - MOE_GROUPS - MOE_EXPERTS), F32)
        w_router = jnp.concatenate([moe_w_group[i], moe_w_expert[i], pad], axis=1).astype(BF16)
        b_router = jnp.concatenate([moe_b_group[i], moe_b_expert[i], pad[0]]).reshape(1, ROUTER_W)
        if kind == 0:
            a = _retention(lay, h, mod, ret_w_in[j], ret_decay[j], ret_gn_g[j], d)
            h1, t, logits = _post(lay, a, h, mod, ret_w_out[j], ln1, w_router, b_router, alpha)
        elif kind == 1:
            a = _mla(lay, h, mod, mla_w_down[j], mla_q_norm[j], mla_kv_norm[j], mla_w_uq[j], mla_w_ukv[j], d)
            h1, t, logits = _post(lay, a, h, mod, mla_w_out[j], ln1, w_router, b_router, alpha)
        else:
            h1, t, logits = _conv_layer(lay, h, mod, conv_w_in[j], conv_w[j], conv_b[j], conv_w_out[j],
                                        ln1, w_router, b_router, alpha)
        h = _moe(lay, t, logits, h1, mod, ln2, moe_w1[i], moe_w3[i], moe_w2[i], alpha)
    return h[lay.n_ctx:].reshape(b, s, d)
```

```python
import functools

import jax
import jax.numpy as jnp
from jax import lax
from jax.experimental import pallas as pl
from jax.experimental.pallas import tpu as pltpu

F32 = jnp.float32
BF16 = jnp.bfloat16

GRID_W = 64
LN_EPS = 1e-5
RMS_EPS = 1e-6
ROPE_BASE = 10000.0
N_MIXERS = 3
RET_HEADS = 4
MLA_HEADS = 8
MLA_NOPE = 128
MLA_ROPE = 64
MLA_V = 128
MLA_QK = 256
MOE_GROUPS = 4
MOE_EPG = 8
MOE_EXPERTS = MOE_GROUPS * MOE_EPG
MOE_TOPK = 2
ROUTER_W = 128

CHUNK = 256
TM = 512
TQ = 1024
TOK_TILE = 512
EXP_BLOCK = 512
RUN_ALIGN = 8
VMEM_LIMIT = 56 * 1024 * 1024


def _params(*sem):
    return pltpu.CompilerParams(dimension_semantics=sem, vmem_limit_bytes=VMEM_LIMIT)


def _const_spec(shape):
    nd = len(shape)
    return pl.BlockSpec(shape, lambda *_: (0,) * nd, pipeline_mode=pl.Buffered(1))


def _dot(a, b):
    return jnp.dot(a, b, preferred_element_type=F32)


def _dot_nt(a, b):
    return lax.dot_general(a, b, (((1,), (1,)), ((), ())), preferred_element_type=F32)


def _dot_tn(a, b):
    return lax.dot_general(a, b, (((0,), (0,)), ((), ())), preferred_element_type=F32)


def _silu(x):
    return x * jax.nn.sigmoid(x)


def _layer_norm(x, g, b):
    mu = jnp.mean(x, axis=-1, keepdims=True)
    xc = x - mu
    var = jnp.mean(xc * xc, axis=-1, keepdims=True)
    return xc * lax.rsqrt(var + LN_EPS) * g + b


def _rms_norm(x, g):
    return x * lax.rsqrt(jnp.mean(x * x, axis=-1, keepdims=True) + RMS_EPS) * g


def _ada_kernel(c_ref, w_ref, b_ref, o_ref):
    a = _silu(c_ref[...]).astype(BF16)
    o_ref[...] = _dot(a, w_ref[...].astype(BF16)) + b_ref[...]


def _ada(cc, ada_w, ada_b):
    depth, d, n6 = ada_w.shape
    rows = cc.shape[0]
    tn = 1536
    return pl.pallas_call(
        _ada_kernel,
        grid=(depth, n6 // tn),
        in_specs=[
            pl.BlockSpec((rows, d), lambda l, j: (0, 0)),
            pl.BlockSpec((None, d, tn), lambda l, j: (l, 0, j)),
            pl.BlockSpec((None, 1, tn), lambda l, j: (l, 0, j)),
        ],
        out_specs=pl.BlockSpec((None, rows, tn), lambda l, j: (l, 0, j)),
        out_shape=jax.ShapeDtypeStruct((depth, rows, n6), F32),
        compiler_params=_params("parallel", "parallel"),
        name="ada_mod",
    )(cc, ada_w, ada_b.reshape(depth, 1, n6))


class _Layout:
    def __init__(self, batch, seq, ctx_len):
        assert ctx_len == CHUNK and seq % TM == 0 and seq % CHUNK == 0
        self.batch, self.seq, self.ctx = batch, seq, ctx_len
        self.n_ctx = batch * ctx_len
        self.n = self.n_ctx + batch * seq
        assert self.n_ctx % TM == 0 and self.n_ctx % seq == 0 and self.n_ctx % TOK_TILE == 0
        assert self.n_ctx % TQ == 0 and seq % TQ == 0 and seq % TOK_TILE == 0

    def mod_row(self, tile, rows_per_tile):
        ctx_tiles = self.n_ctx // rows_per_tile
        per_batch = self.seq // rows_per_tile
        return jnp.where(tile < ctx_tiles, self.batch, (tile - ctx_tiles) // per_batch)

    def pos_block(self, tile, rows_per_tile):
        ctx_tiles = self.n_ctx // rows_per_tile
        per_batch = self.seq // rows_per_tile
        return jnp.where(tile < ctx_tiles, 0, 1 + (tile - ctx_tiles) % per_batch)


def _rope_angles(seq, dim):
    n_rows = seq // GRID_W
    rows = jnp.repeat(jnp.arange(n_rows, dtype=F32), GRID_W)
    cols = jnp.tile(jnp.arange(GRID_W, dtype=F32), n_rows)
    quarter = dim // 4
    inv_freq = ROPE_BASE ** (-jnp.arange(quarter, dtype=F32) / quarter)
    return jnp.concatenate([rows[:, None] * inv_freq, cols[:, None] * inv_freq], axis=-1)


def _with_identity_block(table, fill, rows):
    ident = jnp.full((rows, table.shape[1]), fill, F32)
    return jnp.concatenate([ident, table], axis=0)


def _proj_ret_kernel(h_ref, mod_ref, cos_ref, sin_ref, kdec_ref, wq_ref, wk_ref, wv_ref, wg_ref,
                     q_ref, k_ref, kf_ref, kb_ref, v_ref, g_ref, *, dk):
    u = (h_ref[...] * (1.0 + mod_ref[1:2, :]) + mod_ref[0:1, :]).astype(BF16)
    cos = cos_ref[...]
    sin = sin_ref[...]
    half = dk // 2
    yq = _dot(u, wq_ref[...])
    for hd in range(RET_HEADS):
        x1 = yq[:, hd * dk:hd * dk + half]
        x2 = yq[:, hd * dk + half:(hd + 1) * dk]
        q_ref[:, hd * dk:hd * dk + half] = (x1 * cos - x2 * sin).astype(BF16)
        q_ref[:, hd * dk + half:(hd + 1) * dk] = (x1 * sin + x2 * cos).astype(BF16)
    yk = _dot(u, wk_ref[...])
    k_scale = dk ** -0.5
    for hd in range(RET_HEADS):
        x1 = yk[:, hd * dk:hd * dk + half]
        x2 = yk[:, hd * dk + half:(hd + 1) * dk]
        df = kdec_ref[:, hd:hd + 1]
        db = kdec_ref[:, RET_HEADS + hd:RET_HEADS + hd + 1]
        for part, o in ((0, (x1 * cos - x2 * sin) * k_scale), (1, (x1 * sin + x2 * cos) * k_scale)):
            sl = slice(hd * dk + part * half, hd * dk + (part + 1) * half)
            k_ref[:, sl] = o.astype(BF16)
            kf_ref[:, sl] = (o * df).astype(BF16)
            kb_ref[:, sl] = (o * db).astype(BF16)
    v_ref[...] = _dot(u, wv_ref[...]).astype(BF16)
    g_ref[...] = _silu(_dot(u, wg_ref[...]))


def _ret_scan_kernel(cdec_ref, q_ref, k_ref, kd_ref, v_ref, intra_ref, qdec_ref, *rest, dk, dv, direction):
    if direction == 0:
        o_ref, state_ref = rest
    else:
        of_ref, g_ref, gn_ref, o_ref, state_ref = rest

    @pl.when(pl.program_id(1) == 0)
    def _():
        state_ref[...] = jnp.zeros_like(state_ref)

    for hd in range(RET_HEADS):
        q = q_ref[:, hd * dk:(hd + 1) * dk]
        k = k_ref[:, hd * dk:(hd + 1) * dk]
        kd = kd_ref[:, hd * dk:(hd + 1) * dk]
        v = v_ref[:, hd * dv:(hd + 1) * dv]
        p = (_dot_nt(q, k) * intra_ref[hd]).astype(BF16)
        st = state_ref[hd]
        col = direction * RET_HEADS + hd
        o = _dot(p, v) + qdec_ref[:, col:col + 1] * _dot(q, st.astype(BF16))
        state_ref[hd] = st * cdec_ref[col] + _dot_tn(kd, v)
        if direction == 0:
            o_ref[:, hd * dv:(hd + 1) * dv] = o
        else:
            o = o + of_ref[:, hd * dv:(hd + 1) * dv]
            mu = jnp.mean(o, axis=-1, keepdims=True)
            oc = o - mu
            var = jnp.mean(oc * oc, axis=-1, keepdims=True)
            on = oc * lax.rsqrt(var + LN_EPS) * gn_ref[:, hd * dv:(hd + 1) * dv]
            o_ref[:, hd * dv:(hd + 1) * dv] = (g_ref[:, hd * dv:(hd + 1) * dv] * on).astype(BF16)


def _retention(lay, h, mod, w_in, decay_logit, gn_g, d_model):
    n, b, nc = lay.n, lay.batch, lay.seq // CHUNK
    dk = d_model // RET_HEADS
    dv = 2 * dk
    qk, vw = RET_HEADS * dk, RET_HEADS * dv
    w_in = w_in.astype(BF16)
    wq, wk, wv, wg = w_in[:, :qk], w_in[:, qk:2 * qk], w_in[:, 2 * qk:2 * qk + vw], w_in[:, 2 * qk + vw:]

    ang = _rope_angles(lay.seq, dk)
    cos_t = _with_identity_block(jnp.cos(ang), 1.0, TM)
    sin_t = _with_identity_block(jnp.sin(ang), 0.0, TM)

    lg = jax.nn.log_sigmoid(decay_logit.astype(F32))
    idx = jnp.arange(CHUNK, dtype=F32)
    k_pow = jnp.stack([CHUNK - 1.0 - idx, idx])
    q_pow = jnp.stack([idx + 1.0, CHUNK - idx])
    kdec = jnp.exp(k_pow[:, :, None] * lg[:, None, :])
    qdec = jnp.exp(q_pow[:, :, None] * lg[:, None, :])
    kdec = jnp.moveaxis(kdec, 0, 1).reshape(CHUNK, 2 * RET_HEADS)
    qdec = jnp.moveaxis(qdec, 0, 1).reshape(CHUNK, 2 * RET_HEADS)
    cdec = jnp.exp(CHUNK * lg).reshape(2 * RET_HEADS)
    rel = idx[:, None] - idx[None, :]
    rel = jnp.stack([rel, -rel])
    intra = jnp.where(rel[:, None] >= 0, jnp.exp(jnp.maximum(rel[:, None], 0.0) * lg[:, :, None, None]), 0.0)
    kdec_tm = jnp.tile(kdec, (TM // CHUNK, 1))

    n_tiles = n // TM
    row = lambda i: (i, 0)
    q, k, kf, kb, v, g = pl.pallas_call(
        functools.partial(_proj_ret_kernel, dk=dk),
        grid=(n_tiles,),
        in_specs=[
            pl.BlockSpec((TM, d_model), row),
            pl.BlockSpec((None, 6, d_model), lambda i: (lay.mod_row(i, TM), 0, 0)),
            pl.BlockSpec((TM, dk // 2), lambda i: (lay.pos_block(i, TM), 0)),
            pl.BlockSpec((TM, dk // 2), lambda i: (lay.pos_block(i, TM), 0)),
            _const_spec((TM, 2 * RET_HEADS)),
            _const_spec((d_model, qk)), _const_spec((d_model, qk)),
            _const_spec((d_model, vw)), _const_spec((d_model, vw)),
        ],
        out_specs=[pl.BlockSpec((TM, qk), row)] * 4 + [pl.BlockSpec((TM, vw), row)] * 2,
        out_shape=[jax.ShapeDtypeStruct((n, qk), BF16)] * 4
        + [jax.ShapeDtypeStruct((n, vw), BF16), jax.ShapeDtypeStruct((n, vw), F32)],
        compiler_params=_params("parallel"),
        name="ret_proj",
    )(h, mod, cos_t, sin_t, kdec_tm, wq, wk, wv, wg)

    def chunk_fwd(bi, c):
        return (jnp.where(c == 0, bi, b + bi * nc + c - 1), 0)

    def chunk_bwd(bi, c):
        return (jnp.where(c == 0, bi, b + bi * nc + nc - c), 0)

    def scan(direction, chunk_map, kd, extra_in, extra_specs, out_dtype, name):
        return pl.pallas_call(
            functools.partial(_ret_scan_kernel, dk=dk, dv=dv, direction=direction),
            grid=(b, nc + 1),
            in_specs=[
                pl.BlockSpec(memory_space=pltpu.SMEM),
                pl.BlockSpec((CHUNK, qk), chunk_map),
                pl.BlockSpec((CHUNK, qk), chunk_map),
                pl.BlockSpec((CHUNK, qk), chunk_map),
                pl.BlockSpec((CHUNK, vw), chunk_map),
                _const_spec((RET_HEADS, CHUNK, CHUNK)),
                _const_spec((CHUNK, 2 * RET_HEADS)),
            ] + extra_specs,
            out_specs=pl.BlockSpec((CHUNK, vw), chunk_map),
            out_shape=jax.ShapeDtypeStruct((n, vw), out_dtype),
            scratch_shapes=[pltpu.VMEM((RET_HEADS, dk, dv), F32)],
            compiler_params=_params("parallel", "arbitrary"),
            name=name,
        )(cdec, q, k, kd, v, intra[direction], qdec, *extra_in)

    o_f = scan(0, chunk_fwd, kf, [], [], F32, "ret_scan_fwd")
    return scan(1, chunk_bwd, kb, [o_f, g, gn_g.reshape(1, vw).astype(F32)],
                [pl.BlockSpec((CHUNK, vw), chunk_bwd), pl.BlockSpec((CHUNK, vw), chunk_bwd),
                 _const_spec((1, vw))], BF16, "ret_scan_bwd")


def _proj_mla_kernel(h_ref, mod_ref, ct_ref, st_ref, wdq_ref, wdkv_ref, wkr_ref, qn_ref, kvn_ref,
                     wuq_ref, wukn_ref, wuv_ref, q_ref, k_ref, vt_ref):
    u = (h_ref[...] * (1.0 + mod_ref[1:2, :]) + mod_ref[0:1, :]).astype(BF16)
    ct = ct_ref[...]
    st = st_ref[...]
    half = MLA_QK // 2

    def rope(x):
        return x * ct + pltpu.roll(x, half // 2, 1) * st

    cq = _rms_norm(_dot(u, wdq_ref[...]), qn_ref[...]).astype(BF16)
    yq = _dot(cq, wuq_ref[...])
    ckv = _rms_norm(_dot(u, wdkv_ref[...]), kvn_ref[...]).astype(BF16)
    kn = _dot(ckv, wukn_ref[...])
    kr = rope(_dot(u, wkr_ref[...])).astype(BF16)
    for hd in range(MLA_HEADS):
        q_ref[:, hd * MLA_QK:hd * MLA_QK + half] = yq[:, hd * MLA_QK:hd * MLA_QK + half].astype(BF16)
        q_ref[:, hd * MLA_QK + half:(hd + 1) * MLA_QK] = rope(
            yq[:, hd * MLA_QK + half:(hd + 1) * MLA_QK]).astype(BF16)
        k_ref[:, hd * MLA_QK:hd * MLA_QK + half] = kn[:, hd * half:(hd + 1) * half].astype(BF16)
        k_ref[:, hd * MLA_QK + half:(hd + 1) * MLA_QK] = kr
    vt_ref[...] = _dot(ckv, wuv_ref[...]).T.astype(BF16)


KV_CHUNK = 512
LOG2_E = 1.4426950408889634


def _attn_kernel(q_ref, *rest, scale, n_kv):
    kv_refs, o_ref, s_ref = rest[:2 * n_kv], rest[-2], rest[-1]
    q = q_ref[...]
    chunks = []
    row = 0
    for j in range(n_kv):
        keys = kv_refs[2 * j].shape[0]
        step = min(keys, KV_CHUNK)
        for off in range(0, keys, step):
            chunks.append((j, off, row, step))
            row += step
    m = None
    for j, off, row, w in chunks:
        s = _dot_nt(kv_refs[2 * j][off:off + w, :], q)
        s_ref[row:row + w, :] = s
        cm = jnp.max(s, axis=0, keepdims=True)
        m = cm if m is None else jnp.maximum(m, cm)
    acc = None
    den = None
    for j, off, row, w in chunks:
        p = jnp.exp2((s_ref[row:row + w, :] - m) * (scale * LOG2_E))
        cs = jnp.sum(p, axis=0, keepdims=True)
        den = cs if den is None else den + cs
        pv = _dot(kv_refs[2 * j + 1][:, off:off + w], p.astype(BF16))
        acc = pv if acc is None else acc + pv
    o_ref[...] = (acc / den).T.astype(BF16)


def _pad_rope_cols(w):
    z = jnp.zeros((w.shape[0], MLA_ROPE // 2), w.dtype)
    return jnp.concatenate([w[:, :MLA_ROPE // 2], z, w[:, MLA_ROPE // 2:], z], axis=1)


def _mla(lay, h, mod, w_down, q_norm, kv_norm, w_uq, w_ukv, d_model):
    n, b, s = lay.n, lay.batch, lay.seq
    q_lora, kv_lora = q_norm.shape[0], kv_norm.shape[0]
    w_dq = w_down[:, :q_lora].astype(BF16)
    w_dkv = w_down[:, q_lora:q_lora + kv_lora].astype(BF16)
    w_kr = _pad_rope_cols(w_down[:, q_lora + kv_lora:]).astype(BF16)
    w_uq = w_uq.reshape(q_lora, MLA_HEADS, MLA_NOPE + MLA_ROPE)
    w_uq = jnp.concatenate(
        [w_uq[:, :, :MLA_NOPE], jax.vmap(_pad_rope_cols, 1, 1)(w_uq[:, :, MLA_NOPE:])], axis=2)
    w_uq = w_uq.reshape(q_lora, MLA_HEADS * MLA_QK).astype(BF16)
    w_ukv = w_ukv.reshape(kv_lora, MLA_HEADS, MLA_NOPE + MLA_V)
    w_ukn = w_ukv[:, :, :MLA_NOPE].reshape(kv_lora, MLA_HEADS * MLA_NOPE).astype(BF16)
    w_uv = w_ukv[:, :, MLA_NOPE:].reshape(kv_lora, MLA_HEADS * MLA_V).astype(BF16)

    ang = _rope_angles(s, MLA_ROPE)
    z = jnp.zeros_like(ang)
    ct = _with_identity_block(jnp.concatenate([jnp.cos(ang), z, jnp.cos(ang), z], axis=1), 1.0, TM)
    st = _with_identity_block(jnp.concatenate([-jnp.sin(ang), z, jnp.sin(ang), z], axis=1), 0.0, TM)

    row = lambda i: (i, 0)
    qw, vw = MLA_HEADS * MLA_QK, MLA_HEADS * MLA_V
    q, k, vt = pl.pallas_call(
        _proj_mla_kernel,
        grid=(n // TM,),
        in_specs=[
            pl.BlockSpec((TM, d_model), row),
            pl.BlockSpec((None, 6, d_model), lambda i: (lay.mod_row(i, TM), 0, 0)),
            pl.BlockSpec((TM, MLA_QK // 2), lambda i: (lay.pos_block(i, TM), 0)),
            pl.BlockSpec((TM, MLA_QK // 2), lambda i: (lay.pos_block(i, TM), 0)),
            _const_spec((d_model, q_lora)), _const_spec((d_model, kv_lora)), _const_spec((d_model, MLA_QK // 2)),
            _const_spec((1, q_lora)), _const_spec((1, kv_lora)),
            _const_spec((q_lora, qw)), _const_spec((kv_lora, vw)), _const_spec((kv_lora, vw)),
        ],
        out_specs=[pl.BlockSpec((TM, qw), row), pl.BlockSpec((TM, qw), row),
                   pl.BlockSpec((vw, TM), lambda i: (0, i))],
        out_shape=[jax.ShapeDtypeStruct((n, qw), BF16), jax.ShapeDtypeStruct((n, qw), BF16),
                   jax.ShapeDtypeStruct((vw, n), BF16)],
        compiler_params=_params("parallel"),
        name="mla_proj",
    )(h, mod, ct, st, w_dq, w_dkv, w_kr, q_norm.reshape(1, -1).astype(F32), kv_norm.reshape(1, -1).astype(F32),
      w_uq, w_ukn, w_uv)

    scale = (MLA_NOPE + MLA_ROPE) ** -0.5
    ctx_q = lambda bi, hd: (bi, hd)
    att_ctx = pl.pallas_call(
        functools.partial(_attn_kernel, scale=scale, n_kv=1),
        grid=(b, MLA_HEADS),
        in_specs=[pl.BlockSpec((CHUNK, MLA_QK), ctx_q), pl.BlockSpec((CHUNK, MLA_QK), ctx_q),
                  pl.BlockSpec((MLA_V, CHUNK), lambda bi, hd: (hd, bi))],
        out_specs=pl.BlockSpec((CHUNK, MLA_V), ctx_q),
        out_shape=jax.ShapeDtypeStruct((lay.n_ctx, vw), BF16),
        scratch_shapes=[pltpu.VMEM((CHUNK, CHUNK), F32)],
        compiler_params=_params("parallel", "parallel"),
        name="mla_attn_ctx",
    )(q, k, vt)

    lat_blk = lay.n_ctx // s
    q_tiles = s // TQ
    lat_q = lambda bi, hd, t: (lay.n_ctx // TQ + bi * q_tiles + t, hd)
    ctx_kv = lambda bi, hd, t: (bi, hd)
    lat_kv = lambda bi, hd, t: (lat_blk + bi, hd)
    att_lat = pl.pallas_call(
        functools.partial(_attn_kernel, scale=scale, n_kv=2),
        grid=(b, MLA_HEADS, q_tiles),
        in_specs=[pl.BlockSpec((TQ, MLA_QK), lat_q),
                  pl.BlockSpec((CHUNK, MLA_QK), ctx_kv),
                  pl.BlockSpec((MLA_V, CHUNK), lambda bi, hd, t: (hd, bi)),
                  pl.BlockSpec((s, MLA_QK), lat_kv),
                  pl.BlockSpec((MLA_V, s), lambda bi, hd, t: (hd, lat_blk + bi))],
        out_specs=pl.BlockSpec((TQ, MLA_V), lambda bi, hd, t: (bi * q_tiles + t, hd)),
        out_shape=jax.ShapeDtypeStruct((n - lay.n_ctx, vw), BF16),
        scratch_shapes=[pltpu.VMEM((CHUNK + s, TQ), F32)],
        compiler_params=_params("parallel", "parallel", "arbitrary"),
        name="mla_attn_lat",
    )(q, k, vt, k, vt)
    return att_ctx, att_lat


def _post_tail(y, h, mod_ref, ln_ref, wr_ref, br_ref, h_out, t_out, lg_out, alpha):
    h1 = _layer_norm(alpha * h + mod_ref[2:3, :] * y, ln_ref[0:1, :], ln_ref[1:2, :])
    t = h1 * (1.0 + mod_ref[4:5, :]) + mod_ref[3:4, :]
    h_out[...] = h1
    t_out[...] = t
    lg_out[...] = _dot(t.astype(BF16), wr_ref[...]) + br_ref[...]


def _post_kernel(*refs, alpha, ctx_tiles):
    a_refs, (h_ref, mod_ref, w_ref, ln_ref, wr_ref, br_ref, h_out, t_out, lg_out) = refs[:-9], refs[-9:]
    if len(a_refs) == 1:
        a = a_refs[0][...]
    else:
        a = jnp.where(pl.program_id(0) < ctx_tiles, a_refs[0][...], a_refs[1][...])
    y = _dot(a, w_ref[...])
    _post_tail(y, h_ref[...], mod_ref, ln_ref, wr_ref, br_ref, h_out, t_out, lg_out, alpha)


def _post(lay, a, h, mod, w_out, ln, w_router, b_router, alpha):
    n, d = h.shape
    row = lambda i: (i, 0)
    ctx_tiles = lay.n_ctx // TM
    if isinstance(a, tuple):
        ka = a[0].shape[1]
        a_specs = [pl.BlockSpec((TM, ka), lambda i: (jnp.minimum(i, ctx_tiles - 1), 0)),
                   pl.BlockSpec((TM, ka), lambda i: (jnp.maximum(i - ctx_tiles, 0), 0))]
    else:
        ka = a.shape[1]
        a_specs = [pl.BlockSpec((TM, ka), row)]
        a = (a,)
    return pl.pallas_call(
        functools.partial(_post_kernel, alpha=alpha, ctx_tiles=ctx_tiles),
        grid=(n // TM,),
        in_specs=a_specs + [
            pl.BlockSpec((TM, d), row),
            pl.BlockSpec((None, 6, d), lambda i: (lay.mod_row(i, TM), 0, 0)),
            _const_spec((ka, d)), _const_spec((2, d)), _const_spec((d, ROUTER_W)), _const_spec((1, ROUTER_W)),
        ],
        out_specs=[pl.BlockSpec((TM, d), row), pl.BlockSpec((TM, d), row), pl.BlockSpec((TM, ROUTER_W), row)],
        out_shape=[jax.ShapeDtypeStruct((n, d), F32), jax.ShapeDtypeStruct((n, d), F32),
                   jax.ShapeDtypeStruct((n, ROUTER_W), F32)],
        compiler_params=_params("parallel"),
        name="post_mixer",
    )(*a, h, mod, w_out.astype(BF16), ln, w_router, b_router)


HALO = 8


def _conv_kernel(h_ref, hp_ref, hn_ref, mod_ref, win_ref, cw_ref, cb_ref, w_ref, ln_ref, wr_ref, br_ref,
                 h_out, t_out, lg_out, *, alpha, tiles_per_seq, ctx_tiles):
    i = pl.program_id(0)
    d = h_ref.shape[1]
    h = h_ref[...]
    hx = jnp.concatenate([hp_ref[...], h, hn_ref[...]], axis=0)
    u = (hx * (1.0 + mod_ref[1:2, :]) + mod_ref[0:1, :]).astype(BF16)
    y = _dot(u, win_ref[...])
    gate_b = y[HALO:HALO + CHUNK, :d]
    zx = y[:, d:2 * d] * y[:, 2 * d:]
    z = zx[HALO:HALO + CHUNK]
    pos = jnp.where(i < ctx_tiles, 0, (i - ctx_tiles) % tiles_per_seq)
    last = jnp.where(i < ctx_tiles, 0, tiles_per_seq - 1)
    z_before = jnp.where(pos == 0, 0.0, zx[HALO - 1:HALO])
    z_after = jnp.where(pos == last, 0.0, zx[HALO + CHUNK:HALO + CHUNK + 1])
    r = lax.broadcasted_iota(jnp.int32, (CHUNK, 1), 0)
    z_prev = jnp.where(r == 0, z_before, pltpu.roll(z, 1, 0))
    z_next = jnp.where(r == CHUNK - 1, z_after, pltpu.roll(z, CHUNK - 1, 0))
    conv = cw_ref[0:1, :] * z_prev + cw_ref[1:2, :] * z + cw_ref[2:3, :] * z_next + cb_ref[...]
    yo = _dot((gate_b * conv).astype(BF16), w_ref[...])
    _post_tail(yo, h, mod_ref, ln_ref, wr_ref, br_ref, h_out, t_out, lg_out, alpha)


def _conv_layer(lay, h, mod, w_in, cw, cb, w_out, ln, w_router, b_router, alpha):
    n, d = h.shape
    n_tiles = n // CHUNK
    per = CHUNK // HALO
    row = lambda i: (i, 0)
    return pl.pallas_call(
        functools.partial(_conv_kernel, alpha=alpha, tiles_per_seq=lay.seq // CHUNK, ctx_tiles=lay.n_ctx // CHUNK),
        grid=(n_tiles,),
        in_specs=[
            pl.BlockSpec((CHUNK, d), row),
            pl.BlockSpec((HALO, d), lambda i: (jnp.maximum(i * per - 1, 0), 0)),
            pl.BlockSpec((HALO, d), lambda i: (jnp.minimum((i + 1) * per, n_tiles * per - 1), 0)),
            pl.BlockSpec((None, 6, d), lambda i: (lay.mod_row(i, CHUNK), 0, 0)),
            _const_spec((d, 3 * d)), _const_spec((3, d)), _const_spec((1, d)), _const_spec((d, d)),
            _const_spec((2, d)), _const_spec((d, ROUTER_W)), _const_spec((1, ROUTER_W)),
        ],
        out_specs=[pl.BlockSpec((CHUNK, d), row), pl.BlockSpec((CHUNK, d), row),
                   pl.BlockSpec((CHUNK, ROUTER_W), row)],
        out_shape=[jax.ShapeDtypeStruct((n, d), F32), jax.ShapeDtypeStruct((n, d), F32),
                   jax.ShapeDtypeStruct((n, ROUTER_W), F32)],
        compiler_params=_params("parallel"),
        name="conv_layer",
    )(h, h, h, mod, w_in.astype(BF16), cw.astype(F32), cb.reshape(1, d).astype(F32), w_out.astype(BF16),
      ln, w_router, b_router)


def _route_kernel(lg_ref, tri_ref, upper_ref, ext_ref, lp_ref, lpt_ref, tab_ref, cnt_ref, base_ref):
    @pl.when(pl.program_id(0) == 0)
    def _():
        base_ref[...] = jnp.zeros_like(base_ref)

    x = lg_ref[...]
    lane = lax.broadcasted_iota(jnp.int32, x.shape, 1)

    def softmax(mask):
        m = jnp.max(jnp.where(mask, x, -jnp.inf), axis=-1, keepdims=True)
        e = jnp.where(mask, jnp.exp(x - m), 0.0)
        return e / jnp.sum(e, axis=-1, keepdims=True)

    def top1(prob, mask):
        p = jnp.max(jnp.where(mask, prob, -1.0), axis=-1, keepdims=True)
        i = jnp.min(jnp.where(mask & (prob == p), lane, ROUTER_W), axis=-1, keepdims=True)
        return p, i

    g_mask = lane < MOE_GROUPS
    g_p, g_idx = top1(softmax(g_mask), g_mask)
    e_lo = MOE_GROUPS + MOE_EPG * g_idx
    e_mask = (lane >= e_lo) & (lane < e_lo + MOE_EPG)
    e_prob = softmax(e_mask)
    p1, i1 = top1(e_prob, e_mask)
    p2, i2 = top1(e_prob, e_mask & (lane != i1))
    denom = p1 + p2
    gate1 = g_p * p1 / denom
    gate2 = g_p * p2 / denom

    sel1 = lane == i1
    sel2 = lane == i2
    cnt = (sel1 | sel2).astype(F32)
    within = _dot(tri_ref[...], cnt.astype(BF16))
    cnt_tile = jnp.sum(cnt, axis=0, keepdims=True)
    ci = jnp.broadcast_to(cnt_tile, (8, ROUTER_W)).astype(jnp.int32)
    ci = (ci + (RUN_ALIGN - 1)) & ~(RUN_ALIGN - 1)
    run_tile = ci[0:1].astype(F32)
    start = (_dot((ci >> 4).astype(F32).astype(BF16), upper_ref[...]) * 16.0
             + _dot((ci & 15).astype(F32).astype(BF16), upper_ref[...]))[0:1]
    where_to = within + start
    lp1 = jnp.sum(jnp.where(sel1, where_to, 0.0), axis=-1, keepdims=True)
    lp2 = jnp.sum(jnp.where(sel2, where_to, 0.0), axis=-1, keepdims=True)

    before = base_ref[...]
    total = before + run_tile
    base_ref[...] = total
    cnt_ref[...] = jnp.broadcast_to(total, cnt_ref.shape)
    row = lax.broadcasted_iota(jnp.int32, tab_ref.shape, 0)
    tab_ref[...] = jnp.where(row == 0, run_tile, jnp.where(row == 1, before, jnp.where(row == 2, start, 0.0)))

    def pieces(g):
        hi = g.astype(BF16).astype(F32)
        mid = (g - hi).astype(BF16).astype(F32)
        return hi, mid, g - hi - mid

    a1, b1, c1 = pieces(gate1)
    a2, b2, c2 = pieces(gate2)
    cols = (a1, b1, c1, a2, b2, c2, (i1 - MOE_GROUPS).astype(F32))
    ext = jnp.zeros(x.shape, F32)
    for j, col in enumerate(cols):
        ext = jnp.where(lane == j, col, ext)
    ext_ref[...] = ext.astype(BF16)
    lp = jnp.where(lane == 0, lp1, jnp.where(lane == 1, lp2, 0.0))
    lp_ref[...] = lp
    lpt_ref[...] = lp.T[0:8, :]


def _route(logits):
    n = logits.shape[0]
    n_tiles = n // TOK_TILE
    r = lax.broadcasted_iota(jnp.int32, (TOK_TILE, TOK_TILE), 0)
    c = lax.broadcasted_iota(jnp.int32, (TOK_TILE, TOK_TILE), 1)
    tri = (c < r).astype(BF16)
    r = lax.broadcasted_iota(jnp.int32, (ROUTER_W, ROUTER_W), 0)
    c = lax.broadcasted_iota(jnp.int32, (ROUTER_W, ROUTER_W), 1)
    upper = (r < c).astype(BF16)
    row = lambda i: (i, 0)
    return pl.pallas_call(
        _route_kernel,
        grid=(n_tiles,),
        in_specs=[pl.BlockSpec((TOK_TILE, ROUTER_W), row), _const_spec((TOK_TILE, TOK_TILE)),
                  _const_spec((ROUTER_W, ROUTER_W))],
        out_specs=[pl.BlockSpec((TOK_TILE, ROUTER_W), row), pl.BlockSpec((TOK_TILE, ROUTER_W), row),
                   pl.BlockSpec((8, TOK_TILE), row), pl.BlockSpec((8, ROUTER_W), row),
                   pl.BlockSpec((8, ROUTER_W), lambda i: (0, 0))],
        out_shape=[jax.ShapeDtypeStruct((n, ROUTER_W), BF16), jax.ShapeDtypeStruct((n, ROUTER_W), F32),
                   jax.ShapeDtypeStruct((n_tiles * 8, TOK_TILE), F32),
                   jax.ShapeDtypeStruct((n_tiles * 8, ROUTER_W), F32),
                   jax.ShapeDtypeStruct((8, ROUTER_W), F32)],
        scratch_shapes=[pltpu.VMEM((1, ROUTER_W), F32)],
        compiler_params=_params("arbitrary"),
        name="moe_route",
    )(logits, tri, upper)


def _plan(tables, counts, n_blocks):
    n_tiles = tables.shape[0] // 8
    tables = tables.reshape(n_tiles, 8, ROUTER_W)[:, :3, MOE_GROUPS:MOE_GROUPS + MOE_EXPERTS].astype(jnp.int32)
    counts = counts[0, MOE_GROUPS:MOE_GROUPS + MOE_EXPERTS].astype(jnp.int32)
    padded = (counts + EXP_BLOCK - 1) // EXP_BLOCK * EXP_BLOCK
    pad_ends = jnp.cumsum(padded)
    pad_starts = pad_ends - padded
    run_len = tables[:, 0].reshape(-1)
    run_src = tables[:, 2].reshape(-1)
    run_dst = (tables[:, 1] + pad_starts[None, :]).reshape(-1)
    first_row = jnp.arange(n_blocks, dtype=jnp.int32) * EXP_BLOCK
    block_e = jnp.minimum(jnp.sum(first_row[:, None] >= pad_ends[None, :], axis=1),
                          MOE_EXPERTS - 1).astype(jnp.int32)
    n_valid = (pad_ends[-1] // EXP_BLOCK).astype(jnp.int32).reshape(1)
    clear_row = jnp.concatenate([pad_starts + counts, pad_ends[-1:]])
    clear_len = jnp.concatenate([padded - counts, n_blocks - n_valid])
    return run_len, run_src, run_dst, clear_row, clear_len, block_e, n_valid


RUN_BITS = tuple(b for b in (1 << k for k in range(TOK_TILE.bit_length() - 1, -1, -1)) if b >= RUN_ALIGN)
RARE_BITS = 3
SORT_ROWS = MOE_TOPK * TOK_TILE + MOE_EXPERTS * RUN_ALIGN


def _for_each_piece(length, src, dst, fn):
    def pieces(bits):
        for bit in bits:
            above = length & ~(2 * bit - 1)

            @pl.when((length & bit) != 0)
            def _(bit=bit, above=above):
                fn(pl.multiple_of(src + above, RUN_ALIGN), pl.multiple_of(dst + above, RUN_ALIGN), bit)

    @pl.when(length >= RUN_BITS[RARE_BITS - 1])
    def _():
        pieces(RUN_BITS[:RARE_BITS])
    pieces(RUN_BITS[RARE_BITS:])


def _dispatch_kernel(len_ref, src_ref, dst_ref, zrow_ref, zlen_ref, t_ref, ext_ref, lpt_ref, xs_ref,
                     buf, zeros, sem, zsem, *, n_tiles):
    i = pl.program_id(0)
    half = t_ref.shape[1] // 2

    @pl.when(i == 0)
    def _():
        zeros[...] = jnp.zeros_like(zeros)

        def clear(e, c):
            _for_each_piece(zlen_ref[e], 0, zrow_ref[e], lambda s, d, rows: pltpu.make_async_copy(
                zeros.at[pl.ds(0, rows)], xs_ref.at[pl.ds(d, rows)], zsem).start())
            return c
        lax.fori_loop(0, MOE_EXPERTS, clear, 0)

        def drain(e, c):
            _for_each_piece(zlen_ref[e], 0, zrow_ref[e], lambda s, d, rows: pltpu.make_async_copy(
                zeros.at[pl.ds(0, rows)], xs_ref.at[pl.ds(d, rows)], zsem).wait())
            return c
        lax.fori_loop(0, MOE_EXPERTS, drain, 0)

        def tail_copy(bk):
            row = pl.multiple_of(zrow_ref[MOE_EXPERTS] + bk * EXP_BLOCK, EXP_BLOCK)
            return pltpu.make_async_copy(zeros.at[pl.ds(0, EXP_BLOCK)], xs_ref.at[pl.ds(row, EXP_BLOCK)], zsem)

        def clear_tail(bk, c):
            tail_copy(bk).start()
            return c
        lax.fori_loop(0, zlen_ref[MOE_EXPERTS], clear_tail, 0)

        def drain_tail(bk, c):
            tail_copy(bk).wait()
            return c
        lax.fori_loop(0, zlen_ref[MOE_EXPERTS], drain_tail, 0)

    slot = i % 2

    def scatter(tile, sl, start):
        def body(e, c):
            k = tile * MOE_EXPERTS + e

            def piece(s, d, rows):
                cp = pltpu.make_async_copy(buf.at[sl, pl.ds(s, rows)], xs_ref.at[pl.ds(d, rows)], sem.at[sl])
                cp.start() if start else cp.wait()
            _for_each_piece(len_ref[k], src_ref[k], dst_ref[k], piece)
            return c
        lax.fori_loop(0, MOE_EXPERTS, body, 0)

    @pl.when(i >= 2)
    def _():
        scatter(i - 2, slot, False)

    lp = lpt_ref[...].astype(jnp.int32)
    j = lax.broadcasted_iota(jnp.int32, (SORT_ROWS, TOK_TILE), 0)
    perm = ((j == lp[0:1, :]) | (j == lp[1:2, :])).astype(F32).astype(BF16)
    rhs = jnp.concatenate([t_ref[...].astype(BF16), ext_ref[...]], axis=1)
    srt = _dot(perm, rhs)
    bits = lax.bitcast_convert_type(srt, jnp.uint32)
    buf[slot, :, :half] = (bits[:, :half] >> 16) | (bits[:, half:2 * half] & jnp.uint32(0xFFFF0000))
    buf[slot, :, half:] = bits[:, 2 * half:]
    scatter(i, slot, True)

    @pl.when(i == n_tiles - 1)
    def _():
        if n_tiles > 1:
            scatter(i - 1, 1 - slot, False)
        scatter(i, slot, False)


def _expert_kernel(be_ref, nv_ref, x_ref, w1_ref, w3_ref, w2_ref, o_ref):
    i = pl.program_id(0)
    valid = i < nv_ref[0]
    half = o_ref.shape[1] // 2

    @pl.when(valid)
    def _():
        packed = x_ref[:, :half]
        lo = lax.bitcast_convert_type(packed << 16, F32)
        hi = lax.bitcast_convert_type(packed & jnp.uint32(0xFFFF0000), F32)
        x = jnp.concatenate([lo, hi], axis=1).astype(BF16)
        ext = lax.bitcast_convert_type(x_ref[:, half:], F32)
        g1 = ext[:, 0:1] + ext[:, 1:2] + ext[:, 2:3]
        g2 = ext[:, 3:4] + ext[:, 4:5] + ext[:, 5:6]
        gate = jnp.where(ext[:, 6:7] == be_ref[i].astype(F32), g1, g2)
        hdn = _silu(_dot(x, w1_ref[...])) * _dot(x, w3_ref[...])
        o_ref[...] = _dot(hdn.astype(BF16), w2_ref[...]) * gate

    @pl.when(jnp.logical_not(valid))
    def _():
        o_ref[...] = jnp.zeros_like(o_ref)


def _combine_kernel(len_ref, src_ref, dst_ref, ys_ref, lp_ref, h_ref, mod_ref, ln_ref, o_ref, buf, sem,
                    *, alpha, first_tile, n_steps):
    step = pl.program_id(0)
    tile = step + first_tile
    slot = step % 2

    def gather(tl, sl, start):
        def body(e, c):
            k = tl * MOE_EXPERTS + e

            def piece(s, d, rows):
                cp = pltpu.make_async_copy(ys_ref.at[pl.ds(d, rows)], buf.at[sl, pl.ds(s, rows)], sem.at[sl])
                cp.start() if start else cp.wait()
            _for_each_piece(len_ref[k], src_ref[k], dst_ref[k], piece)
            return c
        lax.fori_loop(0, MOE_EXPERTS, body, 0)

    @pl.when(step == 0)
    def _():
        buf[...] = jnp.zeros_like(buf)
        gather(tile, slot, True)

    @pl.when(step + 1 < n_steps)
    def _():
        gather(tile + 1, 1 - slot, True)
    gather(tile, slot, False)

    lp = lp_ref[...].astype(jnp.int32)
    j = lax.broadcasted_iota(jnp.int32, (TOK_TILE, SORT_ROWS), 1)
    pick = ((j == lp[:, 0:1]) | (j == lp[:, 1:2])).astype(F32).astype(BF16)
    y = buf[slot]
    hi = y.astype(BF16)
    r1 = y - hi.astype(F32)
    mid = r1.astype(BF16)
    lo = (r1 - mid.astype(F32)).astype(BF16)
    f = _dot(pick, hi) + _dot(pick, mid) + _dot(pick, lo)
    o_ref[...] = _layer_norm(alpha * h_ref[...] + mod_ref[5:6, :] * f, ln_ref[0:1, :], ln_ref[1:2, :])


def _moe(lay, t, logits, h, mod, ln, w1, w3, w2, alpha, first_tile=0):
    n, d = t.shape
    hid = w1.shape[2]
    n_tiles = n // TOK_TILE
    max_rows = n * MOE_TOPK + n_tiles * MOE_EXPERTS * (RUN_ALIGN - 1)
    n_blocks = -(-max_rows // EXP_BLOCK) + MOE_EXPERTS
    n_rows = n_blocks * EXP_BLOCK
    xs_w = d // 2 + ROUTER_W
    ext, lp, lpt, tables, counts = _route(logits)
    run_len, run_src, run_dst, zrow, zlen, block_e, n_valid = _plan(tables, counts, n_blocks)

    xs = pl.pallas_call(
        functools.partial(_dispatch_kernel, n_tiles=n_tiles),
        grid_spec=pltpu.PrefetchScalarGridSpec(
            num_scalar_prefetch=5,
            grid=(n_tiles,),
            in_specs=[pl.BlockSpec((TOK_TILE, d), lambda i, *_: (i, 0)),
                      pl.BlockSpec((TOK_TILE, ROUTER_W), lambda i, *_: (i, 0)),
                      pl.BlockSpec((8, TOK_TILE), lambda i, *_: (i, 0))],
            out_specs=pl.BlockSpec(memory_space=pl.ANY),
            scratch_shapes=[pltpu.VMEM((2, SORT_ROWS, xs_w), jnp.uint32),
                            pltpu.VMEM((max(TOK_TILE, EXP_BLOCK), xs_w), jnp.uint32),
                            pltpu.SemaphoreType.DMA((2,)), pltpu.SemaphoreType.DMA],
        ),
        out_shape=jax.ShapeDtypeStruct((n_rows, xs_w), jnp.uint32),
        compiler_params=_params("arbitrary"),
        name="moe_dispatch",
    )(run_len, run_src, run_dst, zrow, zlen, t, ext, lpt)

    blk = lambda i, be, nv: (jnp.minimum(i, nv[0] - 1), 0)
    ys = pl.pallas_call(
        _expert_kernel,
        grid_spec=pltpu.PrefetchScalarGridSpec(
            num_scalar_prefetch=2,
            grid=(n_blocks,),
            in_specs=[
                pl.BlockSpec((EXP_BLOCK, xs_w), blk),
                pl.BlockSpec((None, d, hid), lambda i, be, nv: (be[i], 0, 0)),
                pl.BlockSpec((None, d, hid), lambda i, be, nv: (be[i], 0, 0)),
                pl.BlockSpec((None, hid, d), lambda i, be, nv: (be[i], 0, 0)),
            ],
            out_specs=pl.BlockSpec((EXP_BLOCK, d), lambda i, be, nv: (i, 0)),
        ),
        out_shape=jax.ShapeDtypeStruct((n_rows, d), F32),
        compiler_params=_params("arbitrary"),
        name="moe_experts",
    )(block_e, n_valid, xs, w1.astype(BF16), w3.astype(BF16), w2.astype(BF16))

    n_steps = n_tiles - first_tile
    return pl.pallas_call(
        functools.partial(_combine_kernel, alpha=alpha, first_tile=first_tile, n_steps=n_steps),
        grid_spec=pltpu.PrefetchScalarGridSpec(
            num_scalar_prefetch=3,
            grid=(n_steps,),
            in_specs=[pl.BlockSpec(memory_space=pl.ANY),
                      pl.BlockSpec((TOK_TILE, ROUTER_W), lambda i, *_: (i + first_tile, 0)),
                      pl.BlockSpec((TOK_TILE, d), lambda i, *_: (i + first_tile, 0)),
                      pl.BlockSpec((None, 6, d), lambda i, *_: (lay.mod_row(i + first_tile, TOK_TILE), 0, 0)),
                      pl.BlockSpec((2, d), lambda i, *_: (0, 0))],
            out_specs=pl.BlockSpec((TOK_TILE, d), lambda i, *_: (i, 0)),
            scratch_shapes=[pltpu.VMEM((2, SORT_ROWS, d), F32), pltpu.SemaphoreType.DMA((2,))],
        ),
        out_shape=jax.ShapeDtypeStruct((n_steps * TOK_TILE, d), F32),
        compiler_params=_params("arbitrary"),
        name="moe_combine",
    )(run_len, run_src, run_dst, ys, lp, h, mod, ln)


def kernel(x, c, ctx, c_ctx, ada_w, ada_b, ln_g, ln_b, ret_w_in, ret_decay, ret_gn_g, ret_w_out, mla_w_down, mla_q_norm, mla_kv_norm, mla_w_uq, mla_w_ukv, mla_w_out, conv_w_in, conv_w, conv_b, conv_w_out, moe_w_group, moe_b_group, moe_w_expert, moe_b_expert, moe_w1, moe_w3, moe_w2):
    b, s, d = x.shape
    depth = ada_w.shape[0]
    lay = _Layout(b, s, ctx.shape[1])
    alpha = (2.0 * depth) ** 0.25

    mod_rows = -(-(b + 1) // 8) * 8
    cc = jnp.concatenate([c, c_ctx[None, :], jnp.zeros((mod_rows - b - 1, d), F32)], axis=0)
    mod_all = _ada(cc, ada_w, ada_b).reshape(depth, mod_rows, 6, d)

    h = jnp.concatenate([ctx.reshape(lay.n_ctx, d), x.reshape(b * s, d)], axis=0)
    for i in range(depth):
        kind, j = i % N_MIXERS, i // N_MIXERS
        mod = mod_all[i]
        ln1 = jnp.stack([ln_g[i, 0], ln_b[i, 0]])
        ln2 = jnp.stack([ln_g[i, 1], ln_b[i, 1]])
        pad = jnp.zeros((d, ROUTER_W - MOE_GROUPS - MOE_EXPERTS), F32)
        w_router = jnp.concatenate([moe_w_group[i], moe_w_expert[i], pad], axis=1).astype(BF16)
        b_router = jnp.concatenate([moe_b_group[i], moe_b_expert[i], pad[0]]).reshape(1, ROUTER_W)
        if kind == 0:
            a = _retention(lay, h, mod, ret_w_in[j], ret_decay[j], ret_gn_g[j], d)
            h1, t, logits = _post(lay, a, h, mod, ret_w_out[j], ln1, w_router, b_router, alpha)
        elif kind == 1:
            a = _mla(lay, h, mod, mla_w_down[j], mla_q_norm[j], mla_kv_norm[j], mla_w_uq[j], mla_w_ukv[j], d)
            h1, t, logits = _post(lay, a, h, mod, mla_w_out[j], ln1, w_router, b_router, alpha)
        else:
            h1, t, logits = _conv_layer(lay, h, mod, conv_w_in[j], conv_w[j], conv_b[j], conv_w_out[j],
                                        ln1, w_router, b_router, alpha)
        first_tile = lay.n_ctx // TOK_TILE if i == depth - 1 else 0
        h = _moe(lay, t, logits, h1, mod, ln2, moe_w1[i], moe_w3[i], moe_w2[i], alpha, first_tile)
    return h.reshape(b, s, d)
```

```python
import functools

import jax
import jax.numpy as jnp
from jax import lax
from jax.experimental import pallas as pl
from jax.experimental.pallas import tpu as pltpu

F32 = jnp.float32
BF16 = jnp.bfloat16

GRID_W = 64
LN_EPS = 1e-5
RMS_EPS = 1e-6
ROPE_BASE = 10000.0
N_MIXERS = 3
RET_HEADS = 4
MLA_HEADS = 8
MLA_NOPE = 128
MLA_ROPE = 64
MLA_V = 128
MLA_QK = 256
MOE_GROUPS = 4
MOE_EPG = 8
MOE_EXPERTS = MOE_GROUPS * MOE_EPG
MOE_TOPK = 2
ROUTER_W = 128

CHUNK = 256
TM = 512
TQ = 512
TOK_TILE = 512
EXP_BLOCK = 512
RUN_ALIGN = 8
VMEM_LIMIT = 56 * 1024 * 1024


def _params(*sem):
    return pltpu.CompilerParams(dimension_semantics=sem, vmem_limit_bytes=VMEM_LIMIT)


def _const_spec(shape):
    nd = len(shape)
    return pl.BlockSpec(shape, lambda *_: (0,) * nd, pipeline_mode=pl.Buffered(1))


def _dot(a, b):
    return jnp.dot(a, b, preferred_element_type=F32)


def _dot_nt(a, b):
    return lax.dot_general(a, b, (((1,), (1,)), ((), ())), preferred_element_type=F32)


def _dot_tn(a, b):
    return lax.dot_general(a, b, (((0,), (0,)), ((), ())), preferred_element_type=F32)


def _silu(x):
    return x * jax.nn.sigmoid(x)


def _layer_norm(x, g, b):
    mu = jnp.mean(x, axis=-1, keepdims=True)
    xc = x - mu
    var = jnp.mean(xc * xc, axis=-1, keepdims=True)
    return xc * lax.rsqrt(var + LN_EPS) * g + b


def _rms_norm(x, g):
    return x * lax.rsqrt(jnp.mean(x * x, axis=-1, keepdims=True) + RMS_EPS) * g


def _ada_kernel(c_ref, w_ref, b_ref, o_ref):
    a = _silu(c_ref[...]).astype(BF16)
    o_ref[...] = _dot(a, w_ref[...].astype(BF16)) + b_ref[...]


def _ada(cc, ada_w, ada_b):
    depth, d, n6 = ada_w.shape
    rows = cc.shape[0]
    tn = 1536
    return pl.pallas_call(
        _ada_kernel,
        grid=(depth, n6 // tn),
        in_specs=[
            pl.BlockSpec((rows, d), lambda l, j: (0, 0)),
            pl.BlockSpec((None, d, tn), lambda l, j: (l, 0, j)),
            pl.BlockSpec((None, 1, tn), lambda l, j: (l, 0, j)),
        ],
        out_specs=pl.BlockSpec((None, rows, tn), lambda l, j: (l, 0, j)),
        out_shape=jax.ShapeDtypeStruct((depth, rows, n6), F32),
        compiler_params=_params("parallel", "parallel"),
        name="ada_mod",
    )(cc, ada_w, ada_b.reshape(depth, 1, n6))


class _Layout:
    def __init__(self, batch, seq, ctx_len):
        assert ctx_len == CHUNK and seq % TM == 0 and seq % CHUNK == 0
        self.batch, self.seq, self.ctx = batch, seq, ctx_len
        self.n_ctx = batch * ctx_len
        self.n = self.n_ctx + batch * seq
        assert self.n_ctx % TM == 0 and self.n_ctx % seq == 0 and self.n_ctx % TOK_TILE == 0
        assert self.n_ctx % TQ == 0 and seq % TQ == 0 and seq % TOK_TILE == 0

    def mod_row(self, tile, rows_per_tile):
        ctx_tiles = self.n_ctx // rows_per_tile
        per_batch = self.seq // rows_per_tile
        return jnp.where(tile < ctx_tiles, self.batch, (tile - ctx_tiles) // per_batch)

    def pos_block(self, tile, rows_per_tile):
        ctx_tiles = self.n_ctx // rows_per_tile
        per_batch = self.seq // rows_per_tile
        return jnp.where(tile < ctx_tiles, 0, 1 + (tile - ctx_tiles) % per_batch)


def _row_specs(x, rows, ctx_tiles):
    if not isinstance(x, tuple):
        return (x,), [pl.BlockSpec((rows, x.shape[1]), lambda i: (i, 0))]
    width = x[0].shape[1]
    return x, [pl.BlockSpec((rows, width), lambda i: (jnp.minimum(i, ctx_tiles - 1), 0)),
               pl.BlockSpec((rows, width), lambda i: (jnp.maximum(i - ctx_tiles, 0), 0))]


def _row_tile(refs, ctx_tiles):
    if len(refs) == 1:
        return refs[0][...]
    return jnp.where(pl.program_id(0) < ctx_tiles, refs[0][...], refs[1][...])


def _rope_angles(seq, dim):
    n_rows = seq // GRID_W
    rows = jnp.repeat(jnp.arange(n_rows, dtype=F32), GRID_W)
    cols = jnp.tile(jnp.arange(GRID_W, dtype=F32), n_rows)
    quarter = dim // 4
    inv_freq = ROPE_BASE ** (-jnp.arange(quarter, dtype=F32) / quarter)
    return jnp.concatenate([rows[:, None] * inv_freq, cols[:, None] * inv_freq], axis=-1)


def _with_identity_block(table, fill, rows):
    ident = jnp.full((rows, table.shape[1]), fill, F32)
    return jnp.concatenate([ident, table], axis=0)


def _proj_ret_kernel(*refs, dk, ctx_tiles):
    h_refs = refs[:-14]
    (mod_ref, cos_ref, sin_ref, kdec_ref, wq_ref, wk_ref, wv_ref, wg_ref,
     q_ref, k_ref, kf_ref, kb_ref, v_ref, g_ref) = refs[-14:]
    u = (_row_tile(h_refs, ctx_tiles) * (1.0 + mod_ref[1:2, :]) + mod_ref[0:1, :]).astype(BF16)
    cos = cos_ref[...]
    sin = sin_ref[...]
    half = dk // 2
    yq = _dot(u, wq_ref[...])
    for hd in range(RET_HEADS):
        x1 = yq[:, hd * dk:hd * dk + half]
        x2 = yq[:, hd * dk + half:(hd + 1) * dk]
        q_ref[:, hd * dk:hd * dk + half] = (x1 * cos - x2 * sin).astype(BF16)
        q_ref[:, hd * dk + half:(hd + 1) * dk] = (x1 * sin + x2 * cos).astype(BF16)
    yk = _dot(u, wk_ref[...])
    k_scale = dk ** -0.5
    for hd in range(RET_HEADS):
        x1 = yk[:, hd * dk:hd * dk + half]
        x2 = yk[:, hd * dk + half:(hd + 1) * dk]
        df = kdec_ref[:, hd:hd + 1]
        db = kdec_ref[:, RET_HEADS + hd:RET_HEADS + hd + 1]
        for part, o in ((0, (x1 * cos - x2 * sin) * k_scale), (1, (x1 * sin + x2 * cos) * k_scale)):
            sl = slice(hd * dk + part * half, hd * dk + (part + 1) * half)
            k_ref[:, sl] = o.astype(BF16)
            kf_ref[:, sl] = (o * df).astype(BF16)
            kb_ref[:, sl] = (o * db).astype(BF16)
    v_ref[...] = _dot(u, wv_ref[...]).astype(BF16)
    g_ref[...] = _silu(_dot(u, wg_ref[...]))


def _ret_scan_kernel(cdec_ref, q_ref, k_ref, kd_ref, v_ref, intra_ref, qdec_ref, *rest, dk, dv, direction):
    if direction == 0:
        o_ref, state_ref = rest
    else:
        of_ref, g_ref, gn_ref, o_ref, state_ref = rest

    @pl.when(pl.program_id(1) == 0)
    def _():
        state_ref[...] = jnp.zeros_like(state_ref)

    for hd in range(RET_HEADS):
        q = q_ref[:, hd * dk:(hd + 1) * dk]
        k = k_ref[:, hd * dk:(hd + 1) * dk]
        kd = kd_ref[:, hd * dk:(hd + 1) * dk]
        v = v_ref[:, hd * dv:(hd + 1) * dv]
        p = (_dot_nt(q, k) * intra_ref[hd]).astype(BF16)
        st = state_ref[hd]
        col = direction * RET_HEADS + hd
        o = _dot(p, v) + qdec_ref[:, col:col + 1] * _dot(q, st.astype(BF16))
        state_ref[hd] = st * cdec_ref[col] + _dot_tn(kd, v)
        if direction == 0:
            o_ref[:, hd * dv:(hd + 1) * dv] = o
        else:
            o = o + of_ref[:, hd * dv:(hd + 1) * dv]
            mu = jnp.mean(o, axis=-1, keepdims=True)
            oc = o - mu
            var = jnp.mean(oc * oc, axis=-1, keepdims=True)
            on = oc * lax.rsqrt(var + LN_EPS) * gn_ref[:, hd * dv:(hd + 1) * dv]
            o_ref[:, hd * dv:(hd + 1) * dv] = (g_ref[:, hd * dv:(hd + 1) * dv] * on).astype(BF16)


def _retention(lay, h, mod, w_in, decay_logit, gn_g, d_model):
    n, b, nc = lay.n, lay.batch, lay.seq // CHUNK
    dk = d_model // RET_HEADS
    dv = 2 * dk
    qk, vw = RET_HEADS * dk, RET_HEADS * dv
    w_in = w_in.astype(BF16)
    wq, wk, wv, wg = w_in[:, :qk], w_in[:, qk:2 * qk], w_in[:, 2 * qk:2 * qk + vw], w_in[:, 2 * qk + vw:]

    ang = _rope_angles(lay.seq, dk)
    cos_t = _with_identity_block(jnp.cos(ang), 1.0, TM)
    sin_t = _with_identity_block(jnp.sin(ang), 0.0, TM)

    lg = jax.nn.log_sigmoid(decay_logit.astype(F32))
    idx = jnp.arange(CHUNK, dtype=F32)
    k_pow = jnp.stack([CHUNK - 1.0 - idx, idx])
    q_pow = jnp.stack([idx + 1.0, CHUNK - idx])
    kdec = jnp.exp(k_pow[:, :, None] * lg[:, None, :])
    qdec = jnp.exp(q_pow[:, :, None] * lg[:, None, :])
    kdec = jnp.moveaxis(kdec, 0, 1).reshape(CHUNK, 2 * RET_HEADS)
    qdec = jnp.moveaxis(qdec, 0, 1).reshape(CHUNK, 2 * RET_HEADS)
    cdec = jnp.exp(CHUNK * lg).reshape(2 * RET_HEADS)
    rel = idx[:, None] - idx[None, :]
    rel = jnp.stack([rel, -rel])
    intra = jnp.where(rel[:, None] >= 0, jnp.exp(jnp.maximum(rel[:, None], 0.0) * lg[:, :, None, None]), 0.0)
    kdec_tm = jnp.tile(kdec, (TM // CHUNK, 1))

    n_tiles = n // TM
    row = lambda i: (i, 0)
    ctx_tiles = lay.n_ctx // TM
    h, h_specs = _row_specs(h, TM, ctx_tiles)
    q, k, kf, kb, v, g = pl.pallas_call(
        functools.partial(_proj_ret_kernel, dk=dk, ctx_tiles=ctx_tiles),
        grid=(n_tiles,),
        in_specs=h_specs + [
            pl.BlockSpec((None, 6, d_model), lambda i: (lay.mod_row(i, TM), 0, 0)),
            pl.BlockSpec((TM, dk // 2), lambda i: (lay.pos_block(i, TM), 0)),
            pl.BlockSpec((TM, dk // 2), lambda i: (lay.pos_block(i, TM), 0)),
            _const_spec((TM, 2 * RET_HEADS)),
            _const_spec((d_model, qk)), _const_spec((d_model, qk)),
            _const_spec((d_model, vw)), _const_spec((d_model, vw)),
        ],
        out_specs=[pl.BlockSpec((TM, qk), row)] * 4 + [pl.BlockSpec((TM, vw), row)] * 2,
        out_shape=[jax.ShapeDtypeStruct((n, qk), BF16)] * 4
        + [jax.ShapeDtypeStruct((n, vw), BF16), jax.ShapeDtypeStruct((n, vw), F32)],
        compiler_params=_params("parallel"),
        name="ret_proj",
    )(*h, mod, cos_t, sin_t, kdec_tm, wq, wk, wv, wg)

    def chunk_fwd(bi, c):
        return (jnp.where(c == 0, bi, b + bi * nc + c - 1), 0)

    def chunk_bwd(bi, c):
        return (jnp.where(c == 0, bi, b + bi * nc + nc - c), 0)

    def scan(direction, chunk_map, kd, extra_in, extra_specs, out_dtype, name):
        return pl.pallas_call(
            functools.partial(_ret_scan_kernel, dk=dk, dv=dv, direction=direction),
            grid=(b, nc + 1),
            in_specs=[
                pl.BlockSpec(memory_space=pltpu.SMEM),
                pl.BlockSpec((CHUNK, qk), chunk_map),
                pl.BlockSpec((CHUNK, qk), chunk_map),
                pl.BlockSpec((CHUNK, qk), chunk_map),
                pl.BlockSpec((CHUNK, vw), chunk_map),
                _const_spec((RET_HEADS, CHUNK, CHUNK)),
                _const_spec((CHUNK, 2 * RET_HEADS)),
            ] + extra_specs,
            out_specs=pl.BlockSpec((CHUNK, vw), chunk_map),
            out_shape=jax.ShapeDtypeStruct((n, vw), out_dtype),
            scratch_shapes=[pltpu.VMEM((RET_HEADS, dk, dv), F32)],
            compiler_params=_params("parallel", "arbitrary"),
            name=name,
        )(cdec, q, k, kd, v, intra[direction], qdec, *extra_in)

    o_f = scan(0, chunk_fwd, kf, [], [], F32, "ret_scan_fwd")
    return scan(1, chunk_bwd, kb, [o_f, g, gn_g.reshape(1, vw).astype(F32)],
                [pl.BlockSpec((CHUNK, vw), chunk_bwd), pl.BlockSpec((CHUNK, vw), chunk_bwd),
                 _const_spec((1, vw))], BF16, "ret_scan_bwd")


def _proj_mla_kernel(h_ref, mod_ref, ct_ref, st_ref, wdq_ref, wdkv_ref, wkr_ref, qn_ref, kvn_ref,
                     wuq_ref, wukn_ref, wuv_ref, q_ref, k_ref, vt_ref):
    u = (h_ref[...] * (1.0 + mod_ref[1:2, :]) + mod_ref[0:1, :]).astype(BF16)
    ct = ct_ref[...]
    st = st_ref[...]
    half = MLA_QK // 2

    def rope(x):
        return x * ct + pltpu.roll(x, half // 2, 1) * st

    cq = _rms_norm(_dot(u, wdq_ref[...]), qn_ref[...]).astype(BF16)
    yq = _dot(cq, wuq_ref[...])
    ckv = _rms_norm(_dot(u, wdkv_ref[...]), kvn_ref[...]).astype(BF16)
    kn = _dot(ckv, wukn_ref[...])
    kr = rope(_dot(u, wkr_ref[...])).astype(BF16)
    for hd in range(MLA_HEADS):
        q_ref[:, hd * MLA_QK:hd * MLA_QK + half] = yq[:, hd * MLA_QK:hd * MLA_QK + half].astype(BF16)
        q_ref[:, hd * MLA_QK + half:(hd + 1) * MLA_QK] = rope(
            yq[:, hd * MLA_QK + half:(hd + 1) * MLA_QK]).astype(BF16)
        k_ref[:, hd * MLA_QK:hd * MLA_QK + half] = kn[:, hd * half:(hd + 1) * half].astype(BF16)
        k_ref[:, hd * MLA_QK + half:(hd + 1) * MLA_QK] = kr
    vt_ref[...] = _dot(ckv, wuv_ref[...]).T.astype(BF16)


KV_CHUNK = 512
LOG2_E = 1.4426950408889634


def _kv_chunks(kv_refs, n_kv):
    chunks = []
    row = 0
    for j in range(n_kv):
        keys = kv_refs[2 * j].shape[0]
        step = min(keys, KV_CHUNK)
        for off in range(0, keys, step):
            chunks.append((j, off, row, step))
            row += step
    return chunks


def _attn_kernel(q_ref, *rest, scale, n_kv, pipelined):
    kv_refs = rest[:2 * n_kv]
    chunks = _kv_chunks(kv_refs, n_kv)

    def scores(c, s_ref, m):
        j, off, row, w = c
        s = _dot_nt(kv_refs[2 * j][off:off + w, :], q_ref[...])
        s_ref[row:row + w, :] = s
        cm = jnp.max(s, axis=0, keepdims=True)
        return cm if m is None else jnp.maximum(m, cm)

    def values(c, s_ref, m, acc, den):
        j, off, row, w = c
        p = jnp.exp2((s_ref[row:row + w, :] - m) * (scale * LOG2_E))
        cs = jnp.sum(p, axis=0, keepdims=True)
        pv = _dot(kv_refs[2 * j + 1][:, off:off + w], p.astype(BF16))
        return (pv if acc is None else acc + pv), (cs if den is None else den + cs)

    if not pipelined:
        o_ref, s_ref = rest[-2:]
        m = acc = den = None
        for c in chunks:
            m = scores(c, s_ref, m)
        for c in chunks:
            acc, den = values(c, s_ref, m, acc, den)
        o_ref[...] = (acc / den).T.astype(BF16)
        return

    o_ref, s0_ref, s1_ref, m0_ref, m1_ref = rest[-5:]
    t = pl.program_id(2)
    last = pl.num_programs(2) - 1
    bufs = ((s0_ref, m0_ref), (s1_ref, m1_ref))

    def step(new, old):
        m_prev = old[1][...] if old else None
        m = acc = den = None
        for c in chunks:
            if new:
                m = scores(c, new[0], m)
            if old:
                acc, den = values(c, old[0], m_prev, acc, den)
        if new:
            new[1][...] = m
        if old:
            o_ref[...] = (acc / den).T.astype(BF16)

    @pl.when(t == 0)
    def _():
        step(bufs[0], None)

    for parity in (0, 1):
        @pl.when((t > 0) & (t < last) & (t % 2 == parity))
        def _(parity=parity):
            step(bufs[parity], bufs[1 - parity])

        @pl.when((t == last) & (t % 2 == parity))
        def _(parity=parity):
            step(None, bufs[1 - parity])


def _pad_rope_cols(w):
    z = jnp.zeros((w.shape[0], MLA_ROPE // 2), w.dtype)
    return jnp.concatenate([w[:, :MLA_ROPE // 2], z, w[:, MLA_ROPE // 2:], z], axis=1)


def _mla(lay, h, mod, w_down, q_norm, kv_norm, w_uq, w_ukv, d_model):
    n, b, s = lay.n, lay.batch, lay.seq
    q_lora, kv_lora = q_norm.shape[0], kv_norm.shape[0]
    w_dq = w_down[:, :q_lora].astype(BF16)
    w_dkv = w_down[:, q_lora:q_lora + kv_lora].astype(BF16)
    w_kr = _pad_rope_cols(w_down[:, q_lora + kv_lora:]).astype(BF16)
    w_uq = w_uq.reshape(q_lora, MLA_HEADS, MLA_NOPE + MLA_ROPE)
    w_uq = jnp.concatenate(
        [w_uq[:, :, :MLA_NOPE], jax.vmap(_pad_rope_cols, 1, 1)(w_uq[:, :, MLA_NOPE:])], axis=2)
    w_uq = w_uq.reshape(q_lora, MLA_HEADS * MLA_QK).astype(BF16)
    w_ukv = w_ukv.reshape(kv_lora, MLA_HEADS, MLA_NOPE + MLA_V)
    w_ukn = w_ukv[:, :, :MLA_NOPE].reshape(kv_lora, MLA_HEADS * MLA_NOPE).astype(BF16)
    w_uv = w_ukv[:, :, MLA_NOPE:].reshape(kv_lora, MLA_HEADS * MLA_V).astype(BF16)

    ang = _rope_angles(s, MLA_ROPE)
    z = jnp.zeros_like(ang)
    ct = _with_identity_block(jnp.concatenate([jnp.cos(ang), z, jnp.cos(ang), z], axis=1), 1.0, TM)
    st = _with_identity_block(jnp.concatenate([-jnp.sin(ang), z, jnp.sin(ang), z], axis=1), 0.0, TM)

    row = lambda i: (i, 0)
    qw, vw = MLA_HEADS * MLA_QK, MLA_HEADS * MLA_V
    q, k, vt = pl.pallas_call(
        _proj_mla_kernel,
        grid=(n // TM,),
        in_specs=[
            pl.BlockSpec((TM, d_model), row),
            pl.BlockSpec((None, 6, d_model), lambda i: (lay.mod_row(i, TM), 0, 0)),
            pl.BlockSpec((TM, MLA_QK // 2), lambda i: (lay.pos_block(i, TM), 0)),
            pl.BlockSpec((TM, MLA_QK // 2), lambda i: (lay.pos_block(i, TM), 0)),
            _const_spec((d_model, q_lora)), _const_spec((d_model, kv_lora)), _const_spec((d_model, MLA_QK // 2)),
            _const_spec((1, q_lora)), _const_spec((1, kv_lora)),
            _const_spec((q_lora, qw)), _const_spec((kv_lora, vw)), _const_spec((kv_lora, vw)),
        ],
        out_specs=[pl.BlockSpec((TM, qw), row), pl.BlockSpec((TM, qw), row),
                   pl.BlockSpec((vw, TM), lambda i: (0, i))],
        out_shape=[jax.ShapeDtypeStruct((n, qw), BF16), jax.ShapeDtypeStruct((n, qw), BF16),
                   jax.ShapeDtypeStruct((vw, n), BF16)],
        compiler_params=_params("parallel"),
        name="mla_proj",
    )(h, mod, ct, st, w_dq, w_dkv, w_kr, q_norm.reshape(1, -1).astype(F32), kv_norm.reshape(1, -1).astype(F32),
      w_uq, w_ukn, w_uv)

    scale = (MLA_NOPE + MLA_ROPE) ** -0.5
    ctx_q = lambda bi, hd: (bi, hd)
    att_ctx = pl.pallas_call(
        functools.partial(_attn_kernel, scale=scale, n_kv=1, pipelined=False),
        grid=(b, MLA_HEADS),
        in_specs=[pl.BlockSpec((CHUNK, MLA_QK), ctx_q), pl.BlockSpec((CHUNK, MLA_QK), ctx_q),
                  pl.BlockSpec((MLA_V, CHUNK), lambda bi, hd: (hd, bi))],
        out_specs=pl.BlockSpec((CHUNK, MLA_V), ctx_q),
        out_shape=jax.ShapeDtypeStruct((lay.n_ctx, vw), BF16),
        scratch_shapes=[pltpu.VMEM((CHUNK, CHUNK), F32)],
        compiler_params=_params("parallel", "parallel"),
        name="mla_attn_ctx",
    )(q, k, vt)

    lat_blk = lay.n_ctx // s
    q_tiles = s // TQ
    lat_q = lambda bi, hd, t: (lay.n_ctx // TQ + bi * q_tiles + jnp.minimum(t, q_tiles - 1), hd)
    ctx_kv = lambda bi, hd, t: (bi, hd)
    lat_kv = lambda bi, hd, t: (lat_blk + bi, hd)
    att_lat = pl.pallas_call(
        functools.partial(_attn_kernel, scale=scale, n_kv=2, pipelined=True),
        grid=(b, MLA_HEADS, q_tiles + 1),
        in_specs=[pl.BlockSpec((TQ, MLA_QK), lat_q),
                  pl.BlockSpec((CHUNK, MLA_QK), ctx_kv),
                  pl.BlockSpec((MLA_V, CHUNK), lambda bi, hd, t: (hd, bi)),
                  pl.BlockSpec((s, MLA_QK), lat_kv),
                  pl.BlockSpec((MLA_V, s), lambda bi, hd, t: (hd, lat_blk + bi))],
        out_specs=pl.BlockSpec((TQ, MLA_V), lambda bi, hd, t: (bi * q_tiles + jnp.maximum(t - 1, 0), hd)),
        out_shape=jax.ShapeDtypeStruct((n - lay.n_ctx, vw), BF16),
        scratch_shapes=[pltpu.VMEM((CHUNK + s, TQ), F32)] * 2 + [pltpu.VMEM((1, TQ), F32)] * 2,
        compiler_params=_params("parallel", "parallel", "arbitrary"),
        name="mla_attn_lat",
    )(q, k, vt, k, vt)
    return att_ctx, att_lat


def _post_tail(y, h, mod_ref, ln_ref, wr_ref, br_ref, h_out, t_out, lg_out, alpha):
    h1 = _layer_norm(alpha * h + mod_ref[2:3, :] * y, ln_ref[0:1, :], ln_ref[1:2, :])
    t = h1 * (1.0 + mod_ref[4:5, :]) + mod_ref[3:4, :]
    t = t.astype(BF16)
    h_out[...] = h1
    t_out[...] = t
    lg_out[...] = _dot(t, wr_ref[...]) + br_ref[...]


def _post_kernel(*refs, alpha, ctx_tiles, n_a):
    a_refs, h_refs = refs[:n_a], refs[n_a:-8]
    mod_ref, w_ref, ln_ref, wr_ref, br_ref, h_out, t_out, lg_out = refs[-8:]
    y = _dot(_row_tile(a_refs, ctx_tiles), w_ref[...])
    _post_tail(y, _row_tile(h_refs, ctx_tiles), mod_ref, ln_ref, wr_ref, br_ref, h_out, t_out, lg_out, alpha)


def _post(lay, a, h, mod, w_out, ln, w_router, b_router, alpha):
    n, d = lay.n, w_out.shape[1]
    row = lambda i: (i, 0)
    ctx_tiles = lay.n_ctx // TM
    a, a_specs = _row_specs(a, TM, ctx_tiles)
    h, h_specs = _row_specs(h, TM, ctx_tiles)
    ka = a[0].shape[1]
    return pl.pallas_call(
        functools.partial(_post_kernel, alpha=alpha, ctx_tiles=ctx_tiles, n_a=len(a)),
        grid=(n // TM,),
        in_specs=a_specs + h_specs + [
            pl.BlockSpec((None, 6, d), lambda i: (lay.mod_row(i, TM), 0, 0)),
            _const_spec((ka, d)), _const_spec((2, d)), _const_spec((d, ROUTER_W)), _const_spec((1, ROUTER_W)),
        ],
        out_specs=[pl.BlockSpec((TM, d), row), pl.BlockSpec((TM, d), row), pl.BlockSpec((TM, ROUTER_W), row)],
        out_shape=[jax.ShapeDtypeStruct((n, d), F32), jax.ShapeDtypeStruct((n, d), BF16),
                   jax.ShapeDtypeStruct((n, ROUTER_W), F32)],
        compiler_params=_params("parallel"),
        name="post_mixer",
    )(*a, *h, mod, w_out.astype(BF16), ln, w_router, b_router)


HALO = 8


def _conv_kernel(h_ref, hp_ref, hn_ref, mod_ref, win_ref, cw_ref, cb_ref, w_ref, ln_ref, wr_ref, br_ref,
                 h_out, t_out, lg_out, *, alpha, tiles_per_seq, ctx_tiles):
    i = pl.program_id(0)
    d = h_ref.shape[1]
    h = h_ref[...]
    hx = jnp.concatenate([hp_ref[...], h, hn_ref[...]], axis=0)
    u = (hx * (1.0 + mod_ref[1:2, :]) + mod_ref[0:1, :]).astype(BF16)
    y = _dot(u, win_ref[...])
    gate_b = y[HALO:HALO + CHUNK, :d]
    zx = y[:, d:2 * d] * y[:, 2 * d:]
    z = zx[HALO:HALO + CHUNK]
    pos = jnp.where(i < ctx_tiles, 0, (i - ctx_tiles) % tiles_per_seq)
    last = jnp.where(i < ctx_tiles, 0, tiles_per_seq - 1)
    z_before = jnp.where(pos == 0, 0.0, zx[HALO - 1:HALO])
    z_after = jnp.where(pos == last, 0.0, zx[HALO + CHUNK:HALO + CHUNK + 1])
    r = lax.broadcasted_iota(jnp.int32, (CHUNK, 1), 0)
    z_prev = jnp.where(r == 0, z_before, pltpu.roll(z, 1, 0))
    z_next = jnp.where(r == CHUNK - 1, z_after, pltpu.roll(z, CHUNK - 1, 0))
    conv = cw_ref[0:1, :] * z_prev + cw_ref[1:2, :] * z + cw_ref[2:3, :] * z_next + cb_ref[...]
    yo = _dot((gate_b * conv).astype(BF16), w_ref[...])
    _post_tail(yo, h, mod_ref, ln_ref, wr_ref, br_ref, h_out, t_out, lg_out, alpha)


def _conv_layer(lay, h, mod, w_in, cw, cb, w_out, ln, w_router, b_router, alpha):
    n, d = h.shape
    n_tiles = n // CHUNK
    per = CHUNK // HALO
    row = lambda i: (i, 0)
    return pl.pallas_call(
        functools.partial(_conv_kernel, alpha=alpha, tiles_per_seq=lay.seq // CHUNK, ctx_tiles=lay.n_ctx // CHUNK),
        grid=(n_tiles,),
        in_specs=[
            pl.BlockSpec((CHUNK, d), row),
            pl.BlockSpec((HALO, d), lambda i: (jnp.maximum(i * per - 1, 0), 0)),
            pl.BlockSpec((HALO, d), lambda i: (jnp.minimum((i + 1) * per, n_tiles * per - 1), 0)),
            pl.BlockSpec((None, 6, d), lambda i: (lay.mod_row(i, CHUNK), 0, 0)),
            _const_spec((d, 3 * d)), _const_spec((3, d)), _const_spec((1, d)), _const_spec((d, d)),
            _const_spec((2, d)), _const_spec((d, ROUTER_W)), _const_spec((1, ROUTER_W)),
        ],
        out_specs=[pl.BlockSpec((CHUNK, d), row), pl.BlockSpec((CHUNK, d), row),
                   pl.BlockSpec((CHUNK, ROUTER_W), row)],
        out_shape=[jax.ShapeDtypeStruct((n, d), F32), jax.ShapeDtypeStruct((n, d), BF16),
                   jax.ShapeDtypeStruct((n, ROUTER_W), F32)],
        compiler_params=_params("parallel"),
        name="conv_layer",
    )(h, h, h, mod, w_in.astype(BF16), cw.astype(F32), cb.reshape(1, d).astype(F32), w_out.astype(BF16),
      ln, w_router, b_router)


def _route_kernel(lg_ref, tri_ref, upper_ref, ext_ref, lp_ref, lpt_ref, tab_ref, cnt_ref, base_ref):
    @pl.when(pl.program_id(0) == 0)
    def _():
        base_ref[...] = jnp.zeros_like(base_ref)

    x = lg_ref[...]
    lane = lax.broadcasted_iota(jnp.int32, x.shape, 1)

    def softmax(mask):
        m = jnp.max(jnp.where(mask, x, -jnp.inf), axis=-1, keepdims=True)
        e = jnp.where(mask, jnp.exp(x - m), 0.0)
        return e / jnp.sum(e, axis=-1, keepdims=True)

    def top1(prob, mask):
        p = jnp.max(jnp.where(mask, prob, -1.0), axis=-1, keepdims=True)
        i = jnp.min(jnp.where(mask & (prob == p), lane, ROUTER_W), axis=-1, keepdims=True)
        return p, i

    g_mask = lane < MOE_GROUPS
    g_p, g_idx = top1(softmax(g_mask), g_mask)
    e_lo = MOE_GROUPS + MOE_EPG * g_idx
    e_mask = (lane >= e_lo) & (lane < e_lo + MOE_EPG)
    e_prob = softmax(e_mask)
    p1, i1 = top1(e_prob, e_mask)
    p2, i2 = top1(e_prob, e_mask & (lane != i1))
    denom = p1 + p2
    gate1 = g_p * p1 / denom
    gate2 = g_p * p2 / denom

    sel1 = lane == i1
    sel2 = lane == i2
    cnt = (sel1 | sel2).astype(F32)
    within = _dot(tri_ref[...], cnt.astype(BF16))
    cnt_tile = jnp.sum(cnt, axis=0, keepdims=True)
    ci = jnp.broadcast_to(cnt_tile, (8, ROUTER_W)).astype(jnp.int32)
    ci = (ci + (RUN_ALIGN - 1)) & ~(RUN_ALIGN - 1)
    run_tile = ci[0:1].astype(F32)
    start = (_dot((ci >> 4).astype(F32).astype(BF16), upper_ref[...]) * 16.0
             + _dot((ci & 15).astype(F32).astype(BF16), upper_ref[...]))[0:1]
    where_to = within + start
    lp1 = jnp.sum(jnp.where(sel1, where_to, 0.0), axis=-1, keepdims=True)
    lp2 = jnp.sum(jnp.where(sel2, where_to, 0.0), axis=-1, keepdims=True)

    before = base_ref[...]
    total = before + run_tile
    base_ref[...] = total
    cnt_ref[...] = jnp.broadcast_to(total, cnt_ref.shape)
    row = lax.broadcasted_iota(jnp.int32, tab_ref.shape, 0)
    tab_ref[...] = jnp.where(row == 0, run_tile, jnp.where(row == 1, before, jnp.where(row == 2, start, 0.0)))

    def pieces(g):
        hi = g.astype(BF16).astype(F32)
        mid = (g - hi).astype(BF16).astype(F32)
        return hi, mid, g - hi - mid

    a1, b1, c1 = pieces(gate1)
    a2, b2, c2 = pieces(gate2)
    cols = (a1, b1, c1, a2, b2, c2, (i1 - MOE_GROUPS).astype(F32))
    ext = jnp.zeros(x.shape, F32)
    for j, col in enumerate(cols):
        ext = jnp.where(lane == j, col, ext)
    ext_ref[...] = ext.astype(BF16)
    lp = jnp.where(lane == 0, lp1, jnp.where(lane == 1, lp2, 0.0))
    lp_ref[...] = lp
    lpt_ref[...] = lp.T[0:8, :]


def _route(logits):
    n = logits.shape[0]
    n_tiles = n // TOK_TILE
    r = lax.broadcasted_iota(jnp.int32, (TOK_TILE, TOK_TILE), 0)
    c = lax.broadcasted_iota(jnp.int32, (TOK_TILE, TOK_TILE), 1)
    tri = (c < r).astype(BF16)
    r = lax.broadcasted_iota(jnp.int32, (ROUTER_W, ROUTER_W), 0)
    c = lax.broadcasted_iota(jnp.int32, (ROUTER_W, ROUTER_W), 1)
    upper = (r < c).astype(BF16)
    row = lambda i: (i, 0)
    return pl.pallas_call(
        _route_kernel,
        grid=(n_tiles,),
        in_specs=[pl.BlockSpec((TOK_TILE, ROUTER_W), row), _const_spec((TOK_TILE, TOK_TILE)),
                  _const_spec((ROUTER_W, ROUTER_W))],
        out_specs=[pl.BlockSpec((TOK_TILE, ROUTER_W), row), pl.BlockSpec((TOK_TILE, ROUTER_W), row),
                   pl.BlockSpec((8, TOK_TILE), row), pl.BlockSpec((8, ROUTER_W), row),
                   pl.BlockSpec((8, ROUTER_W), lambda i: (0, 0))],
        out_shape=[jax.ShapeDtypeStruct((n, ROUTER_W), BF16), jax.ShapeDtypeStruct((n, ROUTER_W), F32),
                   jax.ShapeDtypeStruct((n_tiles * 8, TOK_TILE), F32),
                   jax.ShapeDtypeStruct((n_tiles * 8, ROUTER_W), F32),
                   jax.ShapeDtypeStruct((8, ROUTER_W), F32)],
        scratch_shapes=[pltpu.VMEM((1, ROUTER_W), F32)],
        compiler_params=_params("arbitrary"),
        name="moe_route",
    )(logits, tri, upper)


def _plan(tables, counts, n_blocks):
    n_tiles = tables.shape[0] // 8
    tables = tables.reshape(n_tiles, 8, ROUTER_W)[:, :3, MOE_GROUPS:MOE_GROUPS + MOE_EXPERTS].astype(jnp.int32)
    counts = counts[0, MOE_GROUPS:MOE_GROUPS + MOE_EXPERTS].astype(jnp.int32)
    padded = (counts + EXP_BLOCK - 1) // EXP_BLOCK * EXP_BLOCK
    pad_ends = jnp.cumsum(padded)
    pad_starts = pad_ends - padded
    run_len = tables[:, 0].reshape(-1)
    run_src = tables[:, 2].reshape(-1)
    run_dst = (tables[:, 1] + pad_starts[None, :]).reshape(-1)
    first_row = jnp.arange(n_blocks, dtype=jnp.int32) * EXP_BLOCK
    block_e = jnp.minimum(jnp.sum(first_row[:, None] >= pad_ends[None, :], axis=1),
                          MOE_EXPERTS - 1).astype(jnp.int32)
    n_valid = (pad_ends[-1] // EXP_BLOCK).astype(jnp.int32).reshape(1)
    clear_row = jnp.concatenate([pad_starts + counts, pad_ends[-1:]])
    clear_len = jnp.concatenate([padded - counts, n_blocks - n_valid])
    return run_len, run_src, run_dst, clear_row, clear_len, block_e, n_valid


RUN_BITS = tuple(b for b in (1 << k for k in range(TOK_TILE.bit_length() - 1, -1, -1)) if b >= RUN_ALIGN)
RARE_BITS = 4
SORT_ROWS = MOE_TOPK * TOK_TILE + MOE_EXPERTS * RUN_ALIGN


def _for_each_piece(length, src, dst, fn):
    def pieces(bits):
        for bit in bits:
            above = length & ~(2 * bit - 1)

            @pl.when((length & bit) != 0)
            def _(bit=bit, above=above):
                fn(pl.multiple_of(src + above, RUN_ALIGN), pl.multiple_of(dst + above, RUN_ALIGN), bit)

    @pl.when(length >= RUN_BITS[RARE_BITS - 1])
    def _():
        pieces(RUN_BITS[:RARE_BITS])
    pieces(RUN_BITS[RARE_BITS:])


def _dispatch_kernel(len_ref, src_ref, dst_ref, zrow_ref, zlen_ref, t_ref, ext_ref, lpt_ref, xs_ref,
                     buf, zeros, sem, zsem, *, n_tiles):
    i = pl.program_id(0)
    half = t_ref.shape[1] // 2

    @pl.when(i == 0)
    def _():
        zeros[...] = jnp.zeros_like(zeros)

        def clear(e, c):
            _for_each_piece(zlen_ref[e], 0, zrow_ref[e], lambda s, d, rows: pltpu.make_async_copy(
                zeros.at[pl.ds(0, rows)], xs_ref.at[pl.ds(d, rows)], zsem).start())
            return c
        lax.fori_loop(0, MOE_EXPERTS, clear, 0)

        def drain(e, c):
            _for_each_piece(zlen_ref[e], 0, zrow_ref[e], lambda s, d, rows: pltpu.make_async_copy(
                zeros.at[pl.ds(0, rows)], xs_ref.at[pl.ds(d, rows)], zsem).wait())
            return c
        lax.fori_loop(0, MOE_EXPERTS, drain, 0)

        def tail_copy(bk):
            row = pl.multiple_of(zrow_ref[MOE_EXPERTS] + bk * EXP_BLOCK, EXP_BLOCK)
            return pltpu.make_async_copy(zeros.at[pl.ds(0, EXP_BLOCK)], xs_ref.at[pl.ds(row, EXP_BLOCK)], zsem)

        def clear_tail(bk, c):
            tail_copy(bk).start()
            return c
        lax.fori_loop(0, zlen_ref[MOE_EXPERTS], clear_tail, 0)

        def drain_tail(bk, c):
            tail_copy(bk).wait()
            return c
        lax.fori_loop(0, zlen_ref[MOE_EXPERTS], drain_tail, 0)

    slot = i % 2

    def scatter(tile, sl, start):
        def body(e, c):
            k = tile * MOE_EXPERTS + e

            def piece(s, d, rows):
                cp = pltpu.make_async_copy(buf.at[sl, pl.ds(s, rows)], xs_ref.at[pl.ds(d, rows)], sem.at[sl])
                cp.start() if start else cp.wait()
            _for_each_piece(len_ref[k], src_ref[k], dst_ref[k], piece)
            return c
        lax.fori_loop(0, MOE_EXPERTS, body, 0)

    @pl.when(i >= 2)
    def _():
        scatter(i - 2, slot, False)

    lp = lpt_ref[...].astype(jnp.int32)
    j = lax.broadcasted_iota(jnp.int32, (SORT_ROWS, TOK_TILE), 0)
    perm = ((j == lp[0:1, :]) | (j == lp[1:2, :])).astype(F32).astype(BF16)
    rhs = jnp.concatenate([t_ref[...], ext_ref[...]], axis=1)
    srt = _dot(perm, rhs)
    bits = lax.bitcast_convert_type(srt, jnp.uint32)
    buf[slot, :, :half] = (bits[:, :half] >> 16) | (bits[:, half:2 * half] & jnp.uint32(0xFFFF0000))
    buf[slot, :, half:] = bits[:, 2 * half:]
    scatter(i, slot, True)

    @pl.when(i == n_tiles - 1)
    def _():
        if n_tiles > 1:
            scatter(i - 1, 1 - slot, False)
        scatter(i, slot, False)


def _expert_kernel(be_ref, nv_ref, x_ref, w1_ref, w3_ref, w2_ref, o_ref):
    i = pl.program_id(0)
    valid = i < nv_ref[0]
    half = o_ref.shape[1] // 2

    @pl.when(valid)
    def _():
        packed = x_ref[:, :half]
        lo = lax.bitcast_convert_type(packed << 16, F32)
        hi = lax.bitcast_convert_type(packed & jnp.uint32(0xFFFF0000), F32)
        x = jnp.concatenate([lo, hi], axis=1).astype(BF16)
        ext = lax.bitcast_convert_type(x_ref[:, half:], F32)
        g1 = ext[:, 0:1] + ext[:, 1:2] + ext[:, 2:3]
        g2 = ext[:, 3:4] + ext[:, 4:5] + ext[:, 5:6]
        gate = jnp.where(ext[:, 6:7] == be_ref[i].astype(F32), g1, g2)
        hdn = _silu(_dot(x, w1_ref[...])) * _dot(x, w3_ref[...])
        o_ref[...] = _dot(hdn.astype(BF16), w2_ref[...]) * gate

    @pl.when(jnp.logical_not(valid))
    def _():
        o_ref[...] = jnp.zeros_like(o_ref)


def _combine_kernel(len_ref, src_ref, dst_ref, ys_ref, lp_ref, h_ref, mod_ref, ln_ref, o_ref, buf, sem,
                    *, alpha, first_tile, n_steps):
    step = pl.program_id(0)
    tile = step + first_tile
    slot = step % 2

    def gather(tl, sl, start):
        def body(e, c):
            k = tl * MOE_EXPERTS + e

            def piece(s, d, rows):
                cp = pltpu.make_async_copy(ys_ref.at[pl.ds(d, rows)], buf.at[sl, pl.ds(s, rows)], sem.at[sl])
                cp.start() if start else cp.wait()
            _for_each_piece(len_ref[k], src_ref[k], dst_ref[k], piece)
            return c
        lax.fori_loop(0, MOE_EXPERTS, body, 0)

    @pl.when(step == 0)
    def _():
        buf[...] = jnp.zeros_like(buf)
        gather(tile, slot, True)

    @pl.when(step + 1 < n_steps)
    def _():
        gather(tile + 1, 1 - slot, True)
    gather(tile, slot, False)

    lp = lp_ref[...].astype(jnp.int32)
    j = lax.broadcasted_iota(jnp.int32, (TOK_TILE, SORT_ROWS), 1)
    pick = ((j == lp[:, 0:1]) | (j == lp[:, 1:2])).astype(F32).astype(BF16)
    y = buf[slot]
    hi = y.astype(BF16)
    r1 = y - hi.astype(F32)
    mid = r1.astype(BF16)
    lo = (r1 - mid.astype(F32)).astype(BF16)
    f = _dot(pick, hi) + _dot(pick, mid) + _dot(pick, lo)
    o_ref[...] = _layer_norm(alpha * h_ref[...] + mod_ref[5:6, :] * f, ln_ref[0:1, :], ln_ref[1:2, :])


def _moe(lay, t, logits, h, mod, ln, w1, w3, w2, alpha, first_tile=0):
    n, d = t.shape
    hid = w1.shape[2]
    n_tiles = n // TOK_TILE
    max_rows = n * MOE_TOPK + n_tiles * MOE_EXPERTS * (RUN_ALIGN - 1)
    n_blocks = -(-max_rows // EXP_BLOCK) + MOE_EXPERTS
    n_rows = n_blocks * EXP_BLOCK
    xs_w = d // 2 + ROUTER_W
    ext, lp, lpt, tables, counts = _route(logits)
    run_len, run_src, run_dst, zrow, zlen, block_e, n_valid = _plan(tables, counts, n_blocks)

    xs = pl.pallas_call(
        functools.partial(_dispatch_kernel, n_tiles=n_tiles),
        grid_spec=pltpu.PrefetchScalarGridSpec(
            num_scalar_prefetch=5,
            grid=(n_tiles,),
            in_specs=[pl.BlockSpec((TOK_TILE, d), lambda i, *_: (i, 0)),
                      pl.BlockSpec((TOK_TILE, ROUTER_W), lambda i, *_: (i, 0)),
                      pl.BlockSpec((8, TOK_TILE), lambda i, *_: (i, 0))],
            out_specs=pl.BlockSpec(memory_space=pl.ANY),
            scratch_shapes=[pltpu.VMEM((2, SORT_ROWS, xs_w), jnp.uint32),
                            pltpu.VMEM((max(TOK_TILE, EXP_BLOCK), xs_w), jnp.uint32),
                            pltpu.SemaphoreType.DMA((2,)), pltpu.SemaphoreType.DMA],
        ),
        out_shape=jax.ShapeDtypeStruct((n_rows, xs_w), jnp.uint32),
        compiler_params=_params("arbitrary"),
        name="moe_dispatch",
    )(run_len, run_src, run_dst, zrow, zlen, t, ext, lpt)

    blk = lambda i, be, nv: (jnp.minimum(i, nv[0] - 1), 0)
    ys = pl.pallas_call(
        _expert_kernel,
        grid_spec=pltpu.PrefetchScalarGridSpec(
            num_scalar_prefetch=2,
            grid=(n_blocks,),
            in_specs=[
                pl.BlockSpec((EXP_BLOCK, xs_w), blk),
                pl.BlockSpec((None, d, hid), lambda i, be, nv: (be[i], 0, 0)),
                pl.BlockSpec((None, d, hid), lambda i, be, nv: (be[i], 0, 0)),
                pl.BlockSpec((None, hid, d), lambda i, be, nv: (be[i], 0, 0)),
            ],
            out_specs=pl.BlockSpec((EXP_BLOCK, d), lambda i, be, nv: (i, 0)),
        ),
        out_shape=jax.ShapeDtypeStruct((n_rows, d), F32),
        compiler_params=_params("arbitrary"),
        name="moe_experts",
    )(block_e, n_valid, xs, w1.astype(BF16), w3.astype(BF16), w2.astype(BF16))

    n_steps = n_tiles - first_tile
    return pl.pallas_call(
        functools.partial(_combine_kernel, alpha=alpha, first_tile=first_tile, n_steps=n_steps),
        grid_spec=pltpu.PrefetchScalarGridSpec(
            num_scalar_prefetch=3,
            grid=(n_steps,),
            in_specs=[pl.BlockSpec(memory_space=pl.ANY),
                      pl.BlockSpec((TOK_TILE, ROUTER_W), lambda i, *_: (i + first_tile, 0)),
                      pl.BlockSpec((TOK_TILE, d), lambda i, *_: (i + first_tile, 0)),
                      pl.BlockSpec((None, 6, d), lambda i, *_: (lay.mod_row(i + first_tile, TOK_TILE), 0, 0)),
                      pl.BlockSpec((2, d), lambda i, *_: (0, 0))],
            out_specs=pl.BlockSpec((TOK_TILE, d), lambda i, *_: (i, 0)),
            scratch_shapes=[pltpu.VMEM((2, SORT_ROWS, d), F32), pltpu.SemaphoreType.DMA((2,))],
        ),
        out_shape=jax.ShapeDtypeStruct((n_steps * TOK_TILE, d), F32),
        compiler_params=_params("arbitrary"),
        name="moe_combine",
    )(run_len, run_src, run_dst, ys, lp, h, mod, ln)


def kernel(x, c, ctx, c_ctx, ada_w, ada_b, ln_g, ln_b, ret_w_in, ret_decay, ret_gn_g, ret_w_out, mla_w_down, mla_q_norm, mla_kv_norm, mla_w_uq, mla_w_ukv, mla_w_out, conv_w_in, conv_w, conv_b, conv_w_out, moe_w_group, moe_b_group, moe_w_expert, moe_b_expert, moe_w1, moe_w3, moe_w2):
    b, s, d = x.shape
    depth = ada_w.shape[0]
    lay = _Layout(b, s, ctx.shape[1])
    alpha = (2.0 * depth) ** 0.25

    mod_rows = -(-(b + 1) // 8) * 8
    cc = jnp.concatenate([c, c_ctx[None, :], jnp.zeros((mod_rows - b - 1, d), F32)], axis=0)
    mod_all = _ada(cc, ada_w, ada_b).reshape(depth, mod_rows, 6, d)

    h = (ctx.reshape(lay.n_ctx, d), x.reshape(b * s, d))
    for i in range(depth):
        kind, j = i % N_MIXERS, i // N_MIXERS
        mod = mod_all[i]
        ln1 = jnp.stack([ln_g[i, 0], ln_b[i, 0]])
        ln2 = jnp.stack([ln_g[i, 1], ln_b[i, 1]])
        pad = jnp.zeros((d, ROUTER_W - MOE_GROUPS - MOE_EXPERTS), F32)
        w_router = jnp.concatenate([moe_w_group[i], moe_w_expert[i], pad], axis=1).astype(BF16)
        b_router = jnp.concatenate([moe_b_group[i], moe_b_expert[i], pad[0]]).reshape(1, ROUTER_W)
        if kind == 0:
            a = _retention(lay, h, mod, ret_w_in[j], ret_decay[j], ret_gn_g[j], d)
            h1, t, logits = _post(lay, a, h, mod, ret_w_out[j], ln1, w_router, b_router, alpha)
        elif kind == 1:
            a = _mla(lay, h, mod, mla_w_down[j], mla_q_norm[j], mla_kv_norm[j], mla_w_uq[j], mla_w_ukv[j], d)
            h1, t, logits = _post(lay, a, h, mod, mla_w_out[j], ln1, w_router, b_router, alpha)
        else:
            h1, t, logits = _conv_layer(lay, h, mod, conv_w_in[j], conv_w[j], conv_b[j], conv_w_out[j],
                                        ln1, w_router, b_router, alpha)
        first_tile = lay.n_ctx // TOK_TILE if i == depth - 1 else 0
        h = _moe(lay, t, logits, h1, mod, ln2, moe_w1[i], moe_w3[i], moe_w2[i], alpha, first_tile)
    return h.reshape(b, s, d)
```

```python
import functools

import jax
import jax.numpy as jnp
from jax import lax
from jax.experimental import pallas as pl
from jax.experimental.pallas import tpu as pltpu

F32 = jnp.float32
BF16 = jnp.bfloat16

GRID_W = 64
LN_EPS = 1e-5
RMS_EPS = 1e-6
ROPE_BASE = 10000.0
N_MIXERS = 3
RET_HEADS = 4
MLA_HEADS = 8
MLA_NOPE = 128
MLA_ROPE = 64
MLA_V = 128
MLA_QK = 256
MOE_GROUPS = 4
MOE_EPG = 8
MOE_EXPERTS = MOE_GROUPS * MOE_EPG
MOE_TOPK = 2
ROUTER_W = 128

CHUNK = 256
TM = 512
TQ = 512
TOK_TILE = 512
EXP_BLOCK = 512
RUN_ALIGN = 8
VMEM_LIMIT = 56 * 1024 * 1024


def _params(*sem):
    return pltpu.CompilerParams(dimension_semantics=sem, vmem_limit_bytes=VMEM_LIMIT)


def _const_spec(shape):
    nd = len(shape)
    return pl.BlockSpec(shape, lambda *_: (0,) * nd, pipeline_mode=pl.Buffered(1))


def _dot(a, b):
    return jnp.dot(a, b, preferred_element_type=F32)


def _dot_nt(a, b):
    return lax.dot_general(a, b, (((1,), (1,)), ((), ())), preferred_element_type=F32)


def _dot_tn(a, b):
    return lax.dot_general(a, b, (((0,), (0,)), ((), ())), preferred_element_type=F32)


def _silu(x):
    return x * jax.nn.sigmoid(x)


def _layer_norm(x, g, b):
    mu = jnp.mean(x, axis=-1, keepdims=True)
    xc = x - mu
    var = jnp.mean(xc * xc, axis=-1, keepdims=True)
    return xc * lax.rsqrt(var + LN_EPS) * g + b


def _rms_norm(x, g):
    return x * lax.rsqrt(jnp.mean(x * x, axis=-1, keepdims=True) + RMS_EPS) * g


def _ada_kernel(c_ref, w_ref, b_ref, o_ref):
    a = _silu(c_ref[...]).astype(BF16)
    o_ref[...] = _dot(a, w_ref[...].astype(BF16)) + b_ref[...]


def _ada(cc, ada_w, ada_b):
    depth, d, n6 = ada_w.shape
    rows = cc.shape[0]
    tn = 1536
    return pl.pallas_call(
        _ada_kernel,
        grid=(depth, n6 // tn),
        in_specs=[
            pl.BlockSpec((rows, d), lambda l, j: (0, 0)),
            pl.BlockSpec((None, d, tn), lambda l, j: (l, 0, j)),
            pl.BlockSpec((None, 1, tn), lambda l, j: (l, 0, j)),
        ],
        out_specs=pl.BlockSpec((None, rows, tn), lambda l, j: (l, 0, j)),
        out_shape=jax.ShapeDtypeStruct((depth, rows, n6), F32),
        compiler_params=_params("parallel", "parallel"),
        name="ada_mod",
    )(cc, ada_w, ada_b.reshape(depth, 1, n6))


class _Layout:
    def __init__(self, batch, seq, ctx_len):
        assert ctx_len == CHUNK and seq % TM == 0 and seq % CHUNK == 0
        self.batch, self.seq, self.ctx = batch, seq, ctx_len
        self.n_ctx = batch * ctx_len
        self.n = self.n_ctx + batch * seq
        assert self.n_ctx % TM == 0 and self.n_ctx % seq == 0 and self.n_ctx % TOK_TILE == 0
        assert self.n_ctx % TQ == 0 and seq % TQ == 0 and seq % TOK_TILE == 0

    def mod_row(self, tile, rows_per_tile):
        ctx_tiles = self.n_ctx // rows_per_tile
        per_batch = self.seq // rows_per_tile
        return jnp.where(tile < ctx_tiles, self.batch, (tile - ctx_tiles) // per_batch)

    def pos_block(self, tile, rows_per_tile):
        ctx_tiles = self.n_ctx // rows_per_tile
        per_batch = self.seq // rows_per_tile
        return jnp.where(tile < ctx_tiles, 0, 1 + (tile - ctx_tiles) % per_batch)


def _row_specs(x, rows, ctx_tiles):
    if not isinstance(x, tuple):
        return (x,), [pl.BlockSpec((rows, x.shape[1]), lambda i: (i, 0))]
    width = x[0].shape[1]
    return x, [pl.BlockSpec((rows, width), lambda i: (jnp.minimum(i, ctx_tiles - 1), 0)),
               pl.BlockSpec((rows, width), lambda i: (jnp.maximum(i - ctx_tiles, 0), 0))]


def _row_tile(refs, ctx_tiles):
    if len(refs) == 1:
        return refs[0][...]
    return jnp.where(pl.program_id(0) < ctx_tiles, refs[0][...], refs[1][...])


def _rope_angles(seq, dim):
    n_rows = seq // GRID_W
    rows = jnp.repeat(jnp.arange(n_rows, dtype=F32), GRID_W)
    cols = jnp.tile(jnp.arange(GRID_W, dtype=F32), n_rows)
    quarter = dim // 4
    inv_freq = ROPE_BASE ** (-jnp.arange(quarter, dtype=F32) / quarter)
    return jnp.concatenate([rows[:, None] * inv_freq, cols[:, None] * inv_freq], axis=-1)


def _with_identity_block(table, fill, rows):
    ident = jnp.full((rows, table.shape[1]), fill, F32)
    return jnp.concatenate([ident, table], axis=0)


def _proj_ret_kernel(*refs, dk, ctx_tiles):
    h_refs = refs[:-14]
    (mod_ref, cos_ref, sin_ref, kdec_ref, wq_ref, wk_ref, wv_ref, wg_ref,
     q_ref, k_ref, kf_ref, kb_ref, v_ref, g_ref) = refs[-14:]
    u = (_row_tile(h_refs, ctx_tiles) * (1.0 + mod_ref[1:2, :]) + mod_ref[0:1, :]).astype(BF16)
    cos = cos_ref[...]
    sin = sin_ref[...]
    half = dk // 2
    yq = _dot(u, wq_ref[...])
    for hd in range(RET_HEADS):
        x1 = yq[:, hd * dk:hd * dk + half]
        x2 = yq[:, hd * dk + half:(hd + 1) * dk]
        q_ref[:, hd * dk:hd * dk + half] = (x1 * cos - x2 * sin).astype(BF16)
        q_ref[:, hd * dk + half:(hd + 1) * dk] = (x1 * sin + x2 * cos).astype(BF16)
    yk = _dot(u, wk_ref[...])
    k_scale = dk ** -0.5
    for hd in range(RET_HEADS):
        x1 = yk[:, hd * dk:hd * dk + half]
        x2 = yk[:, hd * dk + half:(hd + 1) * dk]
        df = kdec_ref[:, hd:hd + 1]
        db = kdec_ref[:, RET_HEADS + hd:RET_HEADS + hd + 1]
        for part, o in ((0, (x1 * cos - x2 * sin) * k_scale), (1, (x1 * sin + x2 * cos) * k_scale)):
            sl = slice(hd * dk + part * half, hd * dk + (part + 1) * half)
            k_ref[:, sl] = o.astype(BF16)
            kf_ref[:, sl] = (o * df).astype(BF16)
            kb_ref[:, sl] = (o * db).astype(BF16)
    v_ref[...] = _dot(u, wv_ref[...]).astype(BF16)
    g_ref[...] = _silu(_dot(u, wg_ref[...]))


def _ret_scan_kernel(cdec_ref, q_ref, k_ref, kd_ref, v_ref, intra_ref, qdec_ref, *rest, dk, dv, direction):
    if direction == 0:
        o_ref, state_ref = rest
    else:
        of_ref, g_ref, gn_ref, o_ref, state_ref = rest

    @pl.when(pl.program_id(1) == 0)
    def _():
        state_ref[...] = jnp.zeros_like(state_ref)

    for hd in range(RET_HEADS):
        q = q_ref[:, hd * dk:(hd + 1) * dk]
        k = k_ref[:, hd * dk:(hd + 1) * dk]
        kd = kd_ref[:, hd * dk:(hd + 1) * dk]
        v = v_ref[:, hd * dv:(hd + 1) * dv]
        p = (_dot_nt(q, k) * intra_ref[hd]).astype(BF16)
        st = state_ref[hd]
        col = direction * RET_HEADS + hd
        o = _dot(p, v) + qdec_ref[:, col:col + 1] * _dot(q, st.astype(BF16))
        state_ref[hd] = st * cdec_ref[col] + _dot_tn(kd, v)
        if direction == 0:
            o_ref[:, hd * dv:(hd + 1) * dv] = o
        else:
            o = o + of_ref[:, hd * dv:(hd + 1) * dv]
            mu = jnp.mean(o, axis=-1, keepdims=True)
            oc = o - mu
            var = jnp.mean(oc * oc, axis=-1, keepdims=True)
            on = oc * lax.rsqrt(var + LN_EPS) * gn_ref[:, hd * dv:(hd + 1) * dv]
            o_ref[:, hd * dv:(hd + 1) * dv] = (g_ref[:, hd * dv:(hd + 1) * dv] * on).astype(BF16)


def _retention(lay, h, mod, w_in, decay_logit, gn_g, d_model):
    n, b, nc = lay.n, lay.batch, lay.seq // CHUNK
    dk = d_model // RET_HEADS
    dv = 2 * dk
    qk, vw = RET_HEADS * dk, RET_HEADS * dv
    w_in = w_in.astype(BF16)
    wq, wk, wv, wg = w_in[:, :qk], w_in[:, qk:2 * qk], w_in[:, 2 * qk:2 * qk + vw], w_in[:, 2 * qk + vw:]

    ang = _rope_angles(lay.seq, dk)
    cos_t = _with_identity_block(jnp.cos(ang), 1.0, TM)
    sin_t = _with_identity_block(jnp.sin(ang), 0.0, TM)

    lg = jax.nn.log_sigmoid(decay_logit.astype(F32))
    idx = jnp.arange(CHUNK, dtype=F32)
    k_pow = jnp.stack([CHUNK - 1.0 - idx, idx])
    q_pow = jnp.stack([idx + 1.0, CHUNK - idx])
    kdec = jnp.exp(k_pow[:, :, None] * lg[:, None, :])
    qdec = jnp.exp(q_pow[:, :, None] * lg[:, None, :])
    kdec = jnp.moveaxis(kdec, 0, 1).reshape(CHUNK, 2 * RET_HEADS)
    qdec = jnp.moveaxis(qdec, 0, 1).reshape(CHUNK, 2 * RET_HEADS)
    cdec = jnp.exp(CHUNK * lg).reshape(2 * RET_HEADS)
    rel = idx[:, None] - idx[None, :]
    rel = jnp.stack([rel, -rel])
    intra = jnp.where(rel[:, None] >= 0, jnp.exp(jnp.maximum(rel[:, None], 0.0) * lg[:, :, None, None]), 0.0)
    kdec_tm = jnp.tile(kdec, (TM // CHUNK, 1))

    n_tiles = n // TM
    row = lambda i: (i, 0)
    ctx_tiles = lay.n_ctx // TM
    h, h_specs = _row_specs(h, TM, ctx_tiles)
    q, k, kf, kb, v, g = pl.pallas_call(
        functools.partial(_proj_ret_kernel, dk=dk, ctx_tiles=ctx_tiles),
        grid=(n_tiles,),
        in_specs=h_specs + [
            pl.BlockSpec((None, 6, d_model), lambda i: (lay.mod_row(i, TM), 0, 0)),
            pl.BlockSpec((TM, dk // 2), lambda i: (lay.pos_block(i, TM), 0)),
            pl.BlockSpec((TM, dk // 2), lambda i: (lay.pos_block(i, TM), 0)),
            _const_spec((TM, 2 * RET_HEADS)),
            _const_spec((d_model, qk)), _const_spec((d_model, qk)),
            _const_spec((d_model, vw)), _const_spec((d_model, vw)),
        ],
        out_specs=[pl.BlockSpec((TM, qk), row)] * 4 + [pl.BlockSpec((TM, vw), row)] * 2,
        out_shape=[jax.ShapeDtypeStruct((n, qk), BF16)] * 4
        + [jax.ShapeDtypeStruct((n, vw), BF16), jax.ShapeDtypeStruct((n, vw), F32)],
        compiler_params=_params("parallel"),
        name="ret_proj",
    )(*h, mod, cos_t, sin_t, kdec_tm, wq, wk, wv, wg)

    def chunk_fwd(bi, c):
        return (jnp.where(c == 0, bi, b + bi * nc + c - 1), 0)

    def chunk_bwd(bi, c):
        return (jnp.where(c == 0, bi, b + bi * nc + nc - c), 0)

    def scan(direction, chunk_map, kd, extra_in, extra_specs, out_dtype, name):
        return pl.pallas_call(
            functools.partial(_ret_scan_kernel, dk=dk, dv=dv, direction=direction),
            grid=(b, nc + 1),
            in_specs=[
                pl.BlockSpec(memory_space=pltpu.SMEM),
                pl.BlockSpec((CHUNK, qk), chunk_map),
                pl.BlockSpec((CHUNK, qk), chunk_map),
                pl.BlockSpec((CHUNK, qk), chunk_map),
                pl.BlockSpec((CHUNK, vw), chunk_map),
                _const_spec((RET_HEADS, CHUNK, CHUNK)),
                _const_spec((CHUNK, 2 * RET_HEADS)),
            ] + extra_specs,
            out_specs=pl.BlockSpec((CHUNK, vw), chunk_map),
            out_shape=jax.ShapeDtypeStruct((n, vw), out_dtype),
            scratch_shapes=[pltpu.VMEM((RET_HEADS, dk, dv), F32)],
            compiler_params=_params("parallel", "arbitrary"),
            name=name,
        )(cdec, q, k, kd, v, intra[direction], qdec, *extra_in)

    o_f = scan(0, chunk_fwd, kf, [], [], F32, "ret_scan_fwd")
    return scan(1, chunk_bwd, kb, [o_f, g, gn_g.reshape(1, vw).astype(F32)],
                [pl.BlockSpec((CHUNK, vw), chunk_bwd), pl.BlockSpec((CHUNK, vw), chunk_bwd),
                 _const_spec((1, vw))], BF16, "ret_scan_bwd")


def _proj_mla_kernel(h_ref, mod_ref, ct_ref, st_ref, wdq_ref, wdkv_ref, wkr_ref, qn_ref, kvn_ref,
                     wuq_ref, wukn_ref, wuv_ref, q_ref, k_ref, vt_ref):
    u = (h_ref[...] * (1.0 + mod_ref[1:2, :]) + mod_ref[0:1, :]).astype(BF16)
    ct = ct_ref[...]
    st = st_ref[...]
    half = MLA_QK // 2

    def rope(x):
        return x * ct + pltpu.roll(x, half // 2, 1) * st

    cq = _rms_norm(_dot(u, wdq_ref[...]), qn_ref[...]).astype(BF16)
    yq = _dot(cq, wuq_ref[...])
    ckv = _rms_norm(_dot(u, wdkv_ref[...]), kvn_ref[...]).astype(BF16)
    kn = _dot(ckv, wukn_ref[...])
    kr = rope(_dot(u, wkr_ref[...])).astype(BF16)
    for hd in range(MLA_HEADS):
        q_ref[:, hd * MLA_QK:hd * MLA_QK + half] = yq[:, hd * MLA_QK:hd * MLA_QK + half].astype(BF16)
        q_ref[:, hd * MLA_QK + half:(hd + 1) * MLA_QK] = rope(
            yq[:, hd * MLA_QK + half:(hd + 1) * MLA_QK]).astype(BF16)
        k_ref[:, hd * MLA_QK:hd * MLA_QK + half] = kn[:, hd * half:(hd + 1) * half].astype(BF16)
        k_ref[:, hd * MLA_QK + half:(hd + 1) * MLA_QK] = kr
    vt_ref[...] = _dot(ckv, wuv_ref[...]).T.astype(BF16)


KV_CHUNK = 512
LOG2_E = 1.4426950408889634


def _kv_chunks(kv_refs, n_kv):
    chunks = []
    row = 0
    for j in range(n_kv):
        keys = kv_refs[2 * j].shape[0]
        step = min(keys, KV_CHUNK)
        for off in range(0, keys, step):
            chunks.append((j, off, row, step))
            row += step
    return chunks


def _attn_kernel(q_ref, *rest, scale, n_kv, pipelined):
    kv_refs = rest[:2 * n_kv]
    chunks = _kv_chunks(kv_refs, n_kv)

    def scores(c, s_ref, m):
        j, off, row, w = c
        s = _dot_nt(kv_refs[2 * j][off:off + w, :], q_ref[...])
        s_ref[row:row + w, :] = s
        cm = jnp.max(s, axis=0, keepdims=True)
        return cm if m is None else jnp.maximum(m, cm)

    def values(c, s_ref, m, acc, den):
        j, off, row, w = c
        p = jnp.exp2((s_ref[row:row + w, :] - m) * (scale * LOG2_E))
        cs = jnp.sum(p, axis=0, keepdims=True)
        pv = _dot(kv_refs[2 * j + 1][:, off:off + w], p.astype(BF16))
        return (pv if acc is None else acc + pv), (cs if den is None else den + cs)

    if not pipelined:
        o_ref, s_ref = rest[-2:]
        m = acc = den = None
        for c in chunks:
            m = scores(c, s_ref, m)
        for c in chunks:
            acc, den = values(c, s_ref, m, acc, den)
        o_ref[...] = (acc / den).T.astype(BF16)
        return

    o_ref, s0_ref, s1_ref, m0_ref, m1_ref = rest[-5:]
    t = pl.program_id(2)
    last = pl.num_programs(2) - 1
    bufs = ((s0_ref, m0_ref), (s1_ref, m1_ref))

    def step(new, old):
        m_prev = old[1][...] if old else None
        m = acc = den = None
        for c in chunks:
            if new:
                m = scores(c, new[0], m)
            if old:
                acc, den = values(c, old[0], m_prev, acc, den)
        if new:
            new[1][...] = m
        if old:
            o_ref[...] = (acc / den).T.astype(BF16)

    @pl.when(t == 0)
    def _():
        step(bufs[0], None)

    for parity in (0, 1):
        @pl.when((t > 0) & (t < last) & (t % 2 == parity))
        def _(parity=parity):
            step(bufs[parity], bufs[1 - parity])

        @pl.when((t == last) & (t % 2 == parity))
        def _(parity=parity):
            step(None, bufs[1 - parity])


def _pad_rope_cols(w):
    z = jnp.zeros((w.shape[0], MLA_ROPE // 2), w.dtype)
    return jnp.concatenate([w[:, :MLA_ROPE // 2], z, w[:, MLA_ROPE // 2:], z], axis=1)


def _mla(lay, h, mod, w_down, q_norm, kv_norm, w_uq, w_ukv, d_model):
    n, b, s = lay.n, lay.batch, lay.seq
    q_lora, kv_lora = q_norm.shape[0], kv_norm.shape[0]
    w_dq = w_down[:, :q_lora].astype(BF16)
    w_dkv = w_down[:, q_lora:q_lora + kv_lora].astype(BF16)
    w_kr = _pad_rope_cols(w_down[:, q_lora + kv_lora:]).astype(BF16)
    w_uq = w_uq.reshape(q_lora, MLA_HEADS, MLA_NOPE + MLA_ROPE)
    w_uq = jnp.concatenate(
        [w_uq[:, :, :MLA_NOPE], jax.vmap(_pad_rope_cols, 1, 1)(w_uq[:, :, MLA_NOPE:])], axis=2)
    w_uq = w_uq.reshape(q_lora, MLA_HEADS * MLA_QK).astype(BF16)
    w_ukv = w_ukv.reshape(kv_lora, MLA_HEADS, MLA_NOPE + MLA_V)
    w_ukn = w_ukv[:, :, :MLA_NOPE].reshape(kv_lora, MLA_HEADS * MLA_NOPE).astype(BF16)
    w_uv = w_ukv[:, :, MLA_NOPE:].reshape(kv_lora, MLA_HEADS * MLA_V).astype(BF16)

    ang = _rope_angles(s, MLA_ROPE)
    z = jnp.zeros_like(ang)
    ct = _with_identity_block(jnp.concatenate([jnp.cos(ang), z, jnp.cos(ang), z], axis=1), 1.0, TM)
    st = _with_identity_block(jnp.concatenate([-jnp.sin(ang), z, jnp.sin(ang), z], axis=1), 0.0, TM)

    row = lambda i: (i, 0)
    qw, vw = MLA_HEADS * MLA_QK, MLA_HEADS * MLA_V
    q, k, vt = pl.pallas_call(
        _proj_mla_kernel,
        grid=(n // TM,),
        in_specs=[
            pl.BlockSpec((TM, d_model), row),
            pl.BlockSpec((None, 6, d_model), lambda i: (lay.mod_row(i, TM), 0, 0)),
            pl.BlockSpec((TM, MLA_QK // 2), lambda i: (lay.pos_block(i, TM), 0)),
            pl.BlockSpec((TM, MLA_QK // 2), lambda i: (lay.pos_block(i, TM), 0)),
            _const_spec((d_model, q_lora)), _const_spec((d_model, kv_lora)), _const_spec((d_model, MLA_QK // 2)),
            _const_spec((1, q_lora)), _const_spec((1, kv_lora)),
            _const_spec((q_lora, qw)), _const_spec((kv_lora, vw)), _const_spec((kv_lora, vw)),
        ],
        out_specs=[pl.BlockSpec((TM, qw), row), pl.BlockSpec((TM, qw), row),
                   pl.BlockSpec((vw, TM), lambda i: (0, i))],
        out_shape=[jax.ShapeDtypeStruct((n, qw), BF16), jax.ShapeDtypeStruct((n, qw), BF16),
                   jax.ShapeDtypeStruct((vw, n), BF16)],
        compiler_params=_params("parallel"),
        name="mla_proj",
    )(h, mod, ct, st, w_dq, w_dkv, w_kr, q_norm.reshape(1, -1).astype(F32), kv_norm.reshape(1, -1).astype(F32),
      w_uq, w_ukn, w_uv)

    scale = (MLA_NOPE + MLA_ROPE) ** -0.5
    ctx_q = lambda bi, hd: (bi, hd)
    att_ctx = pl.pallas_call(
        functools.partial(_attn_kernel, scale=scale, n_kv=1, pipelined=False),
        grid=(b, MLA_HEADS),
        in_specs=[pl.BlockSpec((CHUNK, MLA_QK), ctx_q), pl.BlockSpec((CHUNK, MLA_QK), ctx_q),
                  pl.BlockSpec((MLA_V, CHUNK), lambda bi, hd: (hd, bi))],
        out_specs=pl.BlockSpec((CHUNK, MLA_V), ctx_q),
        out_shape=jax.ShapeDtypeStruct((lay.n_ctx, vw), BF16),
        scratch_shapes=[pltpu.VMEM((CHUNK, CHUNK), F32)],
        compiler_params=_params("parallel", "parallel"),
        name="mla_attn_ctx",
    )(q, k, vt)

    lat_blk = lay.n_ctx // s
    q_tiles = s // TQ
    lat_q = lambda bi, hd, t: (lay.n_ctx // TQ + bi * q_tiles + jnp.minimum(t, q_tiles - 1), hd)
    ctx_kv = lambda bi, hd, t: (bi, hd)
    lat_kv = lambda bi, hd, t: (lat_blk + bi, hd)
    att_lat = pl.pallas_call(
        functools.partial(_attn_kernel, scale=scale, n_kv=2, pipelined=True),
        grid=(b, MLA_HEADS, q_tiles + 1),
        in_specs=[pl.BlockSpec((TQ, MLA_QK), lat_q),
                  pl.BlockSpec((CHUNK, MLA_QK), ctx_kv),
                  pl.BlockSpec((MLA_V, CHUNK), lambda bi, hd, t: (hd, bi)),
                  pl.BlockSpec((s, MLA_QK), lat_kv),
                  pl.BlockSpec((MLA_V, s), lambda bi, hd, t: (hd, lat_blk + bi))],
        out_specs=pl.BlockSpec((TQ, MLA_V), lambda bi, hd, t: (bi * q_tiles + jnp.maximum(t - 1, 0), hd)),
        out_shape=jax.ShapeDtypeStruct((n - lay.n_ctx, vw), BF16),
        scratch_shapes=[pltpu.VMEM((CHUNK + s, TQ), F32)] * 2 + [pltpu.VMEM((1, TQ), F32)] * 2,
        compiler_params=_params("parallel", "parallel", "arbitrary"),
        name="mla_attn_lat",
    )(q, k, vt, k, vt)
    return att_ctx, att_lat


def _post_tail(y, h, mod_ref, ln_ref, wr_ref, br_ref, h_out, t_out, lg_out, alpha):
    h1 = _layer_norm(alpha * h + mod_ref[2:3, :] * y, ln_ref[0:1, :], ln_ref[1:2, :])
    t = h1 * (1.0 + mod_ref[4:5, :]) + mod_ref[3:4, :]
    t = t.astype(BF16)
    h_out[...] = h1
    t_out[...] = t
    lg_out[...] = _dot(t, wr_ref[...]) + br_ref[...]


def _post_kernel(*refs, alpha, ctx_tiles, n_a):
    a_refs, h_refs = refs[:n_a], refs[n_a:-8]
    mod_ref, w_ref, ln_ref, wr_ref, br_ref, h_out, t_out, lg_out = refs[-8:]
    y = _dot(_row_tile(a_refs, ctx_tiles), w_ref[...])
    _post_tail(y, _row_tile(h_refs, ctx_tiles), mod_ref, ln_ref, wr_ref, br_ref, h_out, t_out, lg_out, alpha)


def _post(lay, a, h, mod, w_out, ln, w_router, b_router, alpha):
    n, d = lay.n, w_out.shape[1]
    row = lambda i: (i, 0)
    ctx_tiles = lay.n_ctx // TM
    a, a_specs = _row_specs(a, TM, ctx_tiles)
    h, h_specs = _row_specs(h, TM, ctx_tiles)
    ka = a[0].shape[1]
    return pl.pallas_call(
        functools.partial(_post_kernel, alpha=alpha, ctx_tiles=ctx_tiles, n_a=len(a)),
        grid=(n // TM,),
        in_specs=a_specs + h_specs + [
            pl.BlockSpec((None, 6, d), lambda i: (lay.mod_row(i, TM), 0, 0)),
            _const_spec((ka, d)), _const_spec((2, d)), _const_spec((d, ROUTER_W)), _const_spec((1, ROUTER_W)),
        ],
        out_specs=[pl.BlockSpec((TM, d), row), pl.BlockSpec((TM, d), row), pl.BlockSpec((TM, ROUTER_W), row)],
        out_shape=[jax.ShapeDtypeStruct((n, d), F32), jax.ShapeDtypeStruct((n, d), BF16),
                   jax.ShapeDtypeStruct((n, ROUTER_W), F32)],
        compiler_params=_params("parallel"),
        name="post_mixer",
    )(*a, *h, mod, w_out.astype(BF16), ln, w_router, b_router)


HALO = 8


def _conv_kernel(h_ref, hp_ref, hn_ref, mod_ref, win_ref, cw_ref, cb_ref, w_ref, ln_ref, wr_ref, br_ref,
                 h_out, t_out, lg_out, *, alpha, tiles_per_seq, ctx_tiles):
    i = pl.program_id(0)
    d = h_ref.shape[1]
    h = h_ref[...]
    hx = jnp.concatenate([hp_ref[...], h, hn_ref[...]], axis=0)
    u = (hx * (1.0 + mod_ref[1:2, :]) + mod_ref[0:1, :]).astype(BF16)
    y = _dot(u, win_ref[...])
    gate_b = y[HALO:HALO + CHUNK, :d]
    zx = y[:, d:2 * d] * y[:, 2 * d:]
    z = zx[HALO:HALO + CHUNK]
    pos = jnp.where(i < ctx_tiles, 0, (i - ctx_tiles) % tiles_per_seq)
    last = jnp.where(i < ctx_tiles, 0, tiles_per_seq - 1)
    z_before = jnp.where(pos == 0, 0.0, zx[HALO - 1:HALO])
    z_after = jnp.where(pos == last, 0.0, zx[HALO + CHUNK:HALO + CHUNK + 1])
    r = lax.broadcasted_iota(jnp.int32, (CHUNK, 1), 0)
    z_prev = jnp.where(r == 0, z_before, pltpu.roll(z, 1, 0))
    z_next = jnp.where(r == CHUNK - 1, z_after, pltpu.roll(z, CHUNK - 1, 0))
    conv = cw_ref[0:1, :] * z_prev + cw_ref[1:2, :] * z + cw_ref[2:3, :] * z_next + cb_ref[...]
    yo = _dot((gate_b * conv).astype(BF16), w_ref[...])
    _post_tail(yo, h, mod_ref, ln_ref, wr_ref, br_ref, h_out, t_out, lg_out, alpha)


def _conv_layer(lay, h, mod, w_in, cw, cb, w_out, ln, w_router, b_router, alpha):
    n, d = h.shape
    n_tiles = n // CHUNK
    per = CHUNK // HALO
    row = lambda i: (i, 0)
    return pl.pallas_call(
        functools.partial(_conv_kernel, alpha=alpha, tiles_per_seq=lay.seq // CHUNK, ctx_tiles=lay.n_ctx // CHUNK),
        grid=(n_tiles,),
        in_specs=[
            pl.BlockSpec((CHUNK, d), row),
            pl.BlockSpec((HALO, d), lambda i: (jnp.maximum(i * per - 1, 0), 0)),
            pl.BlockSpec((HALO, d), lambda i: (jnp.minimum((i + 1) * per, n_tiles * per - 1), 0)),
            pl.BlockSpec((None, 6, d), lambda i: (lay.mod_row(i, CHUNK), 0, 0)),
            _const_spec((d, 3 * d)), _const_spec((3, d)), _const_spec((1, d)), _const_spec((d, d)),
            _const_spec((2, d)), _const_spec((d, ROUTER_W)), _const_spec((1, ROUTER_W)),
        ],
        out_specs=[pl.BlockSpec((CHUNK, d), row), pl.BlockSpec((CHUNK, d), row),
                   pl.BlockSpec((CHUNK, ROUTER_W), row)],
        out_shape=[jax.ShapeDtypeStruct((n, d), F32), jax.ShapeDtypeStruct((n, d), BF16),
                   jax.ShapeDtypeStruct((n, ROUTER_W), F32)],
        compiler_params=_params("parallel"),
        name="conv_layer",
    )(h, h, h, mod, w_in.astype(BF16), cw.astype(F32), cb.reshape(1, d).astype(F32), w_out.astype(BF16),
      ln, w_router, b_router)


def _route_kernel(lg_ref, tri_ref, upper_ref, ext_ref, lp_ref, lpt_ref, tab_ref, cnt_ref, base_ref):
    @pl.when(pl.program_id(0) == 0)
    def _():
        base_ref[...] = jnp.zeros_like(base_ref)

    x = lg_ref[...]
    lane = lax.broadcasted_iota(jnp.int32, x.shape, 1).astype(F32)

    def softmax(mask):
        m = jnp.max(jnp.where(mask, x, -jnp.inf), axis=-1, keepdims=True)
        e = jnp.where(mask, jnp.exp(x - m), 0.0)
        return e / jnp.sum(e, axis=-1, keepdims=True)

    def top1(prob, mask):
        p = jnp.max(jnp.where(mask, prob, -1.0), axis=-1, keepdims=True)
        i = jnp.min(jnp.where(mask & (prob == p), lane, float(ROUTER_W)), axis=-1, keepdims=True)
        return p, i

    g_mask = lane < MOE_GROUPS
    g_p, g_idx = top1(softmax(g_mask), g_mask)
    e_lo = MOE_GROUPS + MOE_EPG * g_idx
    e_mask = (lane >= e_lo) & (lane < e_lo + MOE_EPG)
    e_prob = softmax(e_mask)
    p1, i1 = top1(e_prob, e_mask)
    p2, i2 = top1(e_prob, e_mask & (lane != i1))
    denom = p1 + p2
    gate1 = g_p * p1 / denom
    gate2 = g_p * p2 / denom

    sel1 = lane == i1
    sel2 = lane == i2
    cnt = (sel1 | sel2).astype(F32)
    within = _dot(tri_ref[...], cnt.astype(BF16))
    cnt_tile = jnp.sum(cnt, axis=0, keepdims=True)
    ci = jnp.broadcast_to(cnt_tile, (8, ROUTER_W)).astype(jnp.int32)
    ci = (ci + (RUN_ALIGN - 1)) & ~(RUN_ALIGN - 1)
    run_tile = ci[0:1].astype(F32)
    start = (_dot((ci >> 4).astype(F32).astype(BF16), upper_ref[...]) * 16.0
             + _dot((ci & 15).astype(F32).astype(BF16), upper_ref[...]))[0:1]
    where_to = within + start
    lp1 = jnp.sum(jnp.where(sel1, where_to, 0.0), axis=-1, keepdims=True)
    lp2 = jnp.sum(jnp.where(sel2, where_to, 0.0), axis=-1, keepdims=True)

    before = base_ref[...]
    total = before + run_tile
    base_ref[...] = total
    cnt_ref[...] = jnp.broadcast_to(total, cnt_ref.shape)
    row = lax.broadcasted_iota(jnp.int32, tab_ref.shape, 0)
    tab_ref[...] = jnp.where(row == 0, run_tile, jnp.where(row == 1, before, jnp.where(row == 2, start, 0.0)))

    def pieces(g):
        hi = g.astype(BF16).astype(F32)
        mid = (g - hi).astype(BF16).astype(F32)
        return hi, mid, g - hi - mid

    a1, b1, c1 = pieces(gate1)
    a2, b2, c2 = pieces(gate2)
    cols = (a1, b1, c1, a2, b2, c2, i1 - MOE_GROUPS)
    ext = jnp.zeros(x.shape, F32)
    for j, col in enumerate(cols):
        ext = jnp.where(lane == j, col, ext)
    ext_ref[...] = ext.astype(BF16)
    lp = jnp.where(lane == 0, lp1, jnp.where(lane == 1, lp2, 0.0))
    lp_ref[...] = lp
    lpt_ref[...] = lp.T[0:8, :]


def _route(logits):
    n = logits.shape[0]
    n_tiles = n // TOK_TILE
    r = lax.broadcasted_iota(jnp.int32, (TOK_TILE, TOK_TILE), 0)
    c = lax.broadcasted_iota(jnp.int32, (TOK_TILE, TOK_TILE), 1)
    tri = (c < r).astype(BF16)
    r = lax.broadcasted_iota(jnp.int32, (ROUTER_W, ROUTER_W), 0)
    c = lax.broadcasted_iota(jnp.int32, (ROUTER_W, ROUTER_W), 1)
    upper = (r < c).astype(BF16)
    row = lambda i: (i, 0)
    return pl.pallas_call(
        _route_kernel,
        grid=(n_tiles,),
        in_specs=[pl.BlockSpec((TOK_TILE, ROUTER_W), row), _const_spec((TOK_TILE, TOK_TILE)),
                  _const_spec((ROUTER_W, ROUTER_W))],
        out_specs=[pl.BlockSpec((TOK_TILE, ROUTER_W), row), pl.BlockSpec((TOK_TILE, ROUTER_W), row),
                   pl.BlockSpec((8, TOK_TILE), row), pl.BlockSpec((8, ROUTER_W), row),
                   pl.BlockSpec((8, ROUTER_W), lambda i: (0, 0))],
        out_shape=[jax.ShapeDtypeStruct((n, ROUTER_W), BF16), jax.ShapeDtypeStruct((n, ROUTER_W), F32),
                   jax.ShapeDtypeStruct((n_tiles * 8, TOK_TILE), F32),
                   jax.ShapeDtypeStruct((n_tiles * 8, ROUTER_W), F32),
                   jax.ShapeDtypeStruct((8, ROUTER_W), F32)],
        scratch_shapes=[pltpu.VMEM((1, ROUTER_W), F32)],
        compiler_params=_params("arbitrary"),
        name="moe_route",
    )(logits, tri, upper)


def _plan(tables, counts, n_blocks):
    n_tiles = tables.shape[0] // 8
    tables = tables.reshape(n_tiles, 8, ROUTER_W)[:, :3, MOE_GROUPS:MOE_GROUPS + MOE_EXPERTS].astype(jnp.int32)
    counts = counts[0, MOE_GROUPS:MOE_GROUPS + MOE_EXPERTS].astype(jnp.int32)
    padded = (counts + EXP_BLOCK - 1) // EXP_BLOCK * EXP_BLOCK
    pad_ends = jnp.cumsum(padded)
    pad_starts = pad_ends - padded
    run_len = tables[:, 0].reshape(-1)
    run_src = tables[:, 2].reshape(-1)
    run_dst = (tables[:, 1] + pad_starts[None, :]).reshape(-1)
    first_row = jnp.arange(n_blocks, dtype=jnp.int32) * EXP_BLOCK
    block_e = jnp.minimum(jnp.sum(first_row[:, None] >= pad_ends[None, :], axis=1),
                          MOE_EXPERTS - 1).astype(jnp.int32)
    n_valid = (pad_ends[-1] // EXP_BLOCK).astype(jnp.int32).reshape(1)
    clear_row = jnp.concatenate([pad_starts + counts, pad_ends[-1:]])
    clear_len = jnp.concatenate([padded - counts, n_blocks - n_valid])
    return run_len, run_src, run_dst, clear_row, clear_len, block_e, n_valid


RUN_BITS = tuple(b for b in (1 << k for k in range(TOK_TILE.bit_length() - 1, -1, -1)) if b >= RUN_ALIGN)
RARE_BITS = 4
SORT_ROWS = MOE_TOPK * TOK_TILE + MOE_EXPERTS * RUN_ALIGN


def _for_each_piece(length, src, dst, fn):
    def pieces(bits):
        for bit in bits:
            above = length & ~(2 * bit - 1)

            @pl.when((length & bit) != 0)
            def _(bit=bit, above=above):
                fn(pl.multiple_of(src + above, RUN_ALIGN), pl.multiple_of(dst + above, RUN_ALIGN), bit)

    @pl.when(length >= RUN_BITS[RARE_BITS - 1])
    def _():
        pieces(RUN_BITS[:RARE_BITS])
    pieces(RUN_BITS[RARE_BITS:])


def _dispatch_kernel(len_ref, src_ref, dst_ref, zrow_ref, zlen_ref, t_ref, ext_ref, lpt_ref, xs_ref,
                     buf, zeros, sem, zsem, *, n_tiles):
    i = pl.program_id(0)
    half = t_ref.shape[1] // 2

    @pl.when(i == 0)
    def _():
        zeros[...] = jnp.zeros_like(zeros)

        def clear(e, c):
            _for_each_piece(zlen_ref[e], 0, zrow_ref[e], lambda s, d, rows: pltpu.make_async_copy(
                zeros.at[pl.ds(0, rows)], xs_ref.at[pl.ds(d, rows)], zsem).start())
            return c
        lax.fori_loop(0, MOE_EXPERTS, clear, 0)

        def drain(e, c):
            _for_each_piece(zlen_ref[e], 0, zrow_ref[e], lambda s, d, rows: pltpu.make_async_copy(
                zeros.at[pl.ds(0, rows)], xs_ref.at[pl.ds(d, rows)], zsem).wait())
            return c
        lax.fori_loop(0, MOE_EXPERTS, drain, 0)

        def tail_copy(bk):
            row = pl.multiple_of(zrow_ref[MOE_EXPERTS] + bk * EXP_BLOCK, EXP_BLOCK)
            return pltpu.make_async_copy(zeros.at[pl.ds(0, EXP_BLOCK)], xs_ref.at[pl.ds(row, EXP_BLOCK)], zsem)

        def clear_tail(bk, c):
            tail_copy(bk).start()
            return c
        lax.fori_loop(0, zlen_ref[MOE_EXPERTS], clear_tail, 0)

        def drain_tail(bk, c):
            tail_copy(bk).wait()
            return c
        lax.fori_loop(0, zlen_ref[MOE_EXPERTS], drain_tail, 0)

    slot = i % 2

    def scatter(tile, sl, start):
        def body(e, c):
            k = tile * MOE_EXPERTS + e

            def piece(s, d, rows):
                cp = pltpu.make_async_copy(buf.at[sl, pl.ds(s, rows)], xs_ref.at[pl.ds(d, rows)], sem.at[sl])
                cp.start() if start else cp.wait()
            _for_each_piece(len_ref[k], src_ref[k], dst_ref[k], piece)
            return c
        lax.fori_loop(0, MOE_EXPERTS, body, 0)

    @pl.when(i >= 2)
    def _():
        scatter(i - 2, slot, False)

    lp = lpt_ref[...].astype(jnp.int32)
    j = lax.broadcasted_iota(jnp.int32, (SORT_ROWS, TOK_TILE), 0)
    perm = ((j == lp[0:1, :]) | (j == lp[1:2, :])).astype(F32).astype(BF16)
    rhs = jnp.concatenate([t_ref[...], ext_ref[...]], axis=1)
    srt = _dot(perm, rhs)
    bits = lax.bitcast_convert_type(srt, jnp.uint32)
    buf[slot, :, :half] = (bits[:, :half] >> 16) | (bits[:, half:2 * half] & jnp.uint32(0xFFFF0000))
    buf[slot, :, half:] = bits[:, 2 * half:]
    scatter(i, slot, True)

    @pl.when(i == n_tiles - 1)
    def _():
        if n_tiles > 1:
            scatter(i - 1, 1 - slot, False)
        scatter(i, slot, False)


def _expert_kernel(be_ref, nv_ref, x_ref, w1_ref, w3_ref, w2_ref, o_ref, w1b, w3b, w2b):
    i = pl.program_id(0)
    valid = i < nv_ref[0]
    half = o_ref.shape[1] // 2

    @pl.when(valid & ((i == 0) | (be_ref[i] != be_ref[jnp.maximum(i - 1, 0)])))
    def _():
        w1b[...] = w1_ref[...].astype(BF16)
        w3b[...] = w3_ref[...].astype(BF16)
        w2b[...] = w2_ref[...].astype(BF16)

    @pl.when(valid)
    def _():
        packed = x_ref[:, :half]
        lo = lax.bitcast_convert_type(packed << 16, F32)
        hi = lax.bitcast_convert_type(packed & jnp.uint32(0xFFFF0000), F32)
        x = jnp.concatenate([lo, hi], axis=1).astype(BF16)
        ext = lax.bitcast_convert_type(x_ref[:, half:], F32)
        g1 = ext[:, 0:1] + ext[:, 1:2] + ext[:, 2:3]
        g2 = ext[:, 3:4] + ext[:, 4:5] + ext[:, 5:6]
        gate = jnp.where(ext[:, 6:7] == be_ref[i].astype(F32), g1, g2)
        hdn = _silu(_dot(x, w1b[...])) * _dot(x, w3b[...])
        o_ref[...] = _dot(hdn.astype(BF16), w2b[...]) * gate

    @pl.when(jnp.logical_not(valid))
    def _():
        o_ref[...] = jnp.zeros_like(o_ref)


def _combine_kernel(len_ref, src_ref, dst_ref, ys_ref, lp_ref, h_ref, mod_ref, ln_ref, o_ref, buf, sem,
                    *, alpha, first_tile, n_steps):
    step = pl.program_id(0)
    tile = step + first_tile
    slot = step % 2

    def gather(tl, sl, start):
        def body(e, c):
            k = tl * MOE_EXPERTS + e

            def piece(s, d, rows):
                cp = pltpu.make_async_copy(ys_ref.at[pl.ds(d, rows)], buf.at[sl, pl.ds(s, rows)], sem.at[sl])
                cp.start() if start else cp.wait()
            _for_each_piece(len_ref[k], src_ref[k], dst_ref[k], piece)
            return c
        lax.fori_loop(0, MOE_EXPERTS, body, 0)

    @pl.when(step == 0)
    def _():
        buf[...] = jnp.zeros_like(buf)
        gather(tile, slot, True)

    @pl.when(step + 1 < n_steps)
    def _():
        gather(tile + 1, 1 - slot, True)
    gather(tile, slot, False)

    lp = lp_ref[...].astype(jnp.int32)
    j = lax.broadcasted_iota(jnp.int32, (TOK_TILE, SORT_ROWS), 1)
    pick = ((j == lp[:, 0:1]) | (j == lp[:, 1:2])).astype(F32).astype(BF16)
    y = buf[slot]
    hi = y.astype(BF16)
    r1 = y - hi.astype(F32)
    mid = r1.astype(BF16)
    lo = (r1 - mid.astype(F32)).astype(BF16)
    f = _dot(pick, hi) + _dot(pick, mid) + _dot(pick, lo)
    o_ref[...] = _layer_norm(alpha * h_ref[...] + mod_ref[5:6, :] * f, ln_ref[0:1, :], ln_ref[1:2, :])


def _moe(lay, t, logits, h, mod, ln, w1, w3, w2, layer, alpha, first_tile=0):
    n, d = t.shape
    hid = w1.shape[3]
    n_tiles = n // TOK_TILE
    max_rows = n * MOE_TOPK + n_tiles * MOE_EXPERTS * (RUN_ALIGN - 1)
    n_blocks = -(-max_rows // EXP_BLOCK) + MOE_EXPERTS
    n_rows = n_blocks * EXP_BLOCK
    xs_w = d // 2 + ROUTER_W
    ext, lp, lpt, tables, counts = _route(logits)
    run_len, run_src, run_dst, zrow, zlen, block_e, n_valid = _plan(tables, counts, n_blocks)

    xs = pl.pallas_call(
        functools.partial(_dispatch_kernel, n_tiles=n_tiles),
        grid_spec=pltpu.PrefetchScalarGridSpec(
            num_scalar_prefetch=5,
            grid=(n_tiles,),
            in_specs=[pl.BlockSpec((TOK_TILE, d), lambda i, *_: (i, 0)),
                      pl.BlockSpec((TOK_TILE, ROUTER_W), lambda i, *_: (i, 0)),
                      pl.BlockSpec((8, TOK_TILE), lambda i, *_: (i, 0))],
            out_specs=pl.BlockSpec(memory_space=pl.ANY),
            scratch_shapes=[pltpu.VMEM((2, SORT_ROWS, xs_w), jnp.uint32),
                            pltpu.VMEM((max(TOK_TILE, EXP_BLOCK), xs_w), jnp.uint32),
                            pltpu.SemaphoreType.DMA((2,)), pltpu.SemaphoreType.DMA],
        ),
        out_shape=jax.ShapeDtypeStruct((n_rows, xs_w), jnp.uint32),
        compiler_params=_params("arbitrary"),
        name="moe_dispatch",
    )(run_len, run_src, run_dst, zrow, zlen, t, ext, lpt)

    blk = lambda i, be, nv: (jnp.minimum(i, nv[0] - 1), 0)
    ys = pl.pallas_call(
        _expert_kernel,
        grid_spec=pltpu.PrefetchScalarGridSpec(
            num_scalar_prefetch=2,
            grid=(n_blocks,),
            in_specs=[
                pl.BlockSpec((EXP_BLOCK, xs_w), blk),
                pl.BlockSpec((None, None, d, hid), lambda i, be, nv: (layer, be[i], 0, 0)),
                pl.BlockSpec((None, None, d, hid), lambda i, be, nv: (layer, be[i], 0, 0)),
                pl.BlockSpec((None, None, hid, d), lambda i, be, nv: (layer, be[i], 0, 0)),
            ],
            out_specs=pl.BlockSpec((EXP_BLOCK, d), lambda i, be, nv: (i, 0)),
            scratch_shapes=[pltpu.VMEM((d, hid), BF16), pltpu.VMEM((d, hid), BF16), pltpu.VMEM((hid, d), BF16)],
        ),
        out_shape=jax.ShapeDtypeStruct((n_rows, d), F32),
        compiler_params=_params("arbitrary"),
        name="moe_experts",
    )(block_e, n_valid, xs, w1, w3, w2)

    n_steps = n_tiles - first_tile
    return pl.pallas_call(
        functools.partial(_combine_kernel, alpha=alpha, first_tile=first_tile, n_steps=n_steps),
        grid_spec=pltpu.PrefetchScalarGridSpec(
            num_scalar_prefetch=3,
            grid=(n_steps,),
            in_specs=[pl.BlockSpec(memory_space=pl.ANY),
                      pl.BlockSpec((TOK_TILE, ROUTER_W), lambda i, *_: (i + first_tile, 0)),
                      pl.BlockSpec((TOK_TILE, d), lambda i, *_: (i + first_tile, 0)),
                      pl.BlockSpec((None, 6, d), lambda i, *_: (lay.mod_row(i + first_tile, TOK_TILE), 0, 0)),
                      pl.BlockSpec((2, d), lambda i, *_: (0, 0))],
            out_specs=pl.BlockSpec((TOK_TILE, d), lambda i, *_: (i, 0)),
            scratch_shapes=[pltpu.VMEM((2, SORT_ROWS, d), F32), pltpu.SemaphoreType.DMA((2,))],
        ),
        out_shape=jax.ShapeDtypeStruct((n_steps * TOK_TILE, d), F32),
        compiler_params=_params("arbitrary"),
        name="moe_combine",
    )(run_len, run_src, run_dst, ys, lp, h, mod, ln)


def kernel(x, c, ctx, c_ctx, ada_w, ada_b, ln_g, ln_b, ret_w_in, ret_decay, ret_gn_g, ret_w_out, mla_w_down, mla_q_norm, mla_kv_norm, mla_w_uq, mla_w_ukv, mla_w_out, conv_w_in, conv_w, conv_b, conv_w_out, moe_w_group, moe_b_group, moe_w_expert, moe_b_expert, moe_w1, moe_w3, moe_w2):
    b, s, d = x.shape
    depth = ada_w.shape[0]
    lay = _Layout(b, s, ctx.shape[1])
    alpha = (2.0 * depth) ** 0.25

    mod_rows = -(-(b + 1) // 8) * 8
    cc = jnp.concatenate([c, c_ctx[None, :], jnp.zeros((mod_rows - b - 1, d), F32)], axis=0)
    mod_all = _ada(cc, ada_w, ada_b).reshape(depth, mod_rows, 6, d)

    h = (ctx.reshape(lay.n_ctx, d), x.reshape(b * s, d))
    for i in range(depth):
        kind, j = i % N_MIXERS, i // N_MIXERS
        mod = mod_all[i]
        ln1 = jnp.stack([ln_g[i, 0], ln_b[i, 0]])
        ln2 = jnp.stack([ln_g[i, 1], ln_b[i, 1]])
        pad = jnp.zeros((d, ROUTER_W - MOE_GROUPS - MOE_EXPERTS), F32)
        w_router = jnp.concatenate([moe_w_group[i], moe_w_expert[i], pad], axis=1).astype(BF16)
        b_router = jnp.concatenate([moe_b_group[i], moe_b_expert[i], pad[0]]).reshape(1, ROUTER_W)
        if kind == 0:
            a = _retention(lay, h, mod, ret_w_in[j], ret_decay[j], ret_gn_g[j], d)
            h1, t, logits = _post(lay, a, h, mod, ret_w_out[j], ln1, w_router, b_router, alpha)
        elif kind == 1:
            a = _mla(lay, h, mod, mla_w_down[j], mla_q_norm[j], mla_kv_norm[j], mla_w_uq[j], mla_w_ukv[j], d)
            h1, t, logits = _post(lay, a, h, mod, mla_w_out[j], ln1, w_router, b_router, alpha)
        else:
            h1, t, logits = _conv_layer(lay, h, mod, conv_w_in[j], conv_w[j], conv_b[j], conv_w_out[j],
                                        ln1, w_router, b_router, alpha)
        first_tile = lay.n_ctx // TOK_TILE if i == depth - 1 else 0
        h = _moe(lay, t, logits, h1, mod, ln2, moe_w1, moe_w3, moe_w2, i, alpha, first_tile)
    return h.reshape(b, s, d)
```

```python
import functools

import jax
import jax.numpy as jnp
from jax import lax
from jax.experimental import pallas as pl
from jax.experimental.pallas import tpu as pltpu

F32 = jnp.float32
BF16 = jnp.bfloat16

GRID_W = 64
LN_EPS = 1e-5
RMS_EPS = 1e-6
ROPE_BASE = 10000.0
N_MIXERS = 3
RET_HEADS = 4
MLA_HEADS = 8
MLA_NOPE = 128
MLA_ROPE = 64
MLA_V = 128
MLA_QK = 256
MOE_GROUPS = 4
MOE_EPG = 8
MOE_EXPERTS = MOE_GROUPS * MOE_EPG
MOE_TOPK = 2
ROUTER_W = 128

CHUNK = 256
TM = 512
TQ = 512
TOK_TILE = 512
EXP_BLOCK = 512
RUN_ALIGN = 8
VMEM_LIMIT = 56 * 1024 * 1024


def _params(*sem):
    return pltpu.CompilerParams(dimension_semantics=sem, vmem_limit_bytes=VMEM_LIMIT)


def _const_spec(shape):
    nd = len(shape)
    return pl.BlockSpec(shape, lambda *_: (0,) * nd, pipeline_mode=pl.Buffered(1))


def _dot(a, b):
    return jnp.dot(a, b, preferred_element_type=F32)


def _dot_nt(a, b):
    return lax.dot_general(a, b, (((1,), (1,)), ((), ())), preferred_element_type=F32)


def _dot_tn(a, b):
    return lax.dot_general(a, b, (((0,), (0,)), ((), ())), preferred_element_type=F32)


def _silu(x):
    return x * jax.nn.sigmoid(x)


def _layer_norm(x, g, b):
    mu = jnp.mean(x, axis=-1, keepdims=True)
    xc = x - mu
    var = jnp.mean(xc * xc, axis=-1, keepdims=True)
    return xc * lax.rsqrt(var + LN_EPS) * g + b


def _rms_norm(x, g):
    return x * lax.rsqrt(jnp.mean(x * x, axis=-1, keepdims=True) + RMS_EPS) * g


def _ada_kernel(c_ref, w_ref, b_ref, o_ref):
    a = _silu(c_ref[...]).astype(BF16)
    o_ref[...] = _dot(a, w_ref[...].astype(BF16)) + b_ref[...]


def _ada(cc, ada_w, ada_b):
    depth, d, n6 = ada_w.shape
    rows = cc.shape[0]
    tn = 1536
    return pl.pallas_call(
        _ada_kernel,
        grid=(depth, n6 // tn),
        in_specs=[
            pl.BlockSpec((rows, d), lambda l, j: (0, 0)),
            pl.BlockSpec((None, d, tn), lambda l, j: (l, 0, j)),
            pl.BlockSpec((None, 1, tn), lambda l, j: (l, 0, j)),
        ],
        out_specs=pl.BlockSpec((None, rows, tn), lambda l, j: (l, 0, j)),
        out_shape=jax.ShapeDtypeStruct((depth, rows, n6), F32),
        compiler_params=_params("parallel", "parallel"),
        name="ada_mod",
    )(cc, ada_w, ada_b.reshape(depth, 1, n6))


class _Layout:
    def __init__(self, batch, seq, ctx_len):
        assert ctx_len == CHUNK and seq % TM == 0 and seq % CHUNK == 0
        self.batch, self.seq, self.ctx = batch, seq, ctx_len
        self.n_ctx = batch * ctx_len
        self.n = self.n_ctx + batch * seq
        assert self.n_ctx % TM == 0 and self.n_ctx % seq == 0 and self.n_ctx % TOK_TILE == 0
        assert self.n_ctx % TQ == 0 and seq % TQ == 0 and seq % TOK_TILE == 0

    def mod_row(self, tile, rows_per_tile):
        ctx_tiles = self.n_ctx // rows_per_tile
        per_batch = self.seq // rows_per_tile
        return jnp.where(tile < ctx_tiles, self.batch, (tile - ctx_tiles) // per_batch)

    def pos_block(self, tile, rows_per_tile):
        ctx_tiles = self.n_ctx // rows_per_tile
        per_batch = self.seq // rows_per_tile
        return jnp.where(tile < ctx_tiles, 0, 1 + (tile - ctx_tiles) % per_batch)


def _row_specs(x, rows, ctx_tiles):
    if not isinstance(x, tuple):
        return (x,), [pl.BlockSpec((rows, x.shape[1]), lambda i: (i, 0))]
    width = x[0].shape[1]
    return x, [pl.BlockSpec((rows, width), lambda i: (jnp.minimum(i, ctx_tiles - 1), 0)),
               pl.BlockSpec((rows, width), lambda i: (jnp.maximum(i - ctx_tiles, 0), 0))]


def _row_tile(refs, ctx_tiles):
    if len(refs) == 1:
        return refs[0][...]
    return jnp.where(pl.program_id(0) < ctx_tiles, refs[0][...], refs[1][...])


def _rope_angles(seq, dim):
    n_rows = seq // GRID_W
    rows = jnp.repeat(jnp.arange(n_rows, dtype=F32), GRID_W)
    cols = jnp.tile(jnp.arange(GRID_W, dtype=F32), n_rows)
    quarter = dim // 4
    inv_freq = ROPE_BASE ** (-jnp.arange(quarter, dtype=F32) / quarter)
    return jnp.concatenate([rows[:, None] * inv_freq, cols[:, None] * inv_freq], axis=-1)


def _with_identity_block(table, fill, rows):
    ident = jnp.full((rows, table.shape[1]), fill, F32)
    return jnp.concatenate([ident, table], axis=0)


def _proj_ret_kernel(*refs, dk, ctx_tiles):
    h_refs = refs[:-14]
    (mod_ref, cos_ref, sin_ref, kdec_ref, wq_ref, wk_ref, wv_ref, wg_ref,
     q_ref, k_ref, kf_ref, kb_ref, v_ref, g_ref) = refs[-14:]
    u = (_row_tile(h_refs, ctx_tiles) * (1.0 + mod_ref[1:2, :]) + mod_ref[0:1, :]).astype(BF16)
    cos = cos_ref[...]
    sin = sin_ref[...]
    half = dk // 2
    yq = _dot(u, wq_ref[...])
    for hd in range(RET_HEADS):
        x1 = yq[:, hd * dk:hd * dk + half]
        x2 = yq[:, hd * dk + half:(hd + 1) * dk]
        q_ref[:, hd * dk:hd * dk + half] = (x1 * cos - x2 * sin).astype(BF16)
        q_ref[:, hd * dk + half:(hd + 1) * dk] = (x1 * sin + x2 * cos).astype(BF16)
    yk = _dot(u, wk_ref[...])
    k_scale = dk ** -0.5
    for hd in range(RET_HEADS):
        x1 = yk[:, hd * dk:hd * dk + half]
        x2 = yk[:, hd * dk + half:(hd + 1) * dk]
        df = kdec_ref[:, hd:hd + 1]
        db = kdec_ref[:, RET_HEADS + hd:RET_HEADS + hd + 1]
        for part, o in ((0, (x1 * cos - x2 * sin) * k_scale), (1, (x1 * sin + x2 * cos) * k_scale)):
            sl = slice(hd * dk + part * half, hd * dk + (part + 1) * half)
            k_ref[:, sl] = o.astype(BF16)
            kf_ref[:, sl] = (o * df).astype(BF16)
            kb_ref[:, sl] = (o * db).astype(BF16)
    v_ref[...] = _dot(u, wv_ref[...]).astype(BF16)
    g_ref[...] = _silu(_dot(u, wg_ref[...]))


def _ret_state_kernel(cdec_ref, kd_ref, v_ref, s_ref, state_ref, *, dk, dv):
    @pl.when(pl.program_id(1) == 0)
    def _():
        state_ref[...] = jnp.zeros_like(state_ref)

    for hd in range(RET_HEADS):
        st = state_ref[hd]
        s_ref[hd] = st.astype(BF16)
        kd = kd_ref[:, hd * dk:(hd + 1) * dk]
        v = v_ref[:, hd * dv:(hd + 1) * dv]
        state_ref[hd] = st * cdec_ref[RET_HEADS + hd] + _dot_tn(kd, v)


def _ret_out_kernel(cdec_ref, q_ref, k_ref, kd_ref, v_ref, sb_ref, mask_ref, qdec_ref, g_ref, gn_ref, o_ref,
                    state_ref, *, dk, dv):
    @pl.when(pl.program_id(1) == 0)
    def _():
        state_ref[...] = jnp.zeros_like(state_ref)

    for hd in range(RET_HEADS):
        q = q_ref[:, hd * dk:(hd + 1) * dk]
        k = k_ref[:, hd * dk:(hd + 1) * dk]
        kd = kd_ref[:, hd * dk:(hd + 1) * dk]
        v = v_ref[:, hd * dv:(hd + 1) * dv]
        p = (_dot_nt(q, k) * mask_ref[hd]).astype(BF16)
        st = state_ref[hd]
        o = (_dot(p, v) + qdec_ref[:, hd:hd + 1] * _dot(q, st.astype(BF16))
             + qdec_ref[:, RET_HEADS + hd:RET_HEADS + hd + 1] * _dot(q, sb_ref[hd]))
        state_ref[hd] = st * cdec_ref[hd] + _dot_tn(kd, v)
        mu = jnp.mean(o, axis=-1, keepdims=True)
        oc = o - mu
        var = jnp.mean(oc * oc, axis=-1, keepdims=True)
        on = oc * lax.rsqrt(var + LN_EPS) * gn_ref[:, hd * dv:(hd + 1) * dv]
        o_ref[:, hd * dv:(hd + 1) * dv] = (g_ref[:, hd * dv:(hd + 1) * dv] * on).astype(BF16)


def _retention(lay, h, mod, w_in, decay_logit, gn_g, d_model):
    n, b, nc = lay.n, lay.batch, lay.seq // CHUNK
    dk = d_model // RET_HEADS
    dv = 2 * dk
    qk, vw = RET_HEADS * dk, RET_HEADS * dv
    w_in = w_in.astype(BF16)
    wq, wk, wv, wg = w_in[:, :qk], w_in[:, qk:2 * qk], w_in[:, 2 * qk:2 * qk + vw], w_in[:, 2 * qk + vw:]

    ang = _rope_angles(lay.seq, dk)
    cos_t = _with_identity_block(jnp.cos(ang), 1.0, TM)
    sin_t = _with_identity_block(jnp.sin(ang), 0.0, TM)

    lg = jax.nn.log_sigmoid(decay_logit.astype(F32))
    idx = jnp.arange(CHUNK, dtype=F32)
    k_pow = jnp.stack([CHUNK - 1.0 - idx, idx])
    q_pow = jnp.stack([idx + 1.0, CHUNK - idx])
    kdec = jnp.exp(k_pow[:, :, None] * lg[:, None, :])
    qdec = jnp.exp(q_pow[:, :, None] * lg[:, None, :])
    kdec = jnp.moveaxis(kdec, 0, 1).reshape(CHUNK, 2 * RET_HEADS)
    qdec = jnp.moveaxis(qdec, 0, 1).reshape(CHUNK, 2 * RET_HEADS)
    cdec = jnp.exp(CHUNK * lg).reshape(2 * RET_HEADS)
    rel = idx[:, None] - idx[None, :]
    rel = jnp.stack([rel, -rel])
    intra = jnp.where(rel[:, None] >= 0, jnp.exp(jnp.maximum(rel[:, None], 0.0) * lg[:, :, None, None]), 0.0)
    kdec_tm = jnp.tile(kdec, (TM // CHUNK, 1))

    n_tiles = n // TM
    row = lambda i: (i, 0)
    ctx_tiles = lay.n_ctx // TM
    h, h_specs = _row_specs(h, TM, ctx_tiles)
    q, k, kf, kb, v, g = pl.pallas_call(
        functools.partial(_proj_ret_kernel, dk=dk, ctx_tiles=ctx_tiles),
        grid=(n_tiles,),
        in_specs=h_specs + [
            pl.BlockSpec((None, 6, d_model), lambda i: (lay.mod_row(i, TM), 0, 0)),
            pl.BlockSpec((TM, dk // 2), lambda i: (lay.pos_block(i, TM), 0)),
            pl.BlockSpec((TM, dk // 2), lambda i: (lay.pos_block(i, TM), 0)),
            _const_spec((TM, 2 * RET_HEADS)),
            _const_spec((d_model, qk)), _const_spec((d_model, qk)),
            _const_spec((d_model, vw)), _const_spec((d_model, vw)),
        ],
        out_specs=[pl.BlockSpec((TM, qk), row)] * 4 + [pl.BlockSpec((TM, vw), row)] * 2,
        out_shape=[jax.ShapeDtypeStruct((n, qk), BF16)] * 4
        + [jax.ShapeDtypeStruct((n, vw), BF16), jax.ShapeDtypeStruct((n, vw), F32)],
        compiler_params=_params("parallel"),
        name="ret_proj",
    )(*h, mod, cos_t, sin_t, kdec_tm, wq, wk, wv, wg)

    def chunk_fwd(bi, c):
        return (jnp.where(c == 0, bi, b + bi * nc + c - 1), 0)

    def chunk_bwd(bi, c):
        return (jnp.where(c == 0, bi, b + bi * nc + nc - c), 0)

    n_chunks = n // CHUNK
    s_b = pl.pallas_call(
        functools.partial(_ret_state_kernel, dk=dk, dv=dv),
        grid=(b, nc + 1),
        in_specs=[pl.BlockSpec(memory_space=pltpu.SMEM),
                  pl.BlockSpec((CHUNK, qk), chunk_bwd), pl.BlockSpec((CHUNK, vw), chunk_bwd)],
        out_specs=pl.BlockSpec((None, RET_HEADS, dk, dv), lambda bi, c: (chunk_bwd(bi, c)[0], 0, 0, 0)),
        out_shape=jax.ShapeDtypeStruct((n_chunks, RET_HEADS, dk, dv), BF16),
        scratch_shapes=[pltpu.VMEM((RET_HEADS, dk, dv), F32)],
        compiler_params=_params("parallel", "arbitrary"),
        name="ret_state_bwd",
    )(cdec, kb, v)

    return pl.pallas_call(
        functools.partial(_ret_out_kernel, dk=dk, dv=dv),
        grid=(b, nc + 1),
        in_specs=[
            pl.BlockSpec(memory_space=pltpu.SMEM),
            pl.BlockSpec((CHUNK, qk), chunk_fwd), pl.BlockSpec((CHUNK, qk), chunk_fwd),
            pl.BlockSpec((CHUNK, qk), chunk_fwd), pl.BlockSpec((CHUNK, vw), chunk_fwd),
            pl.BlockSpec((None, RET_HEADS, dk, dv), lambda bi, c: (chunk_fwd(bi, c)[0], 0, 0, 0)),
            _const_spec((RET_HEADS, CHUNK, CHUNK)), _const_spec((CHUNK, 2 * RET_HEADS)),
            pl.BlockSpec((CHUNK, vw), chunk_fwd), _const_spec((1, vw)),
        ],
        out_specs=pl.BlockSpec((CHUNK, vw), chunk_fwd),
        out_shape=jax.ShapeDtypeStruct((n, vw), BF16),
        scratch_shapes=[pltpu.VMEM((RET_HEADS, dk, dv), F32)],
        compiler_params=_params("parallel", "arbitrary"),
        name="ret_scan_out",
    )(cdec, q, k, kf, v, s_b, intra[0] + intra[1], qdec, g, gn_g.reshape(1, vw).astype(F32))


def _proj_mla_kernel(h_ref, mod_ref, ct_ref, st_ref, wdq_ref, wdkv_ref, wkr_ref, qn_ref, kvn_ref,
                     wuq_ref, wukn_ref, wuv_ref, q_ref, k_ref, vt_ref):
    u = (h_ref[...] * (1.0 + mod_ref[1:2, :]) + mod_ref[0:1, :]).astype(BF16)
    ct = ct_ref[...]
    st = st_ref[...]
    half = MLA_QK // 2

    def rope(x):
        return x * ct + pltpu.roll(x, half // 2, 1) * st

    cq = _rms_norm(_dot(u, wdq_ref[...]), qn_ref[...]).astype(BF16)
    yq = _dot(cq, wuq_ref[...])
    ckv = _rms_norm(_dot(u, wdkv_ref[...]), kvn_ref[...]).astype(BF16)
    kn = _dot(ckv, wukn_ref[...])
    kr = rope(_dot(u, wkr_ref[...])).astype(BF16)
    for hd in range(MLA_HEADS):
        q_ref[:, hd * MLA_QK:hd * MLA_QK + half] = yq[:, hd * MLA_QK:hd * MLA_QK + half].astype(BF16)
        q_ref[:, hd * MLA_QK + half:(hd + 1) * MLA_QK] = rope(
            yq[:, hd * MLA_QK + half:(hd + 1) * MLA_QK]).astype(BF16)
        k_ref[:, hd * MLA_QK:hd * MLA_QK + half] = kn[:, hd * half:(hd + 1) * half].astype(BF16)
        k_ref[:, hd * MLA_QK + half:(hd + 1) * MLA_QK] = kr
    vt_ref[...] = _dot(ckv, wuv_ref[...]).T.astype(BF16)


KV_CHUNK = 512
LOG2_E = 1.4426950408889634


def _kv_chunks(kv_refs, n_kv):
    chunks = []
    row = 0
    for j in range(n_kv):
        keys = kv_refs[2 * j].shape[0]
        step = min(keys, KV_CHUNK)
        for off in range(0, keys, step):
            chunks.append((j, off, row, step))
            row += step
    return chunks


def _attn_kernel(q_ref, *rest, scale, n_kv, pipelined):
    kv_refs = rest[:2 * n_kv]
    chunks = _kv_chunks(kv_refs, n_kv)

    def scores(c, s_ref, m):
        j, off, row, w = c
        s = _dot_nt(kv_refs[2 * j][off:off + w, :], q_ref[...])
        s_ref[row:row + w, :] = s
        cm = jnp.max(s, axis=0, keepdims=True)
        return cm if m is None else jnp.maximum(m, cm)

    def values(c, s_ref, m, acc, den):
        j, off, row, w = c
        p = jnp.exp2((s_ref[row:row + w, :] - m) * (scale * LOG2_E))
        cs = jnp.sum(p, axis=0, keepdims=True)
        pv = _dot(kv_refs[2 * j + 1][:, off:off + w], p.astype(BF16))
        return (pv if acc is None else acc + pv), (cs if den is None else den + cs)

    if not pipelined:
        o_ref, s_ref = rest[-2:]
        m = acc = den = None
        for c in chunks:
            m = scores(c, s_ref, m)
        for c in chunks:
            acc, den = values(c, s_ref, m, acc, den)
        o_ref[...] = (acc / den).T.astype(BF16)
        return

    o_ref, s0_ref, s1_ref, m0_ref, m1_ref = rest[-5:]
    t = pl.program_id(2)
    last = pl.num_programs(2) - 1
    bufs = ((s0_ref, m0_ref), (s1_ref, m1_ref))

    def step(new, old):
        m_prev = old[1][...] if old else None
        m = acc = den = None
        for c in chunks:
            if new:
                m = scores(c, new[0], m)
            if old:
                acc, den = values(c, old[0], m_prev, acc, den)
        if new:
            new[1][...] = m
        if old:
            o_ref[...] = (acc / den).T.astype(BF16)

    @pl.when(t == 0)
    def _():
        step(bufs[0], None)

    for parity in (0, 1):
        @pl.when((t > 0) & (t < last) & (t % 2 == parity))
        def _(parity=parity):
            step(bufs[parity], bufs[1 - parity])

        @pl.when((t == last) & (t % 2 == parity))
        def _(parity=parity):
            step(None, bufs[1 - parity])


def _pad_rope_cols(w):
    z = jnp.zeros((w.shape[0], MLA_ROPE // 2), w.dtype)
    return jnp.concatenate([w[:, :MLA_ROPE // 2], z, w[:, MLA_ROPE // 2:], z], axis=1)


def _mla(lay, h, mod, w_down, q_norm, kv_norm, w_uq, w_ukv, d_model):
    n, b, s = lay.n, lay.batch, lay.seq
    q_lora, kv_lora = q_norm.shape[0], kv_norm.shape[0]
    w_dq = w_down[:, :q_lora].astype(BF16)
    w_dkv = w_down[:, q_lora:q_lora + kv_lora].astype(BF16)
    w_kr = _pad_rope_cols(w_down[:, q_lora + kv_lora:]).astype(BF16)
    w_uq = w_uq.reshape(q_lora, MLA_HEADS, MLA_NOPE + MLA_ROPE)
    w_uq = jnp.concatenate(
        [w_uq[:, :, :MLA_NOPE], jax.vmap(_pad_rope_cols, 1, 1)(w_uq[:, :, MLA_NOPE:])], axis=2)
    w_uq = w_uq.reshape(q_lora, MLA_HEADS * MLA_QK).astype(BF16)
    w_ukv = w_ukv.reshape(kv_lora, MLA_HEADS, MLA_NOPE + MLA_V)
    w_ukn = w_ukv[:, :, :MLA_NOPE].reshape(kv_lora, MLA_HEADS * MLA_NOPE).astype(BF16)
    w_uv = w_ukv[:, :, MLA_NOPE:].reshape(kv_lora, MLA_HEADS * MLA_V).astype(BF16)

    ang = _rope_angles(s, MLA_ROPE)
    z = jnp.zeros_like(ang)
    ct = _with_identity_block(jnp.concatenate([jnp.cos(ang), z, jnp.cos(ang), z], axis=1), 1.0, TM)
    st = _with_identity_block(jnp.concatenate([-jnp.sin(ang), z, jnp.sin(ang), z], axis=1), 0.0, TM)

    row = lambda i: (i, 0)
    qw, vw = MLA_HEADS * MLA_QK, MLA_HEADS * MLA_V
    q, k, vt = pl.pallas_call(
        _proj_mla_kernel,
        grid=(n // TM,),
        in_specs=[
            pl.BlockSpec((TM, d_model), row),
            pl.BlockSpec((None, 6, d_model), lambda i: (lay.mod_row(i, TM), 0, 0)),
            pl.BlockSpec((TM, MLA_QK // 2), lambda i: (lay.pos_block(i, TM), 0)),
            pl.BlockSpec((TM, MLA_QK // 2), lambda i: (lay.pos_block(i, TM), 0)),
            _const_spec((d_model, q_lora)), _const_spec((d_model, kv_lora)), _const_spec((d_model, MLA_QK // 2)),
            _const_spec((1, q_lora)), _const_spec((1, kv_lora)),
            _const_spec((q_lora, qw)), _const_spec((kv_lora, vw)), _const_spec((kv_lora, vw)),
        ],
        out_specs=[pl.BlockSpec((TM, qw), row), pl.BlockSpec((TM, qw), row),
                   pl.BlockSpec((vw, TM), lambda i: (0, i))],
        out_shape=[jax.ShapeDtypeStruct((n, qw), BF16), jax.ShapeDtypeStruct((n, qw), BF16),
                   jax.ShapeDtypeStruct((vw, n), BF16)],
        compiler_params=_params("parallel"),
        name="mla_proj",
    )(h, mod, ct, st, w_dq, w_dkv, w_kr, q_norm.reshape(1, -1).astype(F32), kv_norm.reshape(1, -1).astype(F32),
      w_uq, w_ukn, w_uv)

    scale = (MLA_NOPE + MLA_ROPE) ** -0.5
    ctx_q = lambda bi, hd: (bi, hd)
    att_ctx = pl.pallas_call(
        functools.partial(_attn_kernel, scale=scale, n_kv=1, pipelined=False),
        grid=(b, MLA_HEADS),
        in_specs=[pl.BlockSpec((CHUNK, MLA_QK), ctx_q), pl.BlockSpec((CHUNK, MLA_QK), ctx_q),
                  pl.BlockSpec((MLA_V, CHUNK), lambda bi, hd: (hd, bi))],
        out_specs=pl.BlockSpec((CHUNK, MLA_V), ctx_q),
        out_shape=jax.ShapeDtypeStruct((lay.n_ctx, vw), BF16),
        scratch_shapes=[pltpu.VMEM((CHUNK, CHUNK), F32)],
        compiler_params=_params("parallel", "parallel"),
        name="mla_attn_ctx",
    )(q, k, vt)

    lat_blk = lay.n_ctx // s
    q_tiles = s // TQ
    lat_q = lambda bi, hd, t: (lay.n_ctx // TQ + bi * q_tiles + jnp.minimum(t, q_tiles - 1), hd)
    ctx_kv = lambda bi, hd, t: (bi, hd)
    lat_kv = lambda bi, hd, t: (lat_blk + bi, hd)
    att_lat = pl.pallas_call(
        functools.partial(_attn_kernel, scale=scale, n_kv=2, pipelined=True),
        grid=(b, MLA_HEADS, q_tiles + 1),
        in_specs=[pl.BlockSpec((TQ, MLA_QK), lat_q),
                  pl.BlockSpec((CHUNK, MLA_QK), ctx_kv),
                  pl.BlockSpec((MLA_V, CHUNK), lambda bi, hd, t: (hd, bi)),
                  pl.BlockSpec((s, MLA_QK), lat_kv),
                  pl.BlockSpec((MLA_V, s), lambda bi, hd, t: (hd, lat_blk + bi))],
        out_specs=pl.BlockSpec((TQ, MLA_V), lambda bi, hd, t: (bi * q_tiles + jnp.maximum(t - 1, 0), hd)),
        out_shape=jax.ShapeDtypeStruct((n - lay.n_ctx, vw), BF16),
        scratch_shapes=[pltpu.VMEM((CHUNK + s, TQ), F32)] * 2 + [pltpu.VMEM((1, TQ), F32)] * 2,
        compiler_params=_params("parallel", "parallel", "arbitrary"),
        name="mla_attn_lat",
    )(q, k, vt, k, vt)
    return att_ctx, att_lat


def _post_tail(y, h, mod_ref, ln_ref, wr_ref, br_ref, h_out, t_out, lg_out, alpha):
    h1 = _layer_norm(alpha * h + mod_ref[2:3, :] * y, ln_ref[0:1, :], ln_ref[1:2, :])
    t = h1 * (1.0 + mod_ref[4:5, :]) + mod_ref[3:4, :]
    t = t.astype(BF16)
    h_out[...] = h1
    t_out[...] = t
    lg_out[...] = _dot(t, wr_ref[...]) + br_ref[...]


def _post_kernel(*refs, alpha, ctx_tiles, n_a):
    a_refs, h_refs = refs[:n_a], refs[n_a:-8]
    mod_ref, w_ref, ln_ref, wr_ref, br_ref, h_out, t_out, lg_out = refs[-8:]
    y = _dot(_row_tile(a_refs, ctx_tiles), w_ref[...])
    _post_tail(y, _row_tile(h_refs, ctx_tiles), mod_ref, ln_ref, wr_ref, br_ref, h_out, t_out, lg_out, alpha)


def _post(lay, a, h, mod, w_out, ln, w_router, b_router, alpha):
    n, d = lay.n, w_out.shape[1]
    row = lambda i: (i, 0)
    ctx_tiles = lay.n_ctx // TM
    a, a_specs = _row_specs(a, TM, ctx_tiles)
    h, h_specs = _row_specs(h, TM, ctx_tiles)
    ka = a[0].shape[1]
    return pl.pallas_call(
        functools.partial(_post_kernel, alpha=alpha, ctx_tiles=ctx_tiles, n_a=len(a)),
        grid=(n // TM,),
        in_specs=a_specs + h_specs + [
            pl.BlockSpec((None, 6, d), lambda i: (lay.mod_row(i, TM), 0, 0)),
            _const_spec((ka, d)), _const_spec((2, d)), _const_spec((d, ROUTER_W)), _const_spec((1, ROUTER_W)),
        ],
        out_specs=[pl.BlockSpec((TM, d), row), pl.BlockSpec((TM, d), row), pl.BlockSpec((TM, ROUTER_W), row)],
        out_shape=[jax.ShapeDtypeStruct((n, d), F32), jax.ShapeDtypeStruct((n, d), BF16),
                   jax.ShapeDtypeStruct((n, ROUTER_W), F32)],
        compiler_params=_params("parallel"),
        name="post_mixer",
    )(*a, *h, mod, w_out.astype(BF16), ln, w_router, b_router)


HALO = 8


def _conv_kernel(h_ref, hp_ref, hn_ref, mod_ref, win_ref, cw_ref, cb_ref, w_ref, ln_ref, wr_ref, br_ref,
                 h_out, t_out, lg_out, *, alpha, tiles_per_seq, ctx_tiles):
    i = pl.program_id(0)
    d = h_ref.shape[1]
    h = h_ref[...]
    hx = jnp.concatenate([hp_ref[...], h, hn_ref[...]], axis=0)
    u = (hx * (1.0 + mod_ref[1:2, :]) + mod_ref[0:1, :]).astype(BF16)
    y = _dot(u, win_ref[...])
    gate_b = y[HALO:HALO + CHUNK, :d]
    zx = y[:, d:2 * d] * y[:, 2 * d:]
    z = zx[HALO:HALO + CHUNK]
    pos = jnp.where(i < ctx_tiles, 0, (i - ctx_tiles) % tiles_per_seq)
    last = jnp.where(i < ctx_tiles, 0, tiles_per_seq - 1)
    z_before = jnp.where(pos == 0, 0.0, zx[HALO - 1:HALO])
    z_after = jnp.where(pos == last, 0.0, zx[HALO + CHUNK:HALO + CHUNK + 1])
    r = lax.broadcasted_iota(jnp.int32, (CHUNK, 1), 0)
    z_prev = jnp.where(r == 0, z_before, pltpu.roll(z, 1, 0))
    z_next = jnp.where(r == CHUNK - 1, z_after, pltpu.roll(z, CHUNK - 1, 0))
    conv = cw_ref[0:1, :] * z_prev + cw_ref[1:2, :] * z + cw_ref[2:3, :] * z_next + cb_ref[...]
    yo = _dot((gate_b * conv).astype(BF16), w_ref[...])
    _post_tail(yo, h, mod_ref, ln_ref, wr_ref, br_ref, h_out, t_out, lg_out, alpha)


def _conv_layer(lay, h, mod, w_in, cw, cb, w_out, ln, w_router, b_router, alpha):
    n, d = h.shape
    n_tiles = n // CHUNK
    per = CHUNK // HALO
    row = lambda i: (i, 0)
    return pl.pallas_call(
        functools.partial(_conv_kernel, alpha=alpha, tiles_per_seq=lay.seq // CHUNK, ctx_tiles=lay.n_ctx // CHUNK),
        grid=(n_tiles,),
        in_specs=[
            pl.BlockSpec((CHUNK, d), row),
            pl.BlockSpec((HALO, d), lambda i: (jnp.maximum(i * per - 1, 0), 0)),
            pl.BlockSpec((HALO, d), lambda i: (jnp.minimum((i + 1) * per, n_tiles * per - 1), 0)),
            pl.BlockSpec((None, 6, d), lambda i: (lay.mod_row(i, CHUNK), 0, 0)),
            _const_spec((d, 3 * d)), _const_spec((3, d)), _const_spec((1, d)), _const_spec((d, d)),
            _const_spec((2, d)), _const_spec((d, ROUTER_W)), _const_spec((1, ROUTER_W)),
        ],
        out_specs=[pl.BlockSpec((CHUNK, d), row), pl.BlockSpec((CHUNK, d), row),
                   pl.BlockSpec((CHUNK, ROUTER_W), row)],
        out_shape=[jax.ShapeDtypeStruct((n, d), F32), jax.ShapeDtypeStruct((n, d), BF16),
                   jax.ShapeDtypeStruct((n, ROUTER_W), F32)],
        compiler_params=_params("parallel"),
        name="conv_layer",
    )(h, h, h, mod, w_in.astype(BF16), cw.astype(F32), cb.reshape(1, d).astype(F32), w_out.astype(BF16),
      ln, w_router, b_router)


def _route_kernel(lg_ref, tri_ref, upper_ref, ext_ref, lp_ref, lpt_ref, tab_ref, cnt_ref, base_ref):
    @pl.when(pl.program_id(0) == 0)
    def _():
        base_ref[...] = jnp.zeros_like(base_ref)

    x = lg_ref[...]
    lane = lax.broadcasted_iota(jnp.int32, x.shape, 1).astype(F32)

    def softmax(mask):
        m = jnp.max(jnp.where(mask, x, -jnp.inf), axis=-1, keepdims=True)
        e = jnp.where(mask, jnp.exp(x - m), 0.0)
        return e / jnp.sum(e, axis=-1, keepdims=True)

    def top1(prob, mask):
        p = jnp.max(jnp.where(mask, prob, -1.0), axis=-1, keepdims=True)
        i = jnp.min(jnp.where(mask & (prob == p), lane, float(ROUTER_W)), axis=-1, keepdims=True)
        return p, i

    g_mask = lane < MOE_GROUPS
    g_p, g_idx = top1(softmax(g_mask), g_mask)
    e_lo = MOE_GROUPS + MOE_EPG * g_idx
    e_mask = (lane >= e_lo) & (lane < e_lo + MOE_EPG)
    e_prob = softmax(e_mask)
    p1, i1 = top1(e_prob, e_mask)
    p2, i2 = top1(e_prob, e_mask & (lane != i1))
    denom = p1 + p2
    gate1 = g_p * p1 / denom
    gate2 = g_p * p2 / denom

    sel1 = lane == i1
    sel2 = lane == i2
    cnt = (sel1 | sel2).astype(F32)
    within = _dot(tri_ref[...], cnt.astype(BF16))
    cnt_tile = jnp.sum(cnt, axis=0, keepdims=True)
    ci = jnp.broadcast_to(cnt_tile, (8, ROUTER_W)).astype(jnp.int32)
    ci = (ci + (RUN_ALIGN - 1)) & ~(RUN_ALIGN - 1)
    run_tile = ci[0:1].astype(F32)
    start = (_dot((ci >> 4).astype(F32).astype(BF16), upper_ref[...]) * 16.0
             + _dot((ci & 15).astype(F32).astype(BF16), upper_ref[...]))[0:1]
    where_to = within + start
    lp1 = jnp.sum(jnp.where(sel1, where_to, 0.0), axis=-1, keepdims=True)
    lp2 = jnp.sum(jnp.where(sel2, where_to, 0.0), axis=-1, keepdims=True)

    before = base_ref[...]
    total = before + run_tile
    base_ref[...] = total
    cnt_ref[...] = jnp.broadcast_to(total, cnt_ref.shape)
    row = lax.broadcasted_iota(jnp.int32, tab_ref.shape, 0)
    tab_ref[...] = jnp.where(row == 0, run_tile, jnp.where(row == 1, before, jnp.where(row == 2, start, 0.0)))

    def pieces(g):
        hi = g.astype(BF16).astype(F32)
        mid = (g - hi).astype(BF16).astype(F32)
        return hi, mid, g - hi - mid

    a1, b1, c1 = pieces(gate1)
    a2, b2, c2 = pieces(gate2)
    cols = (a1, b1, c1, a2, b2, c2, i1 - MOE_GROUPS)
    ext = jnp.zeros(x.shape, F32)
    for j, col in enumerate(cols):
        ext = jnp.where(lane == j, col, ext)
    ext_ref[...] = ext.astype(BF16)
    lp = jnp.where(lane == 0, lp1, jnp.where(lane == 1, lp2, 0.0))
    lp_ref[...] = lp
    lpt_ref[...] = lp.T[0:8, :]


def _route(logits):
    n = logits.shape[0]
    n_tiles = n // TOK_TILE
    r = lax.broadcasted_iota(jnp.int32, (TOK_TILE, TOK_TILE), 0)
    c = lax.broadcasted_iota(jnp.int32, (TOK_TILE, TOK_TILE), 1)
    tri = (c < r).astype(BF16)
    r = lax.broadcasted_iota(jnp.int32, (ROUTER_W, ROUTER_W), 0)
    c = lax.broadcasted_iota(jnp.int32, (ROUTER_W, ROUTER_W), 1)
    upper = (r < c).astype(BF16)
    row = lambda i: (i, 0)
    return pl.pallas_call(
        _route_kernel,
        grid=(n_tiles,),
        in_specs=[pl.BlockSpec((TOK_TILE, ROUTER_W), row), _const_spec((TOK_TILE, TOK_TILE)),
                  _const_spec((ROUTER_W, ROUTER_W))],
        out_specs=[pl.BlockSpec((TOK_TILE, ROUTER_W), row), pl.BlockSpec((TOK_TILE, ROUTER_W), row),
                   pl.BlockSpec((8, TOK_TILE), row), pl.BlockSpec((8, ROUTER_W), row),
                   pl.BlockSpec((8, ROUTER_W), lambda i: (0, 0))],
        out_shape=[jax.ShapeDtypeStruct((n, ROUTER_W), BF16), jax.ShapeDtypeStruct((n, ROUTER_W), F32),
                   jax.ShapeDtypeStruct((n_tiles * 8, TOK_TILE), F32),
                   jax.ShapeDtypeStruct((n_tiles * 8, ROUTER_W), F32),
                   jax.ShapeDtypeStruct((8, ROUTER_W), F32)],
        scratch_shapes=[pltpu.VMEM((1, ROUTER_W), F32)],
        compiler_params=_params("arbitrary"),
        name="moe_route",
    )(logits, tri, upper)


def _plan(tables, counts, n_blocks):
    n_tiles = tables.shape[0] // 8
    tables = tables.reshape(n_tiles, 8, ROUTER_W)[:, :3, MOE_GROUPS:MOE_GROUPS + MOE_EXPERTS].astype(jnp.int32)
    counts = counts[0, MOE_GROUPS:MOE_GROUPS + MOE_EXPERTS].astype(jnp.int32)
    padded = (counts + EXP_BLOCK - 1) // EXP_BLOCK * EXP_BLOCK
    pad_ends = jnp.cumsum(padded)
    pad_starts = pad_ends - padded
    run_len = tables[:, 0].reshape(-1)
    run_src = tables[:, 2].reshape(-1)
    run_dst = (tables[:, 1] + pad_starts[None, :]).reshape(-1)
    first_row = jnp.arange(n_blocks, dtype=jnp.int32) * EXP_BLOCK
    block_e = jnp.minimum(jnp.sum(first_row[:, None] >= pad_ends[None, :], axis=1),
                          MOE_EXPERTS - 1).astype(jnp.int32)
    n_valid = (pad_ends[-1] // EXP_BLOCK).astype(jnp.int32).reshape(1)
    clear_row = jnp.concatenate([pad_starts + counts, pad_ends[-1:]])
    clear_len = jnp.concatenate([padded - counts, n_blocks - n_valid])
    return run_len, run_src, run_dst, clear_row, clear_len, block_e, n_valid


RUN_BITS = tuple(b for b in (1 << k for k in range(TOK_TILE.bit_length() - 1, -1, -1)) if b >= RUN_ALIGN)
RARE_BITS = 4
SORT_ROWS = MOE_TOPK * TOK_TILE + MOE_EXPERTS * RUN_ALIGN


def _for_each_piece(length, src, dst, fn):
    def pieces(bits):
        for bit in bits:
            above = length & ~(2 * bit - 1)

            @pl.when((length & bit) != 0)
            def _(bit=bit, above=above):
                fn(pl.multiple_of(src + above, RUN_ALIGN), pl.multiple_of(dst + above, RUN_ALIGN), bit)

    @pl.when(length >= RUN_BITS[RARE_BITS - 1])
    def _():
        pieces(RUN_BITS[:RARE_BITS])
    pieces(RUN_BITS[RARE_BITS:])


def _dispatch_kernel(len_ref, src_ref, dst_ref, zrow_ref, zlen_ref, t_ref, ext_ref, lpt_ref, xs_ref,
                     buf, zeros, sem, zsem, *, n_tiles):
    i = pl.program_id(0)
    half = t_ref.shape[1] // 2

    @pl.when(i == 0)
    def _():
        zeros[...] = jnp.zeros_like(zeros)

        def clear(e, c):
            _for_each_piece(zlen_ref[e], 0, zrow_ref[e], lambda s, d, rows: pltpu.make_async_copy(
                zeros.at[pl.ds(0, rows)], xs_ref.at[pl.ds(d, rows)], zsem).start())
            return c
        lax.fori_loop(0, MOE_EXPERTS, clear, 0)

        def drain(e, c):
            _for_each_piece(zlen_ref[e], 0, zrow_ref[e], lambda s, d, rows: pltpu.make_async_copy(
                zeros.at[pl.ds(0, rows)], xs_ref.at[pl.ds(d, rows)], zsem).wait())
            return c
        lax.fori_loop(0, MOE_EXPERTS, drain, 0)

        def tail_copy(bk):
            row = pl.multiple_of(zrow_ref[MOE_EXPERTS] + bk * EXP_BLOCK, EXP_BLOCK)
            return pltpu.make_async_copy(zeros.at[pl.ds(0, EXP_BLOCK)], xs_ref.at[pl.ds(row, EXP_BLOCK)], zsem)

        def clear_tail(bk, c):
            tail_copy(bk).start()
            return c
        lax.fori_loop(0, zlen_ref[MOE_EXPERTS], clear_tail, 0)

        def drain_tail(bk, c):
            tail_copy(bk).wait()
            return c
        lax.fori_loop(0, zlen_ref[MOE_EXPERTS], drain_tail, 0)

    slot = i % 2

    def scatter(tile, sl, start):
        def body(e, c):
            k = tile * MOE_EXPERTS + e

            def piece(s, d, rows):
                cp = pltpu.make_async_copy(buf.at[sl, pl.ds(s, rows)], xs_ref.at[pl.ds(d, rows)], sem.at[sl])
                cp.start() if start else cp.wait()
            _for_each_piece(len_ref[k], src_ref[k], dst_ref[k], piece)
            return c
        lax.fori_loop(0, MOE_EXPERTS, body, 0)

    @pl.when(i >= 2)
    def _():
        scatter(i - 2, slot, False)

    lp = lpt_ref[...].astype(jnp.int32)
    j = lax.broadcasted_iota(jnp.int32, (SORT_ROWS, TOK_TILE), 0)
    perm = ((j == lp[0:1, :]) | (j == lp[1:2, :])).astype(F32).astype(BF16)
    rhs = jnp.concatenate([t_ref[...], ext_ref[...]], axis=1)
    srt = _dot(perm, rhs)
    bits = lax.bitcast_convert_type(srt, jnp.uint32)
    buf[slot, :, :half] = (bits[:, :half] >> 16) | (bits[:, half:2 * half] & jnp.uint32(0xFFFF0000))
    buf[slot, :, half:] = bits[:, 2 * half:]
    scatter(i, slot, True)

    @pl.when(i == n_tiles - 1)
    def _():
        if n_tiles > 1:
            scatter(i - 1, 1 - slot, False)
        scatter(i, slot, False)


def _expert_kernel(be_ref, nv_ref, x_ref, w1_ref, w3_ref, w2_ref, o_ref, w1b, w3b, w2b):
    i = pl.program_id(0)
    valid = i < nv_ref[0]
    half = o_ref.shape[1] // 2

    @pl.when(valid & ((i == 0) | (be_ref[i] != be_ref[jnp.maximum(i - 1, 0)])))
    def _():
        w1b[...] = w1_ref[...].astype(BF16)
        w3b[...] = w3_ref[...].astype(BF16)
        w2b[...] = w2_ref[...].astype(BF16)

    @pl.when(valid)
    def _():
        packed = x_ref[:, :half]
        lo = lax.bitcast_convert_type(packed << 16, F32)
        hi = lax.bitcast_convert_type(packed & jnp.uint32(0xFFFF0000), F32)
        x = jnp.concatenate([lo, hi], axis=1).astype(BF16)
        ext = lax.bitcast_convert_type(x_ref[:, half:], F32)
        g1 = ext[:, 0:1] + ext[:, 1:2] + ext[:, 2:3]
        g2 = ext[:, 3:4] + ext[:, 4:5] + ext[:, 5:6]
        gate = jnp.where(ext[:, 6:7] == be_ref[i].astype(F32), g1, g2)
        hdn = _silu(_dot(x, w1b[...])) * _dot(x, w3b[...])
        o_ref[...] = _dot(hdn.astype(BF16), w2b[...]) * gate

    @pl.when(jnp.logical_not(valid))
    def _():
        o_ref[...] = jnp.zeros_like(o_ref)


def _combine_kernel(len_ref, src_ref, dst_ref, ys_ref, lp_ref, h_ref, mod_ref, ln_ref, o_ref, buf, sem,
                    *, alpha, first_tile, n_steps):
    step = pl.program_id(0)
    tile = step + first_tile
    slot = step % 2

    def gather(tl, sl, start):
        def body(e, c):
            k = tl * MOE_EXPERTS + e

            def piece(s, d, rows):
                cp = pltpu.make_async_copy(ys_ref.at[pl.ds(d, rows)], buf.at[sl, pl.ds(s, rows)], sem.at[sl])
                cp.start() if start else cp.wait()
            _for_each_piece(len_ref[k], src_ref[k], dst_ref[k], piece)
            return c
        lax.fori_loop(0, MOE_EXPERTS, body, 0)

    @pl.when(step == 0)
    def _():
        buf[...] = jnp.zeros_like(buf)
        gather(tile, slot, True)

    @pl.when(step + 1 < n_steps)
    def _():
        gather(tile + 1, 1 - slot, True)
    gather(tile, slot, False)

    lp = lp_ref[...].astype(jnp.int32)
    j = lax.broadcasted_iota(jnp.int32, (TOK_TILE, SORT_ROWS), 1)
    pick = ((j == lp[:, 0:1]) | (j == lp[:, 1:2])).astype(F32).astype(BF16)
    y = buf[slot]
    hi = y.astype(BF16)
    r1 = y - hi.astype(F32)
    mid = r1.astype(BF16)
    lo = (r1 - mid.astype(F32)).astype(BF16)
    f = _dot(pick, hi) + _dot(pick, mid) + _dot(pick, lo)
    o_ref[...] = _layer_norm(alpha * h_ref[...] + mod_ref[5:6, :] * f, ln_ref[0:1, :], ln_ref[1:2, :])


def _moe(lay, t, logits, h, mod, ln, w1, w3, w2, layer, alpha, first_tile=0):
    n, d = t.shape
    hid = w1.shape[3]
    n_tiles = n // TOK_TILE
    max_rows = n * MOE_TOPK + n_tiles * MOE_EXPERTS * (RUN_ALIGN - 1)
    n_blocks = -(-max_rows // EXP_BLOCK) + MOE_EXPERTS
    n_rows = n_blocks * EXP_BLOCK
    xs_w = d // 2 + ROUTER_W
    ext, lp, lpt, tables, counts = _route(logits)
    run_len, run_src, run_dst, zrow, zlen, block_e, n_valid = _plan(tables, counts, n_blocks)

    xs = pl.pallas_call(
        functools.partial(_dispatch_kernel, n_tiles=n_tiles),
        grid_spec=pltpu.PrefetchScalarGridSpec(
            num_scalar_prefetch=5,
            grid=(n_tiles,),
            in_specs=[pl.BlockSpec((TOK_TILE, d), lambda i, *_: (i, 0)),
                      pl.BlockSpec((TOK_TILE, ROUTER_W), lambda i, *_: (i, 0)),
                      pl.BlockSpec((8, TOK_TILE), lambda i, *_: (i, 0))],
            out_specs=pl.BlockSpec(memory_space=pl.ANY),
            scratch_shapes=[pltpu.VMEM((2, SORT_ROWS, xs_w), jnp.uint32),
                            pltpu.VMEM((max(TOK_TILE, EXP_BLOCK), xs_w), jnp.uint32),
                            pltpu.SemaphoreType.DMA((2,)), pltpu.SemaphoreType.DMA],
        ),
        out_shape=jax.ShapeDtypeStruct((n_rows, xs_w), jnp.uint32),
        compiler_params=_params("arbitrary"),
        name="moe_dispatch",
    )(run_len, run_src, run_dst, zrow, zlen, t, ext, lpt)

    blk = lambda i, be, nv: (jnp.minimum(i, nv[0] - 1), 0)
    ys = pl.pallas_call(
        _expert_kernel,
        grid_spec=pltpu.PrefetchScalarGridSpec(
            num_scalar_prefetch=2,
            grid=(n_blocks,),
            in_specs=[
                pl.BlockSpec((EXP_BLOCK, xs_w), blk),
                pl.BlockSpec((None, None, d, hid), lambda i, be, nv: (layer, be[i], 0, 0)),
                pl.BlockSpec((None, None, d, hid), lambda i, be, nv: (layer, be[i], 0, 0)),
                pl.BlockSpec((None, None, hid, d), lambda i, be, nv: (layer, be[i], 0, 0)),
            ],
            out_specs=pl.BlockSpec((EXP_BLOCK, d), lambda i, be, nv: (i, 0)),
            scratch_shapes=[pltpu.VMEM((d, hid), BF16), pltpu.VMEM((d, hid), BF16), pltpu.VMEM((hid, d), BF16)],
        ),
        out_shape=jax.ShapeDtypeStruct((n_rows, d), F32),
        compiler_params=_params("arbitrary"),
        name="moe_experts",
    )(block_e, n_valid, xs, w1, w3, w2)

    n_steps = n_tiles - first_tile
    return pl.pallas_call(
        functools.partial(_combine_kernel, alpha=alpha, first_tile=first_tile, n_steps=n_steps),
        grid_spec=pltpu.PrefetchScalarGridSpec(
            num_scalar_prefetch=3,
            grid=(n_steps,),
            in_specs=[pl.BlockSpec(memory_space=pl.ANY),
                      pl.BlockSpec((TOK_TILE, ROUTER_W), lambda i, *_: (i + first_tile, 0)),
                      pl.BlockSpec((TOK_TILE, d), lambda i, *_: (i + first_tile, 0)),
                      pl.BlockSpec((None, 6, d), lambda i, *_: (lay.mod_row(i + first_tile, TOK_TILE), 0, 0)),
                      pl.BlockSpec((2, d), lambda i, *_: (0, 0))],
            out_specs=pl.BlockSpec((TOK_TILE, d), lambda i, *_: (i, 0)),
            scratch_shapes=[pltpu.VMEM((2, SORT_ROWS, d), F32), pltpu.SemaphoreType.DMA((2,))],
        ),
        out_shape=jax.ShapeDtypeStruct((n_steps * TOK_TILE, d), F32),
        compiler_params=_params("arbitrary"),
        name="moe_combine",
    )(run_len, run_src, run_dst, ys, lp, h, mod, ln)


def kernel(x, c, ctx, c_ctx, ada_w, ada_b, ln_g, ln_b, ret_w_in, ret_decay, ret_gn_g, ret_w_out, mla_w_down, mla_q_norm, mla_kv_norm, mla_w_uq, mla_w_ukv, mla_w_out, conv_w_in, conv_w, conv_b, conv_w_out, moe_w_group, moe_b_group, moe_w_expert, moe_b_expert, moe_w1, moe_w3, moe_w2):
    b, s, d = x.shape
    depth = ada_w.shape[0]
    lay = _Layout(b, s, ctx.shape[1])
    alpha = (2.0 * depth) ** 0.25

    mod_rows = -(-(b + 1) // 8) * 8
    cc = jnp.concatenate([c, c_ctx[None, :], jnp.zeros((mod_rows - b - 1, d), F32)], axis=0)
    mod_all = _ada(cc, ada_w, ada_b).reshape(depth, mod_rows, 6, d)

    h = (ctx.reshape(lay.n_ctx, d), x.reshape(b * s, d))
    for i in range(depth):
        kind, j = i % N_MIXERS, i // N_MIXERS
        mod = mod_all[i]
        ln1 = jnp.stack([ln_g[i, 0], ln_b[i, 0]])
        ln2 = jnp.stack([ln_g[i, 1], ln_b[i, 1]])
        pad = jnp.zeros((d, ROUTER_W - MOE_GROUPS - MOE_EXPERTS), F32)
        w_router = jnp.concatenate([moe_w_group[i], moe_w_expert[i], pad], axis=1).astype(BF16)
        b_router = jnp.concatenate([moe_b_group[i], moe_b_expert[i], pad[0]]).reshape(1, ROUTER_W)
        if kind == 0:
            a = _retention(lay, h, mod, ret_w_in[j], ret_decay[j], ret_gn_g[j], d)
            h1, t, logits = _post(lay, a, h, mod, ret_w_out[j], ln1, w_router, b_router, alpha)
        elif kind == 1:
            a = _mla(lay, h, mod, mla_w_down[j], mla_q_norm[j], mla_kv_norm[j], mla_w_uq[j], mla_w_ukv[j], d)
            h1, t, logits = _post(lay, a, h, mod, mla_w_out[j], ln1, w_router, b_router, alpha)
        else:
            h1, t, logits = _conv_layer(lay, h, mod, conv_w_in[j], conv_w[j], conv_b[j], conv_w_out[j],
                                        ln1, w_router, b_router, alpha)
        first_tile = lay.n_ctx // TOK_TILE if i == depth - 1 else 0
        h = _moe(lay, t, logits, h1, mod, ln2, moe_w1, moe_w3, moe_w2, i, alpha, first_tile)
    return h.reshape(b, s, d)
```

```python
import functools

import jax
import jax.numpy as jnp
from jax import lax
from jax.experimental import pallas as pl
from jax.experimental.pallas import tpu as pltpu

F32 = jnp.float32
BF16 = jnp.bfloat16

GRID_W = 64
LN_EPS = 1e-5
RMS_EPS = 1e-6
ROPE_BASE = 10000.0
N_MIXERS = 3
RET_HEADS = 4
MLA_HEADS = 8
MLA_NOPE = 128
MLA_ROPE = 64
MLA_V = 128
MLA_QK = 256
MOE_GROUPS = 4
MOE_EPG = 8
MOE_EXPERTS = MOE_GROUPS * MOE_EPG
MOE_TOPK = 2
ROUTER_W = 128

CHUNK = 256
TM = 512
TQ = 512
TOK_TILE = 512
EXP_BLOCK = 512
RUN_ALIGN = 8
VMEM_LIMIT = 56 * 1024 * 1024


def _params(*sem):
    return pltpu.CompilerParams(dimension_semantics=sem, vmem_limit_bytes=VMEM_LIMIT)


def _const_spec(shape):
    nd = len(shape)
    return pl.BlockSpec(shape, lambda *_: (0,) * nd, pipeline_mode=pl.Buffered(1))


def _dot(a, b):
    return jnp.dot(a, b, preferred_element_type=F32)


def _dot_nt(a, b):
    return lax.dot_general(a, b, (((1,), (1,)), ((), ())), preferred_element_type=F32)


def _dot_tn(a, b):
    return lax.dot_general(a, b, (((0,), (0,)), ((), ())), preferred_element_type=F32)


def _silu(x):
    return x * jax.nn.sigmoid(x)


def _layer_norm(x, g, b):
    mu = jnp.mean(x, axis=-1, keepdims=True)
    xc = x - mu
    var = jnp.mean(xc * xc, axis=-1, keepdims=True)
    return xc * lax.rsqrt(var + LN_EPS) * g + b


def _rms_norm(x, g):
    return x * lax.rsqrt(jnp.mean(x * x, axis=-1, keepdims=True) + RMS_EPS) * g


def _ada_kernel(c_ref, w_ref, b_ref, o_ref):
    a = _silu(c_ref[...]).astype(BF16)
    o_ref[...] = _dot(a, w_ref[...].astype(BF16)) + b_ref[...]


def _ada(cc, ada_w, ada_b):
    depth, d, n6 = ada_w.shape
    rows = cc.shape[0]
    tn = 1536
    return pl.pallas_call(
        _ada_kernel,
        grid=(depth, n6 // tn),
        in_specs=[
            pl.BlockSpec((rows, d), lambda l, j: (0, 0)),
            pl.BlockSpec((None, d, tn), lambda l, j: (l, 0, j)),
            pl.BlockSpec((None, 1, tn), lambda l, j: (l, 0, j)),
        ],
        out_specs=pl.BlockSpec((None, rows, tn), lambda l, j: (l, 0, j)),
        out_shape=jax.ShapeDtypeStruct((depth, rows, n6), F32),
        compiler_params=_params("parallel", "parallel"),
        name="ada_mod",
    )(cc, ada_w, ada_b.reshape(depth, 1, n6))


class _Layout:
    def __init__(self, batch, seq, ctx_len):
        assert ctx_len == CHUNK and seq % TM == 0 and seq % CHUNK == 0
        self.batch, self.seq, self.ctx = batch, seq, ctx_len
        self.n_ctx = batch * ctx_len
        self.n = self.n_ctx + batch * seq
        assert self.n_ctx % TM == 0 and self.n_ctx % seq == 0 and self.n_ctx % TOK_TILE == 0
        assert self.n_ctx % TQ == 0 and seq % TQ == 0 and seq % TOK_TILE == 0

    def mod_row(self, tile, rows_per_tile):
        ctx_tiles = self.n_ctx // rows_per_tile
        per_batch = self.seq // rows_per_tile
        return jnp.where(tile < ctx_tiles, self.batch, (tile - ctx_tiles) // per_batch)

    def pos_block(self, tile, rows_per_tile):
        ctx_tiles = self.n_ctx // rows_per_tile
        per_batch = self.seq // rows_per_tile
        return jnp.where(tile < ctx_tiles, 0, 1 + (tile - ctx_tiles) % per_batch)


def _row_specs(x, rows, ctx_tiles):
    if not isinstance(x, tuple):
        return (x,), [pl.BlockSpec((rows, x.shape[1]), lambda i: (i, 0))]
    width = x[0].shape[1]
    return x, [pl.BlockSpec((rows, width), lambda i: (jnp.minimum(i, ctx_tiles - 1), 0)),
               pl.BlockSpec((rows, width), lambda i: (jnp.maximum(i - ctx_tiles, 0), 0))]


def _row_tile(refs, ctx_tiles):
    if len(refs) == 1:
        return refs[0][...]
    return jnp.where(pl.program_id(0) < ctx_tiles, refs[0][...], refs[1][...])


def _rope_angles(seq, dim):
    n_rows = seq // GRID_W
    rows = jnp.repeat(jnp.arange(n_rows, dtype=F32), GRID_W)
    cols = jnp.tile(jnp.arange(GRID_W, dtype=F32), n_rows)
    quarter = dim // 4
    inv_freq = ROPE_BASE ** (-jnp.arange(quarter, dtype=F32) / quarter)
    return jnp.concatenate([rows[:, None] * inv_freq, cols[:, None] * inv_freq], axis=-1)


def _with_identity_block(table, fill, rows):
    ident = jnp.full((rows, table.shape[1]), fill, F32)
    return jnp.concatenate([ident, table], axis=0)


def _proj_ret_kernel(*refs, dk, ctx_tiles):
    h_refs = refs[:-14]
    (mod_ref, cos_ref, sin_ref, kdec_ref, wq_ref, wk_ref, wv_ref, wg_ref,
     q_ref, k_ref, kf_ref, kb_ref, v_ref, g_ref) = refs[-14:]
    u = (_row_tile(h_refs, ctx_tiles) * (1.0 + mod_ref[1:2, :]) + mod_ref[0:1, :]).astype(BF16)
    cos = cos_ref[...]
    sin = sin_ref[...]
    half = dk // 2
    yq = _dot(u, wq_ref[...])
    for hd in range(RET_HEADS):
        x1 = yq[:, hd * dk:hd * dk + half]
        x2 = yq[:, hd * dk + half:(hd + 1) * dk]
        q_ref[:, hd * dk:hd * dk + half] = (x1 * cos - x2 * sin).astype(BF16)
        q_ref[:, hd * dk + half:(hd + 1) * dk] = (x1 * sin + x2 * cos).astype(BF16)
    yk = _dot(u, wk_ref[...])
    k_scale = dk ** -0.5
    for hd in range(RET_HEADS):
        x1 = yk[:, hd * dk:hd * dk + half]
        x2 = yk[:, hd * dk + half:(hd + 1) * dk]
        df = kdec_ref[:, hd:hd + 1]
        db = kdec_ref[:, RET_HEADS + hd:RET_HEADS + hd + 1]
        for part, o in ((0, (x1 * cos - x2 * sin) * k_scale), (1, (x1 * sin + x2 * cos) * k_scale)):
            sl = slice(hd * dk + part * half, hd * dk + (part + 1) * half)
            k_ref[:, sl] = o.astype(BF16)
            kf_ref[:, sl] = (o * df).astype(BF16)
            kb_ref[:, sl] = (o * db).astype(BF16)
    v_ref[...] = _dot(u, wv_ref[...]).astype(BF16)
    g_ref[...] = _silu(_dot(u, wg_ref[...]))


def _ret_state_kernel(cdec_ref, kd_ref, v_ref, s_ref, state_ref, *, dk, dv):
    @pl.when(pl.program_id(1) == 0)
    def _():
        state_ref[...] = jnp.zeros_like(state_ref)

    for hd in range(RET_HEADS):
        st = state_ref[hd]
        s_ref[hd] = st.astype(BF16)
        kd = kd_ref[:, hd * dk:(hd + 1) * dk]
        v = v_ref[:, hd * dv:(hd + 1) * dv]
        state_ref[hd] = st * cdec_ref[RET_HEADS + hd] + _dot_tn(kd, v)


def _ret_out_kernel(cdec_ref, q_ref, k_ref, kd_ref, v_ref, sb_ref, mask_ref, qdec_ref, g_ref, gn_ref, o_ref,
                    state_ref, *, dk, dv):
    @pl.when(pl.program_id(1) == 0)
    def _():
        state_ref[...] = jnp.zeros_like(state_ref)

    for hd in range(RET_HEADS):
        q = q_ref[:, hd * dk:(hd + 1) * dk]
        k = k_ref[:, hd * dk:(hd + 1) * dk]
        kd = kd_ref[:, hd * dk:(hd + 1) * dk]
        v = v_ref[:, hd * dv:(hd + 1) * dv]
        p = (_dot_nt(q, k) * mask_ref[hd]).astype(BF16)
        st = state_ref[hd]
        o = (_dot(p, v) + qdec_ref[:, hd:hd + 1] * _dot(q, st.astype(BF16))
             + qdec_ref[:, RET_HEADS + hd:RET_HEADS + hd + 1] * _dot(q, sb_ref[hd]))
        state_ref[hd] = st * cdec_ref[hd] + _dot_tn(kd, v)
        mu = jnp.mean(o, axis=-1, keepdims=True)
        oc = o - mu
        var = jnp.mean(oc * oc, axis=-1, keepdims=True)
        on = oc * lax.rsqrt(var + LN_EPS) * gn_ref[:, hd * dv:(hd + 1) * dv]
        o_ref[:, hd * dv:(hd + 1) * dv] = (g_ref[:, hd * dv:(hd + 1) * dv] * on).astype(BF16)


def _retention(lay, h, mod, w_in, decay_logit, gn_g, d_model):
    n, b, nc = lay.n, lay.batch, lay.seq // CHUNK
    dk = d_model // RET_HEADS
    dv = 2 * dk
    qk, vw = RET_HEADS * dk, RET_HEADS * dv
    w_in = w_in.astype(BF16)
    wq, wk, wv, wg = w_in[:, :qk], w_in[:, qk:2 * qk], w_in[:, 2 * qk:2 * qk + vw], w_in[:, 2 * qk + vw:]

    ang = _rope_angles(lay.seq, dk)
    cos_t = _with_identity_block(jnp.cos(ang), 1.0, TM)
    sin_t = _with_identity_block(jnp.sin(ang), 0.0, TM)

    lg = jax.nn.log_sigmoid(decay_logit.astype(F32))
    idx = jnp.arange(CHUNK, dtype=F32)
    k_pow = jnp.stack([CHUNK - 1.0 - idx, idx])
    q_pow = jnp.stack([idx + 1.0, CHUNK - idx])
    kdec = jnp.exp(k_pow[:, :, None] * lg[:, None, :])
    qdec = jnp.exp(q_pow[:, :, None] * lg[:, None, :])
    kdec = jnp.moveaxis(kdec, 0, 1).reshape(CHUNK, 2 * RET_HEADS)
    qdec = jnp.moveaxis(qdec, 0, 1).reshape(CHUNK, 2 * RET_HEADS)
    cdec = jnp.exp(CHUNK * lg).reshape(2 * RET_HEADS)
    rel = idx[:, None] - idx[None, :]
    rel = jnp.stack([rel, -rel])
    intra = jnp.where(rel[:, None] >= 0, jnp.exp(jnp.maximum(rel[:, None], 0.0) * lg[:, :, None, None]), 0.0)
    kdec_tm = jnp.tile(kdec, (TM // CHUNK, 1))

    n_tiles = n // TM
    row = lambda i: (i, 0)
    ctx_tiles = lay.n_ctx // TM
    h, h_specs = _row_specs(h, TM, ctx_tiles)
    q, k, kf, kb, v, g = pl.pallas_call(
        functools.partial(_proj_ret_kernel, dk=dk, ctx_tiles=ctx_tiles),
        grid=(n_tiles,),
        in_specs=h_specs + [
            pl.BlockSpec((None, 6, d_model), lambda i: (lay.mod_row(i, TM), 0, 0)),
            pl.BlockSpec((TM, dk // 2), lambda i: (lay.pos_block(i, TM), 0)),
            pl.BlockSpec((TM, dk // 2), lambda i: (lay.pos_block(i, TM), 0)),
            _const_spec((TM, 2 * RET_HEADS)),
            _const_spec((d_model, qk)), _const_spec((d_model, qk)),
            _const_spec((d_model, vw)), _const_spec((d_model, vw)),
        ],
        out_specs=[pl.BlockSpec((TM, qk), row)] * 4 + [pl.BlockSpec((TM, vw), row)] * 2,
        out_shape=[jax.ShapeDtypeStruct((n, qk), BF16)] * 4
        + [jax.ShapeDtypeStruct((n, vw), BF16), jax.ShapeDtypeStruct((n, vw), F32)],
        compiler_params=_params("parallel"),
        name="ret_proj",
    )(*h, mod, cos_t, sin_t, kdec_tm, wq, wk, wv, wg)

    def chunk_fwd(bi, c):
        return (jnp.where(c == 0, bi, b + bi * nc + c - 1), 0)

    def chunk_bwd(bi, c):
        return (jnp.where(c == 0, bi, b + bi * nc + nc - c), 0)

    n_chunks = n // CHUNK
    s_b = pl.pallas_call(
        functools.partial(_ret_state_kernel, dk=dk, dv=dv),
        grid=(b, nc + 1),
        in_specs=[pl.BlockSpec(memory_space=pltpu.SMEM),
                  pl.BlockSpec((CHUNK, qk), chunk_bwd), pl.BlockSpec((CHUNK, vw), chunk_bwd)],
        out_specs=pl.BlockSpec((None, RET_HEADS, dk, dv), lambda bi, c: (chunk_bwd(bi, c)[0], 0, 0, 0)),
        out_shape=jax.ShapeDtypeStruct((n_chunks, RET_HEADS, dk, dv), BF16),
        scratch_shapes=[pltpu.VMEM((RET_HEADS, dk, dv), F32)],
        compiler_params=_params("parallel", "arbitrary"),
        name="ret_state_bwd",
    )(cdec, kb, v)

    return pl.pallas_call(
        functools.partial(_ret_out_kernel, dk=dk, dv=dv),
        grid=(b, nc + 1),
        in_specs=[
            pl.BlockSpec(memory_space=pltpu.SMEM),
            pl.BlockSpec((CHUNK, qk), chunk_fwd), pl.BlockSpec((CHUNK, qk), chunk_fwd),
            pl.BlockSpec((CHUNK, qk), chunk_fwd), pl.BlockSpec((CHUNK, vw), chunk_fwd),
            pl.BlockSpec((None, RET_HEADS, dk, dv), lambda bi, c: (chunk_fwd(bi, c)[0], 0, 0, 0)),
            _const_spec((RET_HEADS, CHUNK, CHUNK)), _const_spec((CHUNK, 2 * RET_HEADS)),
            pl.BlockSpec((CHUNK, vw), chunk_fwd), _const_spec((1, vw)),
        ],
        out_specs=pl.BlockSpec((CHUNK, vw), chunk_fwd),
        out_shape=jax.ShapeDtypeStruct((n, vw), BF16),
        scratch_shapes=[pltpu.VMEM((RET_HEADS, dk, dv), F32)],
        compiler_params=_params("parallel", "arbitrary"),
        name="ret_scan_out",
    )(cdec, q, k, kf, v, s_b, intra[0] + intra[1], qdec, g, gn_g.reshape(1, vw).astype(F32))


def _proj_mla_kernel(h_ref, mod_ref, ct_ref, st_ref, wdq_ref, wdkv_ref, wkr_ref, qn_ref, kvn_ref,
                     wuq_ref, wukn_ref, wuv_ref, q_ref, k_ref, vt_ref):
    u = (h_ref[...] * (1.0 + mod_ref[1:2, :]) + mod_ref[0:1, :]).astype(BF16)
    ct = ct_ref[...]
    st = st_ref[...]
    half = MLA_QK // 2

    def rope(x):
        return x * ct + pltpu.roll(x, half // 2, 1) * st

    cq = _rms_norm(_dot(u, wdq_ref[...]), qn_ref[...]).astype(BF16)
    yq = _dot(cq, wuq_ref[...])
    ckv = _rms_norm(_dot(u, wdkv_ref[...]), kvn_ref[...]).astype(BF16)
    kn = _dot(ckv, wukn_ref[...])
    kr = rope(_dot(u, wkr_ref[...])).astype(BF16)
    for hd in range(MLA_HEADS):
        q_ref[:, hd * MLA_QK:hd * MLA_QK + half] = yq[:, hd * MLA_QK:hd * MLA_QK + half].astype(BF16)
        q_ref[:, hd * MLA_QK + half:(hd + 1) * MLA_QK] = rope(
            yq[:, hd * MLA_QK + half:(hd + 1) * MLA_QK]).astype(BF16)
        k_ref[:, hd * MLA_QK:hd * MLA_QK + half] = kn[:, hd * half:(hd + 1) * half].astype(BF16)
        k_ref[:, hd * MLA_QK + half:(hd + 1) * MLA_QK] = kr
    vt_ref[...] = _dot(ckv, wuv_ref[...]).T.astype(BF16)


KV_CHUNK = 512
LOG2_E = 1.4426950408889634


def _kv_chunks(kv_refs, n_kv):
    chunks = []
    row = 0
    for j in range(n_kv):
        keys = kv_refs[2 * j].shape[0]
        step = min(keys, KV_CHUNK)
        for off in range(0, keys, step):
            chunks.append((j, off, row, step))
            row += step
    return chunks


def _attn_kernel(q_ref, *rest, scale, n_kv, pipelined):
    kv_refs = rest[:2 * n_kv]
    chunks = _kv_chunks(kv_refs, n_kv)

    def scores(c, s_ref, m):
        j, off, row, w = c
        s = _dot_nt(kv_refs[2 * j][off:off + w, :], q_ref[...])
        s_ref[row:row + w, :] = s
        cm = jnp.max(s, axis=0, keepdims=True)
        return cm if m is None else jnp.maximum(m, cm)

    def values(c, s_ref, m, acc, den):
        j, off, row, w = c
        p = jnp.exp2((s_ref[row:row + w, :] - m) * (scale * LOG2_E))
        cs = jnp.sum(p, axis=0, keepdims=True)
        pv = _dot(kv_refs[2 * j + 1][:, off:off + w], p.astype(BF16))
        return (pv if acc is None else acc + pv), (cs if den is None else den + cs)

    if not pipelined:
        o_ref, s_ref = rest[-2:]
        m = acc = den = None
        for c in chunks:
            m = scores(c, s_ref, m)
        for c in chunks:
            acc, den = values(c, s_ref, m, acc, den)
        o_ref[...] = (acc / den).T.astype(BF16)
        return

    o_ref, s0_ref, s1_ref, m0_ref, m1_ref = rest[-5:]
    t = pl.program_id(0)
    last = pl.num_programs(0) - 1
    bufs = ((s0_ref, m0_ref), (s1_ref, m1_ref))

    def step(new, old):
        m_prev = old[1][...] if old else None
        m = acc = den = None
        for c in chunks:
            if new:
                m = scores(c, new[0], m)
            if old:
                acc, den = values(c, old[0], m_prev, acc, den)
        if new:
            new[1][...] = m
        if old:
            o_ref[...] = (acc / den).T.astype(BF16)

    @pl.when(t == 0)
    def _():
        step(bufs[0], None)

    for parity in (0, 1):
        @pl.when((t > 0) & (t < last) & (t % 2 == parity))
        def _(parity=parity):
            step(bufs[parity], bufs[1 - parity])

        @pl.when((t == last) & (t % 2 == parity))
        def _(parity=parity):
            step(None, bufs[1 - parity])


def _pad_rope_cols(w):
    z = jnp.zeros((w.shape[0], MLA_ROPE // 2), w.dtype)
    return jnp.concatenate([w[:, :MLA_ROPE // 2], z, w[:, MLA_ROPE // 2:], z], axis=1)


def _mla(lay, h, mod, w_down, q_norm, kv_norm, w_uq, w_ukv, d_model):
    n, b, s = lay.n, lay.batch, lay.seq
    q_lora, kv_lora = q_norm.shape[0], kv_norm.shape[0]
    w_dq = w_down[:, :q_lora].astype(BF16)
    w_dkv = w_down[:, q_lora:q_lora + kv_lora].astype(BF16)
    w_kr = _pad_rope_cols(w_down[:, q_lora + kv_lora:]).astype(BF16)
    w_uq = w_uq.reshape(q_lora, MLA_HEADS, MLA_NOPE + MLA_ROPE)
    w_uq = jnp.concatenate(
        [w_uq[:, :, :MLA_NOPE], jax.vmap(_pad_rope_cols, 1, 1)(w_uq[:, :, MLA_NOPE:])], axis=2)
    w_uq = w_uq.reshape(q_lora, MLA_HEADS * MLA_QK).astype(BF16)
    w_ukv = w_ukv.reshape(kv_lora, MLA_HEADS, MLA_NOPE + MLA_V)
    w_ukn = w_ukv[:, :, :MLA_NOPE].reshape(kv_lora, MLA_HEADS * MLA_NOPE).astype(BF16)
    w_uv = w_ukv[:, :, MLA_NOPE:].reshape(kv_lora, MLA_HEADS * MLA_V).astype(BF16)

    ang = _rope_angles(s, MLA_ROPE)
    z = jnp.zeros_like(ang)
    ct = _with_identity_block(jnp.concatenate([jnp.cos(ang), z, jnp.cos(ang), z], axis=1), 1.0, TM)
    st = _with_identity_block(jnp.concatenate([-jnp.sin(ang), z, jnp.sin(ang), z], axis=1), 0.0, TM)

    row = lambda i: (i, 0)
    qw, vw = MLA_HEADS * MLA_QK, MLA_HEADS * MLA_V
    q, k, vt = pl.pallas_call(
        _proj_mla_kernel,
        grid=(n // TM,),
        in_specs=[
            pl.BlockSpec((TM, d_model), row),
            pl.BlockSpec((None, 6, d_model), lambda i: (lay.mod_row(i, TM), 0, 0)),
            pl.BlockSpec((TM, MLA_QK // 2), lambda i: (lay.pos_block(i, TM), 0)),
            pl.BlockSpec((TM, MLA_QK // 2), lambda i: (lay.pos_block(i, TM), 0)),
            _const_spec((d_model, q_lora)), _const_spec((d_model, kv_lora)), _const_spec((d_model, MLA_QK // 2)),
            _const_spec((1, q_lora)), _const_spec((1, kv_lora)),
            _const_spec((q_lora, qw)), _const_spec((kv_lora, vw)), _const_spec((kv_lora, vw)),
        ],
        out_specs=[pl.BlockSpec((TM, qw), row), pl.BlockSpec((TM, qw), row),
                   pl.BlockSpec((vw, TM), lambda i: (0, i))],
        out_shape=[jax.ShapeDtypeStruct((n, qw), BF16), jax.ShapeDtypeStruct((n, qw), BF16),
                   jax.ShapeDtypeStruct((vw, n), BF16)],
        compiler_params=_params("parallel"),
        name="mla_proj",
    )(h, mod, ct, st, w_dq, w_dkv, w_kr, q_norm.reshape(1, -1).astype(F32), kv_norm.reshape(1, -1).astype(F32),
      w_uq, w_ukn, w_uv)

    scale = (MLA_NOPE + MLA_ROPE) ** -0.5
    ctx_q = lambda bi, hd: (bi, hd)
    att_ctx = pl.pallas_call(
        functools.partial(_attn_kernel, scale=scale, n_kv=1, pipelined=False),
        grid=(b, MLA_HEADS),
        in_specs=[pl.BlockSpec((CHUNK, MLA_QK), ctx_q), pl.BlockSpec((CHUNK, MLA_QK), ctx_q),
                  pl.BlockSpec((MLA_V, CHUNK), lambda bi, hd: (hd, bi))],
        out_specs=pl.BlockSpec((CHUNK, MLA_V), ctx_q),
        out_shape=jax.ShapeDtypeStruct((lay.n_ctx, vw), BF16),
        scratch_shapes=[pltpu.VMEM((CHUNK, CHUNK), F32)],
        compiler_params=_params("parallel", "parallel"),
        name="mla_attn_ctx",
    )(q, k, vt)

    lat_blk = lay.n_ctx // s
    q_tiles = s // TQ
    n_work = b * MLA_HEADS * q_tiles

    def work(f):
        bh, t = f // q_tiles, f % q_tiles
        return bh // MLA_HEADS, bh % MLA_HEADS, t

    def scored(f):
        return work(jnp.minimum(f, n_work - 1))

    def finished(f):
        return work(jnp.maximum(f - 1, 0))

    def q_map(f):
        bi, hd, t = scored(f)
        return (lay.n_ctx // TQ + bi * q_tiles + t, hd)

    def out_map(f):
        bi, hd, t = finished(f)
        return (bi * q_tiles + t, hd)

    att_lat = pl.pallas_call(
        functools.partial(_attn_kernel, scale=scale, n_kv=2, pipelined=True),
        grid=(n_work + 1,),
        in_specs=[pl.BlockSpec((TQ, MLA_QK), q_map),
                  pl.BlockSpec((CHUNK, MLA_QK), lambda f: scored(f)[:2]),
                  pl.BlockSpec((MLA_V, CHUNK), lambda f: (finished(f)[1], finished(f)[0])),
                  pl.BlockSpec((s, MLA_QK), lambda f: (lat_blk + scored(f)[0], scored(f)[1])),
                  pl.BlockSpec((MLA_V, s), lambda f: (finished(f)[1], lat_blk + finished(f)[0]))],
        out_specs=pl.BlockSpec((TQ, MLA_V), out_map),
        out_shape=jax.ShapeDtypeStruct((n - lay.n_ctx, vw), BF16),
        scratch_shapes=[pltpu.VMEM((CHUNK + s, TQ), F32)] * 2 + [pltpu.VMEM((1, TQ), F32)] * 2,
        compiler_params=_params("arbitrary"),
        name="mla_attn_lat",
    )(q, k, vt, k, vt)
    return att_ctx, att_lat


def _post_tail(y, h, mod_ref, ln_ref, wr_ref, br_ref, h_out, t_out, lg_out, alpha):
    h1 = _layer_norm(alpha * h + mod_ref[2:3, :] * y, ln_ref[0:1, :], ln_ref[1:2, :])
    t = h1 * (1.0 + mod_ref[4:5, :]) + mod_ref[3:4, :]
    t = t.astype(BF16)
    h_out[...] = h1
    t_out[...] = t
    lg_out[...] = _dot(t, wr_ref[...]) + br_ref[...]


def _post_kernel(*refs, alpha, ctx_tiles, n_a):
    a_refs, h_refs = refs[:n_a], refs[n_a:-8]
    mod_ref, w_ref, ln_ref, wr_ref, br_ref, h_out, t_out, lg_out = refs[-8:]
    y = _dot(_row_tile(a_refs, ctx_tiles), w_ref[...])
    _post_tail(y, _row_tile(h_refs, ctx_tiles), mod_ref, ln_ref, wr_ref, br_ref, h_out, t_out, lg_out, alpha)


def _post(lay, a, h, mod, w_out, ln, w_router, b_router, alpha):
    n, d = lay.n, w_out.shape[1]
    row = lambda i: (i, 0)
    ctx_tiles = lay.n_ctx // TM
    a, a_specs = _row_specs(a, TM, ctx_tiles)
    h, h_specs = _row_specs(h, TM, ctx_tiles)
    ka = a[0].shape[1]
    return pl.pallas_call(
        functools.partial(_post_kernel, alpha=alpha, ctx_tiles=ctx_tiles, n_a=len(a)),
        grid=(n // TM,),
        in_specs=a_specs + h_specs + [
            pl.BlockSpec((None, 6, d), lambda i: (lay.mod_row(i, TM), 0, 0)),
            _const_spec((ka, d)), _const_spec((2, d)), _const_spec((d, ROUTER_W)), _const_spec((1, ROUTER_W)),
        ],
        out_specs=[pl.BlockSpec((TM, d), row), pl.BlockSpec((TM, d), row), pl.BlockSpec((TM, ROUTER_W), row)],
        out_shape=[jax.ShapeDtypeStruct((n, d), F32), jax.ShapeDtypeStruct((n, d), BF16),
                   jax.ShapeDtypeStruct((n, ROUTER_W), F32)],
        compiler_params=_params("parallel"),
        name="post_mixer",
    )(*a, *h, mod, w_out.astype(BF16), ln, w_router, b_router)


HALO = 8


def _conv_kernel(h_ref, hp_ref, hn_ref, mod_ref, win_ref, cw_ref, cb_ref, w_ref, ln_ref, wr_ref, br_ref,
                 h_out, t_out, lg_out, *, alpha, tiles_per_seq, ctx_tiles):
    i = pl.program_id(0)
    rows, d = h_ref.shape
    h = h_ref[...]
    hx = jnp.concatenate([hp_ref[...], h, hn_ref[...]], axis=0)
    u = (hx * (1.0 + mod_ref[1:2, :]) + mod_ref[0:1, :]).astype(BF16)
    y = _dot(u, win_ref[...])
    gate_b = y[HALO:HALO + rows, :d]
    zx = y[:, d:2 * d] * y[:, 2 * d:]
    z = zx[HALO:HALO + rows]
    is_ctx = i < ctx_tiles
    pos = jnp.where(is_ctx, 0, (i - ctx_tiles) % tiles_per_seq)
    last = jnp.where(is_ctx, 0, tiles_per_seq - 1)
    z_before = jnp.where(pos == 0, 0.0, zx[HALO - 1:HALO])
    z_after = jnp.where(pos == last, 0.0, zx[HALO + rows:HALO + rows + 1])
    r = lax.broadcasted_iota(jnp.int32, (rows, 1), 0)
    in_seq = r & (CHUNK - 1)
    z_prev = jnp.where(r == 0, z_before, pltpu.roll(z, 1, 0))
    z_next = jnp.where(r == rows - 1, z_after, pltpu.roll(z, rows - 1, 0))
    z_prev = jnp.where(jnp.logical_and(is_ctx, in_seq == 0), 0.0, z_prev)
    z_next = jnp.where(jnp.logical_and(is_ctx, in_seq == CHUNK - 1), 0.0, z_next)
    conv = cw_ref[0:1, :] * z_prev + cw_ref[1:2, :] * z + cw_ref[2:3, :] * z_next + cb_ref[...]
    yo = _dot((gate_b * conv).astype(BF16), w_ref[...])
    _post_tail(yo, h, mod_ref, ln_ref, wr_ref, br_ref, h_out, t_out, lg_out, alpha)


def _conv_layer(lay, h, mod, w_in, cw, cb, w_out, ln, w_router, b_router, alpha):
    n, d = h.shape
    n_tiles = n // TM
    per = TM // HALO
    row = lambda i: (i, 0)
    return pl.pallas_call(
        functools.partial(_conv_kernel, alpha=alpha, tiles_per_seq=lay.seq // TM, ctx_tiles=lay.n_ctx // TM),
        grid=(n_tiles,),
        in_specs=[
            pl.BlockSpec((TM, d), row),
            pl.BlockSpec((HALO, d), lambda i: (jnp.maximum(i * per - 1, 0), 0)),
            pl.BlockSpec((HALO, d), lambda i: (jnp.minimum((i + 1) * per, n_tiles * per - 1), 0)),
            pl.BlockSpec((None, 6, d), lambda i: (lay.mod_row(i, TM), 0, 0)),
            _const_spec((d, 3 * d)), _const_spec((3, d)), _const_spec((1, d)), _const_spec((d, d)),
            _const_spec((2, d)), _const_spec((d, ROUTER_W)), _const_spec((1, ROUTER_W)),
        ],
        out_specs=[pl.BlockSpec((TM, d), row), pl.BlockSpec((TM, d), row), pl.BlockSpec((TM, ROUTER_W), row)],
        out_shape=[jax.ShapeDtypeStruct((n, d), F32), jax.ShapeDtypeStruct((n, d), BF16),
                   jax.ShapeDtypeStruct((n, ROUTER_W), F32)],
        compiler_params=_params("parallel"),
        name="conv_layer",
    )(h, h, h, mod, w_in.astype(BF16), cw.astype(F32), cb.reshape(1, d).astype(F32), w_out.astype(BF16),
      ln, w_router, b_router)


def _route_kernel(lg_ref, tri_ref, upper_ref, ext_ref, lp_ref, lpt_ref, tab_ref, cnt_ref, base_ref):
    @pl.when(pl.program_id(0) == 0)
    def _():
        base_ref[...] = jnp.zeros_like(base_ref)

    x = lg_ref[...]
    lane = lax.broadcasted_iota(jnp.int32, x.shape, 1).astype(F32)

    def softmax(mask):
        m = jnp.max(jnp.where(mask, x, -jnp.inf), axis=-1, keepdims=True)
        e = jnp.where(mask, jnp.exp(x - m), 0.0)
        return e / jnp.sum(e, axis=-1, keepdims=True)

    def top1(prob, mask):
        p = jnp.max(jnp.where(mask, prob, -1.0), axis=-1, keepdims=True)
        i = jnp.min(jnp.where(mask & (prob == p), lane, float(ROUTER_W)), axis=-1, keepdims=True)
        return p, i

    g_mask = lane < MOE_GROUPS
    g_p, g_idx = top1(softmax(g_mask), g_mask)
    e_lo = MOE_GROUPS + MOE_EPG * g_idx
    e_mask = (lane >= e_lo) & (lane < e_lo + MOE_EPG)
    e_prob = softmax(e_mask)
    p1, i1 = top1(e_prob, e_mask)
    p2, i2 = top1(e_prob, e_mask & (lane != i1))
    denom = p1 + p2
    gate1 = g_p * p1 / denom
    gate2 = g_p * p2 / denom

    sel1 = lane == i1
    sel2 = lane == i2
    cnt = (sel1 | sel2).astype(F32)
    within = _dot(tri_ref[...], cnt.astype(BF16))
    cnt_tile = jnp.sum(cnt, axis=0, keepdims=True)
    ci = jnp.broadcast_to(cnt_tile, (8, ROUTER_W)).astype(jnp.int32)
    ci = (ci + (RUN_ALIGN - 1)) & ~(RUN_ALIGN - 1)
    run_tile = ci[0:1].astype(F32)
    start = (_dot((ci >> 4).astype(F32).astype(BF16), upper_ref[...]) * 16.0
             + _dot((ci & 15).astype(F32).astype(BF16), upper_ref[...]))[0:1]
    where_to = within + start
    lp1 = jnp.sum(jnp.where(sel1, where_to, 0.0), axis=-1, keepdims=True)
    lp2 = jnp.sum(jnp.where(sel2, where_to, 0.0), axis=-1, keepdims=True)

    before = base_ref[...]
    total = before + run_tile
    base_ref[...] = total
    cnt_ref[...] = jnp.broadcast_to(total, cnt_ref.shape)
    row = lax.broadcasted_iota(jnp.int32, tab_ref.shape, 0)
    tab_ref[...] = jnp.where(row == 0, run_tile, jnp.where(row == 1, before, jnp.where(row == 2, start, 0.0)))

    def pieces(g):
        hi = g.astype(BF16).astype(F32)
        mid = (g - hi).astype(BF16).astype(F32)
        return hi, mid, g - hi - mid

    a1, b1, c1 = pieces(gate1)
    a2, b2, c2 = pieces(gate2)
    cols = (a1, b1, c1, a2, b2, c2, i1 - MOE_GROUPS)
    ext = jnp.zeros(x.shape, F32)
    for j, col in enumerate(cols):
        ext = jnp.where(lane == j, col, ext)
    ext_ref[...] = ext.astype(BF16)
    lp = jnp.where(lane == 0, lp1, jnp.where(lane == 1, lp2, 0.0))
    lp_ref[...] = lp
    lpt_ref[...] = lp.T[0:8, :]


def _route(logits):
    n = logits.shape[0]
    n_tiles = n // TOK_TILE
    r = lax.broadcasted_iota(jnp.int32, (TOK_TILE, TOK_TILE), 0)
    c = lax.broadcasted_iota(jnp.int32, (TOK_TILE, TOK_TILE), 1)
    tri = (c < r).astype(BF16)
    r = lax.broadcasted_iota(jnp.int32, (ROUTER_W, ROUTER_W), 0)
    c = lax.broadcasted_iota(jnp.int32, (ROUTER_W, ROUTER_W), 1)
    upper = (r < c).astype(BF16)
    row = lambda i: (i, 0)
    return pl.pallas_call(
        _route_kernel,
        grid=(n_tiles,),
        in_specs=[pl.BlockSpec((TOK_TILE, ROUTER_W), row), _const_spec((TOK_TILE, TOK_TILE)),
                  _const_spec((ROUTER_W, ROUTER_W))],
        out_specs=[pl.BlockSpec((TOK_TILE, ROUTER_W), row), pl.BlockSpec((TOK_TILE, ROUTER_W), row),
                   pl.BlockSpec((8, TOK_TILE), row), pl.BlockSpec((8, ROUTER_W), row),
                   pl.BlockSpec((8, ROUTER_W), lambda i: (0, 0))],
        out_shape=[jax.ShapeDtypeStruct((n, ROUTER_W), BF16), jax.ShapeDtypeStruct((n, ROUTER_W), F32),
                   jax.ShapeDtypeStruct((n_tiles * 8, TOK_TILE), F32),
                   jax.ShapeDtypeStruct((n_tiles * 8, ROUTER_W), F32),
                   jax.ShapeDtypeStruct((8, ROUTER_W), F32)],
        scratch_shapes=[pltpu.VMEM((1, ROUTER_W), F32)],
        compiler_params=_params("arbitrary"),
        name="moe_route",
    )(logits, tri, upper)


def _plan(tables, counts, n_blocks):
    n_tiles = tables.shape[0] // 8
    tables = tables.reshape(n_tiles, 8, ROUTER_W)[:, :3, MOE_GROUPS:MOE_GROUPS + MOE_EXPERTS].astype(jnp.int32)
    counts = counts[0, MOE_GROUPS:MOE_GROUPS + MOE_EXPERTS].astype(jnp.int32)
    padded = (counts + EXP_BLOCK - 1) // EXP_BLOCK * EXP_BLOCK
    pad_ends = jnp.cumsum(padded)
    pad_starts = pad_ends - padded
    run_len = tables[:, 0].reshape(-1)
    run_src = tables[:, 2].reshape(-1)
    run_dst = (tables[:, 1] + pad_starts[None, :]).reshape(-1)
    first_row = jnp.arange(n_blocks, dtype=jnp.int32) * EXP_BLOCK
    block_e = jnp.minimum(jnp.sum(first_row[:, None] >= pad_ends[None, :], axis=1),
                          MOE_EXPERTS - 1).astype(jnp.int32)
    n_valid = (pad_ends[-1] // EXP_BLOCK).astype(jnp.int32).reshape(1)
    clear_row = jnp.concatenate([pad_starts + counts, pad_ends[-1:]])
    clear_len = jnp.concatenate([padded - counts, n_blocks - n_valid])
    return run_len, run_src, run_dst, clear_row, clear_len, block_e, n_valid


RUN_BITS = tuple(b for b in (1 << k for k in range(TOK_TILE.bit_length() - 1, -1, -1)) if b >= RUN_ALIGN)
RARE_BITS = 4
SORT_ROWS = MOE_TOPK * TOK_TILE + MOE_EXPERTS * RUN_ALIGN


def _for_each_piece(length, src, dst, fn):
    def pieces(bits):
        for bit in bits:
            above = length & ~(2 * bit - 1)

            @pl.when((length & bit) != 0)
            def _(bit=bit, above=above):
                fn(pl.multiple_of(src + above, RUN_ALIGN), pl.multiple_of(dst + above, RUN_ALIGN), bit)

    @pl.when(length >= RUN_BITS[RARE_BITS - 1])
    def _():
        pieces(RUN_BITS[:RARE_BITS])
    pieces(RUN_BITS[RARE_BITS:])


def _dispatch_kernel(len_ref, src_ref, dst_ref, zrow_ref, zlen_ref, t_ref, ext_ref, lpt_ref, xs_ref,
                     buf, zeros, sem, zsem, *, n_tiles):
    i = pl.program_id(0)
    half = t_ref.shape[1] // 2

    @pl.when(i == 0)
    def _():
        zeros[...] = jnp.zeros_like(zeros)

        def clear(e, c):
            _for_each_piece(zlen_ref[e], 0, zrow_ref[e], lambda s, d, rows: pltpu.make_async_copy(
                zeros.at[pl.ds(0, rows)], xs_ref.at[pl.ds(d, rows)], zsem).start())
            return c
        lax.fori_loop(0, MOE_EXPERTS, clear, 0)

        def drain(e, c):
            _for_each_piece(zlen_ref[e], 0, zrow_ref[e], lambda s, d, rows: pltpu.make_async_copy(
                zeros.at[pl.ds(0, rows)], xs_ref.at[pl.ds(d, rows)], zsem).wait())
            return c
        lax.fori_loop(0, MOE_EXPERTS, drain, 0)

        def tail_copy(bk):
            row = pl.multiple_of(zrow_ref[MOE_EXPERTS] + bk * EXP_BLOCK, EXP_BLOCK)
            return pltpu.make_async_copy(zeros.at[pl.ds(0, EXP_BLOCK)], xs_ref.at[pl.ds(row, EXP_BLOCK)], zsem)

        def clear_tail(bk, c):
            tail_copy(bk).start()
            return c
        lax.fori_loop(0, zlen_ref[MOE_EXPERTS], clear_tail, 0)

        def drain_tail(bk, c):
            tail_copy(bk).wait()
            return c
        lax.fori_loop(0, zlen_ref[MOE_EXPERTS], drain_tail, 0)

    slot = i % 2

    def scatter(tile, sl, start):
        def body(e, c):
            k = tile * MOE_EXPERTS + e

            def piece(s, d, rows):
                cp = pltpu.make_async_copy(buf.at[sl, pl.ds(s, rows)], xs_ref.at[pl.ds(d, rows)], sem.at[sl])
                cp.start() if start else cp.wait()
            _for_each_piece(len_ref[k], src_ref[k], dst_ref[k], piece)
            return c
        lax.fori_loop(0, MOE_EXPERTS, body, 0)

    @pl.when(i >= 2)
    def _():
        scatter(i - 2, slot, False)

    lp = lpt_ref[...].astype(jnp.int32)
    j = lax.broadcasted_iota(jnp.int32, (SORT_ROWS, TOK_TILE), 0)
    perm = ((j == lp[0:1, :]) | (j == lp[1:2, :])).astype(F32).astype(BF16)
    rhs = jnp.concatenate([t_ref[...], ext_ref[...]], axis=1)
    srt = _dot(perm, rhs)
    bits = lax.bitcast_convert_type(srt, jnp.uint32)
    buf[slot, :, :half] = (bits[:, :half] >> 16) | (bits[:, half:2 * half] & jnp.uint32(0xFFFF0000))
    buf[slot, :, half:] = bits[:, 2 * half:]
    scatter(i, slot, True)

    @pl.when(i == n_tiles - 1)
    def _():
        if n_tiles > 1:
            scatter(i - 1, 1 - slot, False)
        scatter(i, slot, False)


def _expert_kernel(be_ref, nv_ref, x_ref, w1_ref, w3_ref, w2_ref, o_ref, w1b, w3b, w2b):
    i = pl.program_id(0)
    valid = i < nv_ref[0]
    half = o_ref.shape[1] // 2

    @pl.when(valid & ((i == 0) | (be_ref[i] != be_ref[jnp.maximum(i - 1, 0)])))
    def _():
        w1b[...] = w1_ref[...].astype(BF16)
        w3b[...] = w3_ref[...].astype(BF16)
        w2b[...] = w2_ref[...].astype(BF16)

    @pl.when(valid)
    def _():
        packed = x_ref[:, :half]
        lo = lax.bitcast_convert_type(packed << 16, F32)
        hi = lax.bitcast_convert_type(packed & jnp.uint32(0xFFFF0000), F32)
        x = jnp.concatenate([lo, hi], axis=1).astype(BF16)
        ext = lax.bitcast_convert_type(x_ref[:, half:], F32)
        g1 = ext[:, 0:1] + ext[:, 1:2] + ext[:, 2:3]
        g2 = ext[:, 3:4] + ext[:, 4:5] + ext[:, 5:6]
        gate = jnp.where(ext[:, 6:7] == be_ref[i].astype(F32), g1, g2)
        hdn = _silu(_dot(x, w1b[...])) * _dot(x, w3b[...])
        o_ref[...] = _dot(hdn.astype(BF16), w2b[...]) * gate

    @pl.when(jnp.logical_not(valid))
    def _():
        o_ref[...] = jnp.zeros_like(o_ref)


def _combine_kernel(len_ref, src_ref, dst_ref, ys_ref, lp_ref, h_ref, mod_ref, ln_ref, o_ref, buf, sem,
                    *, alpha, first_tile, n_steps):
    step = pl.program_id(0)
    tile = step + first_tile
    slot = step % 2

    def gather(tl, sl, start):
        def body(e, c):
            k = tl * MOE_EXPERTS + e

            def piece(s, d, rows):
                cp = pltpu.make_async_copy(ys_ref.at[pl.ds(d, rows)], buf.at[sl, pl.ds(s, rows)], sem.at[sl])
                cp.start() if start else cp.wait()
            _for_each_piece(len_ref[k], src_ref[k], dst_ref[k], piece)
            return c
        lax.fori_loop(0, MOE_EXPERTS, body, 0)

    @pl.when(step == 0)
    def _():
        buf[...] = jnp.zeros_like(buf)
        gather(tile, slot, True)

    @pl.when(step + 1 < n_steps)
    def _():
        gather(tile + 1, 1 - slot, True)
    gather(tile, slot, False)

    lp = lp_ref[...].astype(jnp.int32)
    j = lax.broadcasted_iota(jnp.int32, (TOK_TILE, SORT_ROWS), 1)
    pick = ((j == lp[:, 0:1]) | (j == lp[:, 1:2])).astype(F32).astype(BF16)
    y = buf[slot]
    hi = y.astype(BF16)
    r1 = y - hi.astype(F32)
    mid = r1.astype(BF16)
    lo = (r1 - mid.astype(F32)).astype(BF16)
    f = _dot(pick, hi) + _dot(pick, mid) + _dot(pick, lo)
    o_ref[...] = _layer_norm(alpha * h_ref[...] + mod_ref[5:6, :] * f, ln_ref[0:1, :], ln_ref[1:2, :])


def _moe(lay, t, logits, h, mod, ln, w1, w3, w2, layer, alpha, first_tile=0):
    n, d = t.shape
    hid = w1.shape[3]
    n_tiles = n // TOK_TILE
    max_rows = n * MOE_TOPK + n_tiles * MOE_EXPERTS * (RUN_ALIGN - 1)
    n_blocks = -(-max_rows // EXP_BLOCK) + MOE_EXPERTS
    n_rows = n_blocks * EXP_BLOCK
    xs_w = d // 2 + ROUTER_W
    ext, lp, lpt, tables, counts = _route(logits)
    run_len, run_src, run_dst, zrow, zlen, block_e, n_valid = _plan(tables, counts, n_blocks)

    xs = pl.pallas_call(
        functools.partial(_dispatch_kernel, n_tiles=n_tiles),
        grid_spec=pltpu.PrefetchScalarGridSpec(
            num_scalar_prefetch=5,
            grid=(n_tiles,),
            in_specs=[pl.BlockSpec((TOK_TILE, d), lambda i, *_: (i, 0)),
                      pl.BlockSpec((TOK_TILE, ROUTER_W), lambda i, *_: (i, 0)),
                      pl.BlockSpec((8, TOK_TILE), lambda i, *_: (i, 0))],
            out_specs=pl.BlockSpec(memory_space=pl.ANY),
            scratch_shapes=[pltpu.VMEM((2, SORT_ROWS, xs_w), jnp.uint32),
                            pltpu.VMEM((max(TOK_TILE, EXP_BLOCK), xs_w), jnp.uint32),
                            pltpu.SemaphoreType.DMA((2,)), pltpu.SemaphoreType.DMA],
        ),
        out_shape=jax.ShapeDtypeStruct((n_rows, xs_w), jnp.uint32),
        compiler_params=_params("arbitrary"),
        name="moe_dispatch",
    )(run_len, run_src, run_dst, zrow, zlen, t, ext, lpt)

    blk = lambda i, be, nv: (jnp.minimum(i, nv[0] - 1), 0)
    ys = pl.pallas_call(
        _expert_kernel,
        grid_spec=pltpu.PrefetchScalarGridSpec(
            num_scalar_prefetch=2,
            grid=(n_blocks,),
            in_specs=[
                pl.BlockSpec((EXP_BLOCK, xs_w), blk),
                pl.BlockSpec((None, None, d, hid), lambda i, be, nv: (layer, be[i], 0, 0)),
                pl.BlockSpec((None, None, d, hid), lambda i, be, nv: (layer, be[i], 0, 0)),
                pl.BlockSpec((None, None, hid, d), lambda i, be, nv: (layer, be[i], 0, 0)),
            ],
            out_specs=pl.BlockSpec((EXP_BLOCK, d), lambda i, be, nv: (i, 0)),
            scratch_shapes=[pltpu.VMEM((d, hid), BF16), pltpu.VMEM((d, hid), BF16), pltpu.VMEM((hid, d), BF16)],
        ),
        out_shape=jax.ShapeDtypeStruct((n_rows, d), F32),
        compiler_params=_params("arbitrary"),
        name="moe_experts",
    )(block_e, n_valid, xs, w1, w3, w2)

    n_steps = n_tiles - first_tile
    return pl.pallas_call(
        functools.partial(_combine_kernel, alpha=alpha, first_tile=first_tile, n_steps=n_steps),
        grid_spec=pltpu.PrefetchScalarGridSpec(
            num_scalar_prefetch=3,
            grid=(n_steps,),
            in_specs=[pl.BlockSpec(memory_space=pl.ANY),
                      pl.BlockSpec((TOK_TILE, ROUTER_W), lambda i, *_: (i + first_tile, 0)),
                      pl.BlockSpec((TOK_TILE, d), lambda i, *_: (i + first_tile, 0)),
                      pl.BlockSpec((None, 6, d), lambda i, *_: (lay.mod_row(i + first_tile, TOK_TILE), 0, 0)),
                      pl.BlockSpec((2, d), lambda i, *_: (0, 0))],
            out_specs=pl.BlockSpec((TOK_TILE, d), lambda i, *_: (i, 0)),
            scratch_shapes=[pltpu.VMEM((2, SORT_ROWS, d), F32), pltpu.SemaphoreType.DMA((2,))],
        ),
        out_shape=jax.ShapeDtypeStruct((n_steps * TOK_TILE, d), F32),
        compiler_params=_params("arbitrary"),
        name="moe_combine",
    )(run_len, run_src, run_dst, ys, lp, h, mod, ln)


def kernel(x, c, ctx, c_ctx, ada_w, ada_b, ln_g, ln_b, ret_w_in, ret_decay, ret_gn_g, ret_w_out, mla_w_down, mla_q_norm, mla_kv_norm, mla_w_uq, mla_w_ukv, mla_w_out, conv_w_in, conv_w, conv_b, conv_w_out, moe_w_group, moe_b_group, moe_w_expert, moe_b_expert, moe_w1, moe_w3, moe_w2):
    b, s, d = x.shape
    depth = ada_w.shape[0]
    lay = _Layout(b, s, ctx.shape[1])
    alpha = (2.0 * depth) ** 0.25

    mod_rows = -(-(b + 1) // 8) * 8
    cc = jnp.concatenate([c, c_ctx[None, :], jnp.zeros((mod_rows - b - 1, d), F32)], axis=0)
    mod_all = _ada(cc, ada_w, ada_b).reshape(depth, mod_rows, 6, d)

    h = (ctx.reshape(lay.n_ctx, d), x.reshape(b * s, d))
    for i in range(depth):
        kind, j = i % N_MIXERS, i // N_MIXERS
        mod = mod_all[i]
        ln1 = jnp.stack([ln_g[i, 0], ln_b[i, 0]])
        ln2 = jnp.stack([ln_g[i, 1], ln_b[i, 1]])
        pad = jnp.zeros((d, ROUTER_W - MOE_GROUPS - MOE_EXPERTS), F32)
        w_router = jnp.concatenate([moe_w_group[i], moe_w_expert[i], pad], axis=1).astype(BF16)
        b_router = jnp.concatenate([moe_b_group[i], moe_b_expert[i], pad[0]]).reshape(1, ROUTER_W)
        if kind == 0:
            a = _retention(lay, h, mod, ret_w_in[j], ret_decay[j], ret_gn_g[j], d)
            h1, t, logits = _post(lay, a, h, mod, ret_w_out[j], ln1, w_router, b_router, alpha)
        elif kind == 1:
            a = _mla(lay, h, mod, mla_w_down[j], mla_q_norm[j], mla_kv_norm[j], mla_w_uq[j], mla_w_ukv[j], d)
            h1, t, logits = _post(lay, a, h, mod, mla_w_out[j], ln1, w_router, b_router, alpha)
        else:
            h1, t, logits = _conv_layer(lay, h, mod, conv_w_in[j], conv_w[j], conv_b[j], conv_w_out[j],
                                        ln1, w_router, b_router, alpha)
        first_tile = lay.n_ctx // TOK_TILE if i == depth - 1 else 0
        h = _moe(lay, t, logits, h1, mod, ln2, moe_w1, moe_w3, moe_w2, i, alpha, first_tile)
    return h.reshape(b, s, d)
```

```python
import functools

import jax
import jax.numpy as jnp
from jax import lax
from jax.experimental import pallas as pl
from jax.experimental.pallas import tpu as pltpu

F32 = jnp.float32
BF16 = jnp.bfloat16

GRID_W = 64
LN_EPS = 1e-5
RMS_EPS = 1e-6
ROPE_BASE = 10000.0
N_MIXERS = 3
RET_HEADS = 4
MLA_HEADS = 8
MLA_NOPE = 128
MLA_ROPE = 64
MLA_V = 128
MLA_QK = 256
MOE_GROUPS = 4
MOE_EPG = 8
MOE_EXPERTS = MOE_GROUPS * MOE_EPG
MOE_TOPK = 2
ROUTER_W = 128

CHUNK = 256
TM = 512
TQ = 512
TOK_TILE = 512
EXP_BLOCK = 512
RUN_ALIGN = 8
VMEM_LIMIT = 56 * 1024 * 1024


def _params(*sem):
    return pltpu.CompilerParams(dimension_semantics=sem, vmem_limit_bytes=VMEM_LIMIT)


def _const_spec(shape):
    nd = len(shape)
    return pl.BlockSpec(shape, lambda *_: (0,) * nd, pipeline_mode=pl.Buffered(1))


def _dot(a, b):
    return jnp.dot(a, b, preferred_element_type=F32)


def _dot_nt(a, b):
    return lax.dot_general(a, b, (((1,), (1,)), ((), ())), preferred_element_type=F32)


def _dot_tn(a, b):
    return lax.dot_general(a, b, (((0,), (0,)), ((), ())), preferred_element_type=F32)


def _silu(x):
    return x * jax.nn.sigmoid(x)


def _layer_norm(x, g, b):
    mu = jnp.mean(x, axis=-1, keepdims=True)
    xc = x - mu
    var = jnp.mean(xc * xc, axis=-1, keepdims=True)
    return xc * lax.rsqrt(var + LN_EPS) * g + b


def _rms_norm(x, g):
    return x * lax.rsqrt(jnp.mean(x * x, axis=-1, keepdims=True) + RMS_EPS) * g


def _ada_kernel(c_ref, w_ref, b_ref, o_ref):
    a = _silu(c_ref[...]).astype(BF16)
    o_ref[...] = _dot(a, w_ref[...].astype(BF16)) + b_ref[...]


def _ada(cc, ada_w, ada_b):
    depth, d, n6 = ada_w.shape
    rows = cc.shape[0]
    tn = 1536
    return pl.pallas_call(
        _ada_kernel,
        grid=(depth, n6 // tn),
        in_specs=[
            pl.BlockSpec((rows, d), lambda l, j: (0, 0)),
            pl.BlockSpec((None, d, tn), lambda l, j: (l, 0, j)),
            pl.BlockSpec((None, 1, tn), lambda l, j: (l, 0, j)),
        ],
        out_specs=pl.BlockSpec((None, rows, tn), lambda l, j: (l, 0, j)),
        out_shape=jax.ShapeDtypeStruct((depth, rows, n6), F32),
        compiler_params=_params("parallel", "parallel"),
        name="ada_mod",
    )(cc, ada_w, ada_b.reshape(depth, 1, n6))


class _Layout:
    def __init__(self, batch, seq, ctx_len):
        assert ctx_len == CHUNK and seq % TM == 0 and seq % CHUNK == 0
        self.batch, self.seq, self.ctx = batch, seq, ctx_len
        self.n_ctx = batch * ctx_len
        self.n = self.n_ctx + batch * seq
        assert self.n_ctx % TM == 0 and self.n_ctx % seq == 0 and self.n_ctx % TOK_TILE == 0
        assert self.n_ctx % TQ == 0 and seq % TQ == 0 and seq % TOK_TILE == 0

    def mod_row(self, tile, rows_per_tile):
        ctx_tiles = self.n_ctx // rows_per_tile
        per_batch = self.seq // rows_per_tile
        return jnp.where(tile < ctx_tiles, self.batch, (tile - ctx_tiles) // per_batch)

    def pos_block(self, tile, rows_per_tile):
        ctx_tiles = self.n_ctx // rows_per_tile
        per_batch = self.seq // rows_per_tile
        return jnp.where(tile < ctx_tiles, 0, 1 + (tile - ctx_tiles) % per_batch)


def _row_specs(x, rows, ctx_tiles):
    if not isinstance(x, tuple):
        return (x,), [pl.BlockSpec((rows, x.shape[1]), lambda i: (i, 0))]
    width = x[0].shape[1]
    return x, [pl.BlockSpec((rows, width), lambda i: (jnp.minimum(i, ctx_tiles - 1), 0)),
               pl.BlockSpec((rows, width), lambda i: (jnp.maximum(i - ctx_tiles, 0), 0))]


def _row_tile(refs, ctx_tiles):
    if len(refs) == 1:
        return refs[0][...]
    return jnp.where(pl.program_id(0) < ctx_tiles, refs[0][...], refs[1][...])


def _rope_angles(seq, dim):
    n_rows = seq // GRID_W
    rows = jnp.repeat(jnp.arange(n_rows, dtype=F32), GRID_W)
    cols = jnp.tile(jnp.arange(GRID_W, dtype=F32), n_rows)
    quarter = dim // 4
    inv_freq = ROPE_BASE ** (-jnp.arange(quarter, dtype=F32) / quarter)
    return jnp.concatenate([rows[:, None] * inv_freq, cols[:, None] * inv_freq], axis=-1)


def _with_identity_block(table, fill, rows):
    ident = jnp.full((rows, table.shape[1]), fill, F32)
    return jnp.concatenate([ident, table], axis=0)


def _proj_ret_kernel(*refs, dk, ctx_tiles):
    h_refs = refs[:-14]
    (mod_ref, cos_ref, sin_ref, kdec_ref, wq_ref, wk_ref, wv_ref, wg_ref,
     q_ref, k_ref, kf_ref, kb_ref, v_ref, g_ref) = refs[-14:]
    u = (_row_tile(h_refs, ctx_tiles) * (1.0 + mod_ref[1:2, :]) + mod_ref[0:1, :]).astype(BF16)
    cos = cos_ref[...]
    sin = sin_ref[...]
    half = dk // 2
    yq = _dot(u, wq_ref[...])
    for hd in range(RET_HEADS):
        x1 = yq[:, hd * dk:hd * dk + half]
        x2 = yq[:, hd * dk + half:(hd + 1) * dk]
        q_ref[:, hd * dk:hd * dk + half] = (x1 * cos - x2 * sin).astype(BF16)
        q_ref[:, hd * dk + half:(hd + 1) * dk] = (x1 * sin + x2 * cos).astype(BF16)
    yk = _dot(u, wk_ref[...])
    k_scale = dk ** -0.5
    for hd in range(RET_HEADS):
        x1 = yk[:, hd * dk:hd * dk + half]
        x2 = yk[:, hd * dk + half:(hd + 1) * dk]
        df = kdec_ref[:, hd:hd + 1]
        db = kdec_ref[:, RET_HEADS + hd:RET_HEADS + hd + 1]
        for part, o in ((0, (x1 * cos - x2 * sin) * k_scale), (1, (x1 * sin + x2 * cos) * k_scale)):
            sl = slice(hd * dk + part * half, hd * dk + (part + 1) * half)
            k_ref[:, sl] = o.astype(BF16)
            kf_ref[:, sl] = (o * df).astype(BF16)
            kb_ref[:, sl] = (o * db).astype(BF16)
    v_ref[...] = _dot(u, wv_ref[...]).astype(BF16)
    g_ref[...] = _silu(_dot(u, wg_ref[...]))


def _ret_state_kernel(cdec_ref, kd_ref, v_ref, s_ref, state_ref, *, dk, dv):
    @pl.when(pl.program_id(1) == 0)
    def _():
        state_ref[...] = jnp.zeros_like(state_ref)

    for hd in range(RET_HEADS):
        st = state_ref[hd]
        s_ref[hd] = st.astype(BF16)
        kd = kd_ref[:, hd * dk:(hd + 1) * dk]
        v = v_ref[:, hd * dv:(hd + 1) * dv]
        state_ref[hd] = st * cdec_ref[RET_HEADS + hd] + _dot_tn(kd, v)


def _ret_out_kernel(cdec_ref, q_ref, k_ref, kd_ref, v_ref, sb_ref, mask_ref, qdec_ref, g_ref, gn_ref, o_ref,
                    state_ref, *, dk, dv):
    @pl.when(pl.program_id(1) == 0)
    def _():
        state_ref[...] = jnp.zeros_like(state_ref)

    for hd in range(RET_HEADS):
        q = q_ref[:, hd * dk:(hd + 1) * dk]
        k = k_ref[:, hd * dk:(hd + 1) * dk]
        kd = kd_ref[:, hd * dk:(hd + 1) * dk]
        v = v_ref[:, hd * dv:(hd + 1) * dv]
        p = (_dot_nt(q, k) * mask_ref[hd]).astype(BF16)
        st = state_ref[hd]
        o = (_dot(p, v) + qdec_ref[:, hd:hd + 1] * _dot(q, st.astype(BF16))
             + qdec_ref[:, RET_HEADS + hd:RET_HEADS + hd + 1] * _dot(q, sb_ref[hd]))
        state_ref[hd] = st * cdec_ref[hd] + _dot_tn(kd, v)
        mu = jnp.mean(o, axis=-1, keepdims=True)
        oc = o - mu
        var = jnp.mean(oc * oc, axis=-1, keepdims=True)
        on = oc * lax.rsqrt(var + LN_EPS) * gn_ref[:, hd * dv:(hd + 1) * dv]
        o_ref[:, hd * dv:(hd + 1) * dv] = (g_ref[:, hd * dv:(hd + 1) * dv] * on).astype(BF16)


def _retention(lay, h, mod, w_in, decay_logit, gn_g, d_model):
    n, b, nc = lay.n, lay.batch, lay.seq // CHUNK
    dk = d_model // RET_HEADS
    dv = 2 * dk
    qk, vw = RET_HEADS * dk, RET_HEADS * dv
    w_in = w_in.astype(BF16)
    wq, wk, wv, wg = w_in[:, :qk], w_in[:, qk:2 * qk], w_in[:, 2 * qk:2 * qk + vw], w_in[:, 2 * qk + vw:]

    ang = _rope_angles(lay.seq, dk)
    cos_t = _with_identity_block(jnp.cos(ang), 1.0, TM)
    sin_t = _with_identity_block(jnp.sin(ang), 0.0, TM)

    lg = jax.nn.log_sigmoid(decay_logit.astype(F32))
    idx = jnp.arange(CHUNK, dtype=F32)
    k_pow = jnp.stack([CHUNK - 1.0 - idx, idx])
    q_pow = jnp.stack([idx + 1.0, CHUNK - idx])
    kdec = jnp.exp(k_pow[:, :, None] * lg[:, None, :])
    qdec = jnp.exp(q_pow[:, :, None] * lg[:, None, :])
    kdec = jnp.moveaxis(kdec, 0, 1).reshape(CHUNK, 2 * RET_HEADS)
    qdec = jnp.moveaxis(qdec, 0, 1).reshape(CHUNK, 2 * RET_HEADS)
    cdec = jnp.exp(CHUNK * lg).reshape(2 * RET_HEADS)
    rel = idx[:, None] - idx[None, :]
    rel = jnp.stack([rel, -rel])
    intra = jnp.where(rel[:, None] >= 0, jnp.exp(jnp.maximum(rel[:, None], 0.0) * lg[:, :, None, None]), 0.0)
    kdec_tm = jnp.tile(kdec, (TM // CHUNK, 1))

    n_tiles = n // TM
    row = lambda i: (i, 0)
    ctx_tiles = lay.n_ctx // TM
    h, h_specs = _row_specs(h, TM, ctx_tiles)
    q, k, kf, kb, v, g = pl.pallas_call(
        functools.partial(_proj_ret_kernel, dk=dk, ctx_tiles=ctx_tiles),
        grid=(n_tiles,),
        in_specs=h_specs + [
            pl.BlockSpec((None, 6, d_model), lambda i: (lay.mod_row(i, TM), 0, 0)),
            pl.BlockSpec((TM, dk // 2), lambda i: (lay.pos_block(i, TM), 0)),
            pl.BlockSpec((TM, dk // 2), lambda i: (lay.pos_block(i, TM), 0)),
            _const_spec((TM, 2 * RET_HEADS)),
            _const_spec((d_model, qk)), _const_spec((d_model, qk)),
            _const_spec((d_model, vw)), _const_spec((d_model, vw)),
        ],
        out_specs=[pl.BlockSpec((TM, qk), row)] * 4 + [pl.BlockSpec((TM, vw), row)] * 2,
        out_shape=[jax.ShapeDtypeStruct((n, qk), BF16)] * 4
        + [jax.ShapeDtypeStruct((n, vw), BF16), jax.ShapeDtypeStruct((n, vw), F32)],
        compiler_params=_params("parallel"),
        name="ret_proj",
    )(*h, mod, cos_t, sin_t, kdec_tm, wq, wk, wv, wg)

    def chunk_fwd(bi, c):
        return (jnp.where(c == 0, bi, b + bi * nc + c - 1), 0)

    def chunk_bwd(bi, c):
        return (jnp.where(c == 0, bi, b + bi * nc + nc - c), 0)

    n_chunks = n // CHUNK
    s_b = pl.pallas_call(
        functools.partial(_ret_state_kernel, dk=dk, dv=dv),
        grid=(b, nc + 1),
        in_specs=[pl.BlockSpec(memory_space=pltpu.SMEM),
                  pl.BlockSpec((CHUNK, qk), chunk_bwd), pl.BlockSpec((CHUNK, vw), chunk_bwd)],
        out_specs=pl.BlockSpec((None, RET_HEADS, dk, dv), lambda bi, c: (chunk_bwd(bi, c)[0], 0, 0, 0)),
        out_shape=jax.ShapeDtypeStruct((n_chunks, RET_HEADS, dk, dv), BF16),
        scratch_shapes=[pltpu.VMEM((RET_HEADS, dk, dv), F32)],
        compiler_params=_params("parallel", "arbitrary"),
        name="ret_state_bwd",
    )(cdec, kb, v)

    return pl.pallas_call(
        functools.partial(_ret_out_kernel, dk=dk, dv=dv),
        grid=(b, nc + 1),
        in_specs=[
            pl.BlockSpec(memory_space=pltpu.SMEM),
            pl.BlockSpec((CHUNK, qk), chunk_fwd), pl.BlockSpec((CHUNK, qk), chunk_fwd),
            pl.BlockSpec((CHUNK, qk), chunk_fwd), pl.BlockSpec((CHUNK, vw), chunk_fwd),
            pl.BlockSpec((None, RET_HEADS, dk, dv), lambda bi, c: (chunk_fwd(bi, c)[0], 0, 0, 0)),
            _const_spec((RET_HEADS, CHUNK, CHUNK)), _const_spec((CHUNK, 2 * RET_HEADS)),
            pl.BlockSpec((CHUNK, vw), chunk_fwd), _const_spec((1, vw)),
        ],
        out_specs=pl.BlockSpec((CHUNK, vw), chunk_fwd),
        out_shape=jax.ShapeDtypeStruct((n, vw), BF16),
        scratch_shapes=[pltpu.VMEM((RET_HEADS, dk, dv), F32)],
        compiler_params=_params("parallel", "arbitrary"),
        name="ret_scan_out",
    )(cdec, q, k, kf, v, s_b, intra[0] + intra[1], qdec, g, gn_g.reshape(1, vw).astype(F32))


def _proj_mla_kernel(h_ref, mod_ref, ct_ref, st_ref, wdq_ref, wdkv_ref, wkr_ref, qn_ref, kvn_ref,
                     wuq_ref, wukn_ref, wuv_ref, q_ref, k_ref, vt_ref):
    u = (h_ref[...] * (1.0 + mod_ref[1:2, :]) + mod_ref[0:1, :]).astype(BF16)
    ct = ct_ref[...]
    st = st_ref[...]
    half = MLA_QK // 2

    def rope(x):
        return x * ct + pltpu.roll(x, half // 2, 1) * st

    cq = _rms_norm(_dot(u, wdq_ref[...]), qn_ref[...]).astype(BF16)
    yq = _dot(cq, wuq_ref[...])
    ckv = _rms_norm(_dot(u, wdkv_ref[...]), kvn_ref[...]).astype(BF16)
    kn = _dot(ckv, wukn_ref[...])
    kr = rope(_dot(u, wkr_ref[...])).astype(BF16)
    for hd in range(MLA_HEADS):
        q_ref[:, hd * MLA_QK:hd * MLA_QK + half] = yq[:, hd * MLA_QK:hd * MLA_QK + half].astype(BF16)
        q_ref[:, hd * MLA_QK + half:(hd + 1) * MLA_QK] = rope(
            yq[:, hd * MLA_QK + half:(hd + 1) * MLA_QK]).astype(BF16)
        k_ref[:, hd * MLA_QK:hd * MLA_QK + half] = kn[:, hd * half:(hd + 1) * half].astype(BF16)
        k_ref[:, hd * MLA_QK + half:(hd + 1) * MLA_QK] = kr
    vt_ref[...] = _dot(ckv, wuv_ref[...]).T.astype(BF16)


KV_CHUNK = 512
LOG2_E = 1.4426950408889634


def _kv_chunks(kv_refs, n_kv):
    chunks = []
    row = 0
    for j in range(n_kv):
        keys = kv_refs[2 * j].shape[0]
        step = min(keys, KV_CHUNK)
        for off in range(0, keys, step):
            chunks.append((j, off, row, step))
            row += step
    return chunks


def _attn_kernel(q_ref, *rest, scale, n_kv, pipelined):
    kv_refs = rest[:2 * n_kv]
    chunks = _kv_chunks(kv_refs, n_kv)

    def scores(c, s_ref, m):
        j, off, row, w = c
        s = _dot_nt(kv_refs[2 * j][off:off + w, :], q_ref[...])
        s_ref[row:row + w, :] = s
        cm = jnp.max(s, axis=0, keepdims=True)
        return cm if m is None else jnp.maximum(m, cm)

    def values(c, s_ref, m, acc, den):
        j, off, row, w = c
        p = jnp.exp2((s_ref[row:row + w, :] - m) * (scale * LOG2_E))
        cs = jnp.sum(p, axis=0, keepdims=True)
        pv = _dot(kv_refs[2 * j + 1][:, off:off + w], p.astype(BF16))
        return (pv if acc is None else acc + pv), (cs if den is None else den + cs)

    if not pipelined:
        o_ref, s_ref = rest[-2:]
        m = acc = den = None
        for c in chunks:
            m = scores(c, s_ref, m)
        for c in chunks:
            acc, den = values(c, s_ref, m, acc, den)
        o_ref[...] = (acc / den).T.astype(BF16)
        return

    o_ref, s0_ref, s1_ref, m0_ref, m1_ref = rest[-5:]
    t = pl.program_id(0)
    last = pl.num_programs(0) - 1
    bufs = ((s0_ref, m0_ref), (s1_ref, m1_ref))

    def step(new, old):
        m_prev = old[1][...] if old else None
        m = acc = den = None
        for c in chunks:
            if new:
                m = scores(c, new[0], m)
            if old:
                acc, den = values(c, old[0], m_prev, acc, den)
        if new:
            new[1][...] = m
        if old:
            o_ref[...] = (acc / den).T.astype(BF16)

    @pl.when(t == 0)
    def _():
        step(bufs[0], None)

    for parity in (0, 1):
        @pl.when((t > 0) & (t < last) & (t % 2 == parity))
        def _(parity=parity):
            step(bufs[parity], bufs[1 - parity])

        @pl.when((t == last) & (t % 2 == parity))
        def _(parity=parity):
            step(None, bufs[1 - parity])


def _pad_rope_cols(w):
    z = jnp.zeros((w.shape[0], MLA_ROPE // 2), w.dtype)
    return jnp.concatenate([w[:, :MLA_ROPE // 2], z, w[:, MLA_ROPE // 2:], z], axis=1)


def _mla(lay, h, mod, w_down, q_norm, kv_norm, w_uq, w_ukv, d_model):
    n, b, s = lay.n, lay.batch, lay.seq
    q_lora, kv_lora = q_norm.shape[0], kv_norm.shape[0]
    w_dq = w_down[:, :q_lora].astype(BF16)
    w_dkv = w_down[:, q_lora:q_lora + kv_lora].astype(BF16)
    w_kr = _pad_rope_cols(w_down[:, q_lora + kv_lora:]).astype(BF16)
    w_uq = w_uq.reshape(q_lora, MLA_HEADS, MLA_NOPE + MLA_ROPE)
    w_uq = jnp.concatenate(
        [w_uq[:, :, :MLA_NOPE], jax.vmap(_pad_rope_cols, 1, 1)(w_uq[:, :, MLA_NOPE:])], axis=2)
    w_uq = w_uq.reshape(q_lora, MLA_HEADS * MLA_QK).astype(BF16)
    w_ukv = w_ukv.reshape(kv_lora, MLA_HEADS, MLA_NOPE + MLA_V)
    w_ukn = w_ukv[:, :, :MLA_NOPE].reshape(kv_lora, MLA_HEADS * MLA_NOPE).astype(BF16)
    w_uv = w_ukv[:, :, MLA_NOPE:].reshape(kv_lora, MLA_HEADS * MLA_V).astype(BF16)

    ang = _rope_angles(s, MLA_ROPE)
    z = jnp.zeros_like(ang)
    ct = _with_identity_block(jnp.concatenate([jnp.cos(ang), z, jnp.cos(ang), z], axis=1), 1.0, TM)
    st = _with_identity_block(jnp.concatenate([-jnp.sin(ang), z, jnp.sin(ang), z], axis=1), 0.0, TM)

    row = lambda i: (i, 0)
    qw, vw = MLA_HEADS * MLA_QK, MLA_HEADS * MLA_V
    q, k, vt = pl.pallas_call(
        _proj_mla_kernel,
        grid=(n // TM,),
        in_specs=[
            pl.BlockSpec((TM, d_model), row),
            pl.BlockSpec((None, 6, d_model), lambda i: (lay.mod_row(i, TM), 0, 0)),
            pl.BlockSpec((TM, MLA_QK // 2), lambda i: (lay.pos_block(i, TM), 0)),
            pl.BlockSpec((TM, MLA_QK // 2), lambda i: (lay.pos_block(i, TM), 0)),
            _const_spec((d_model, q_lora)), _const_spec((d_model, kv_lora)), _const_spec((d_model, MLA_QK // 2)),
            _const_spec((1, q_lora)), _const_spec((1, kv_lora)),
            _const_spec((q_lora, qw)), _const_spec((kv_lora, vw)), _const_spec((kv_lora, vw)),
        ],
        out_specs=[pl.BlockSpec((TM, qw), row), pl.BlockSpec((TM, qw), row),
                   pl.BlockSpec((vw, TM), lambda i: (0, i))],
        out_shape=[jax.ShapeDtypeStruct((n, qw), BF16), jax.ShapeDtypeStruct((n, qw), BF16),
                   jax.ShapeDtypeStruct((vw, n), BF16)],
        compiler_params=_params("parallel"),
        name="mla_proj",
    )(h, mod, ct, st, w_dq, w_dkv, w_kr, q_norm.reshape(1, -1).astype(F32), kv_norm.reshape(1, -1).astype(F32),
      w_uq, w_ukn, w_uv)

    scale = (MLA_NOPE + MLA_ROPE) ** -0.5
    ctx_q = lambda bi, hd: (bi, hd)
    att_ctx = pl.pallas_call(
        functools.partial(_attn_kernel, scale=scale, n_kv=1, pipelined=False),
        grid=(b, MLA_HEADS),
        in_specs=[pl.BlockSpec((CHUNK, MLA_QK), ctx_q), pl.BlockSpec((CHUNK, MLA_QK), ctx_q),
                  pl.BlockSpec((MLA_V, CHUNK), lambda bi, hd: (hd, bi))],
        out_specs=pl.BlockSpec((CHUNK, MLA_V), ctx_q),
        out_shape=jax.ShapeDtypeStruct((lay.n_ctx, vw), BF16),
        scratch_shapes=[pltpu.VMEM((CHUNK, CHUNK), F32)],
        compiler_params=_params("parallel", "parallel"),
        name="mla_attn_ctx",
    )(q, k, vt)

    lat_blk = lay.n_ctx // s
    q_tiles = s // TQ
    n_work = b * MLA_HEADS * q_tiles

    def work(f):
        bh, t = f // q_tiles, f % q_tiles
        return bh // MLA_HEADS, bh % MLA_HEADS, t

    def scored(f):
        return work(jnp.minimum(f, n_work - 1))

    def finished(f):
        return work(jnp.maximum(f - 1, 0))

    def q_map(f):
        bi, hd, t = scored(f)
        return (lay.n_ctx // TQ + bi * q_tiles + t, hd)

    def out_map(f):
        bi, hd, t = finished(f)
        return (bi * q_tiles + t, hd)

    att_lat = pl.pallas_call(
        functools.partial(_attn_kernel, scale=scale, n_kv=2, pipelined=True),
        grid=(n_work + 1,),
        in_specs=[pl.BlockSpec((TQ, MLA_QK), q_map),
                  pl.BlockSpec((CHUNK, MLA_QK), lambda f: scored(f)[:2]),
                  pl.BlockSpec((MLA_V, CHUNK), lambda f: (finished(f)[1], finished(f)[0])),
                  pl.BlockSpec((s, MLA_QK), lambda f: (lat_blk + scored(f)[0], scored(f)[1])),
                  pl.BlockSpec((MLA_V, s), lambda f: (finished(f)[1], lat_blk + finished(f)[0]))],
        out_specs=pl.BlockSpec((TQ, MLA_V), out_map),
        out_shape=jax.ShapeDtypeStruct((n - lay.n_ctx, vw), BF16),
        scratch_shapes=[pltpu.VMEM((CHUNK + s, TQ), F32)] * 2 + [pltpu.VMEM((1, TQ), F32)] * 2,
        compiler_params=_params("arbitrary"),
        name="mla_attn_lat",
    )(q, k, vt, k, vt)
    return att_ctx, att_lat


def _post_tail(y, h, mod_ref, ln_ref, wr_ref, br_ref, h_out, t_out, lg_out, alpha):
    h1 = _layer_norm(alpha * h + mod_ref[2:3, :] * y, ln_ref[0:1, :], ln_ref[1:2, :])
    t = h1 * (1.0 + mod_ref[4:5, :]) + mod_ref[3:4, :]
    t = t.astype(BF16)
    h_out[...] = h1
    t_out[...] = t
    lg_out[...] = _dot(t, wr_ref[...]) + br_ref[...]


def _post_kernel(*refs, alpha, ctx_tiles, n_a):
    a_refs, h_refs = refs[:n_a], refs[n_a:-8]
    mod_ref, w_ref, ln_ref, wr_ref, br_ref, h_out, t_out, lg_out = refs[-8:]
    y = _dot(_row_tile(a_refs, ctx_tiles), w_ref[...])
    _post_tail(y, _row_tile(h_refs, ctx_tiles), mod_ref, ln_ref, wr_ref, br_ref, h_out, t_out, lg_out, alpha)


def _post(lay, a, h, mod, w_out, ln, w_router, b_router, alpha):
    n, d = lay.n, w_out.shape[1]
    row = lambda i: (i, 0)
    ctx_tiles = lay.n_ctx // TM
    a, a_specs = _row_specs(a, TM, ctx_tiles)
    h, h_specs = _row_specs(h, TM, ctx_tiles)
    ka = a[0].shape[1]
    return pl.pallas_call(
        functools.partial(_post_kernel, alpha=alpha, ctx_tiles=ctx_tiles, n_a=len(a)),
        grid=(n // TM,),
        in_specs=a_specs + h_specs + [
            pl.BlockSpec((None, 6, d), lambda i: (lay.mod_row(i, TM), 0, 0)),
            _const_spec((ka, d)), _const_spec((2, d)), _const_spec((d, ROUTER_W)), _const_spec((1, ROUTER_W)),
        ],
        out_specs=[pl.BlockSpec((TM, d), row), pl.BlockSpec((TM, d), row), pl.BlockSpec((TM, ROUTER_W), row)],
        out_shape=[jax.ShapeDtypeStruct((n, d), F32), jax.ShapeDtypeStruct((n, d), BF16),
                   jax.ShapeDtypeStruct((n, ROUTER_W), F32)],
        compiler_params=_params("parallel"),
        name="post_mixer",
    )(*a, *h, mod, w_out.astype(BF16), ln, w_router, b_router)


HALO = 8


def _conv_kernel(h_ref, hp_ref, hn_ref, mod_ref, win_ref, cw_ref, cb_ref, w_ref, ln_ref, wr_ref, br_ref,
                 h_out, t_out, lg_out, *, alpha, tiles_per_seq, ctx_tiles):
    i = pl.program_id(0)
    rows, d = h_ref.shape
    h = h_ref[...]
    hx = jnp.concatenate([hp_ref[...], h, hn_ref[...]], axis=0)
    u = (hx * (1.0 + mod_ref[1:2, :]) + mod_ref[0:1, :]).astype(BF16)
    y = _dot(u, win_ref[...])
    gate_b = y[HALO:HALO + rows, :d]
    zx = y[:, d:2 * d] * y[:, 2 * d:]
    z = zx[HALO:HALO + rows]
    is_ctx = i < ctx_tiles
    pos = jnp.where(is_ctx, 0, (i - ctx_tiles) % tiles_per_seq)
    last = jnp.where(is_ctx, 0, tiles_per_seq - 1)
    z_before = jnp.where(pos == 0, 0.0, zx[HALO - 1:HALO])
    z_after = jnp.where(pos == last, 0.0, zx[HALO + rows:HALO + rows + 1])
    r = lax.broadcasted_iota(jnp.int32, (rows, 1), 0)
    in_seq = r & (CHUNK - 1)
    z_prev = jnp.where(r == 0, z_before, pltpu.roll(z, 1, 0))
    z_next = jnp.where(r == rows - 1, z_after, pltpu.roll(z, rows - 1, 0))
    z_prev = jnp.where(jnp.logical_and(is_ctx, in_seq == 0), 0.0, z_prev)
    z_next = jnp.where(jnp.logical_and(is_ctx, in_seq == CHUNK - 1), 0.0, z_next)
    conv = cw_ref[0:1, :] * z_prev + cw_ref[1:2, :] * z + cw_ref[2:3, :] * z_next + cb_ref[...]
    yo = _dot((gate_b * conv).astype(BF16), w_ref[...])
    _post_tail(yo, h, mod_ref, ln_ref, wr_ref, br_ref, h_out, t_out, lg_out, alpha)


def _conv_layer(lay, h, mod, w_in, cw, cb, w_out, ln, w_router, b_router, alpha):
    n, d = h.shape
    n_tiles = n // TM
    per = TM // HALO
    row = lambda i: (i, 0)
    return pl.pallas_call(
        functools.partial(_conv_kernel, alpha=alpha, tiles_per_seq=lay.seq // TM, ctx_tiles=lay.n_ctx // TM),
        grid=(n_tiles,),
        in_specs=[
            pl.BlockSpec((TM, d), row),
            pl.BlockSpec((HALO, d), lambda i: (jnp.maximum(i * per - 1, 0), 0)),
            pl.BlockSpec((HALO, d), lambda i: (jnp.minimum((i + 1) * per, n_tiles * per - 1), 0)),
            pl.BlockSpec((None, 6, d), lambda i: (lay.mod_row(i, TM), 0, 0)),
            _const_spec((d, 3 * d)), _const_spec((3, d)), _const_spec((1, d)), _const_spec((d, d)),
            _const_spec((2, d)), _const_spec((d, ROUTER_W)), _const_spec((1, ROUTER_W)),
        ],
        out_specs=[pl.BlockSpec((TM, d), row), pl.BlockSpec((TM, d), row), pl.BlockSpec((TM, ROUTER_W), row)],
        out_shape=[jax.ShapeDtypeStruct((n, d), F32), jax.ShapeDtypeStruct((n, d), BF16),
                   jax.ShapeDtypeStruct((n, ROUTER_W), F32)],
        compiler_params=_params("parallel"),
        name="conv_layer",
    )(h, h, h, mod, w_in.astype(BF16), cw.astype(F32), cb.reshape(1, d).astype(F32), w_out.astype(BF16),
      ln, w_router, b_router)


def _route_kernel(lg_ref, tri_ref, upper_ref, ext_ref, lp_ref, lpt_ref, tab_ref, cnt_ref, base_ref):
    @pl.when(pl.program_id(0) == 0)
    def _():
        base_ref[...] = jnp.zeros_like(base_ref)

    x = lg_ref[...]
    lane = lax.broadcasted_iota(jnp.int32, x.shape, 1).astype(F32)

    def softmax(mask):
        m = jnp.max(jnp.where(mask, x, -jnp.inf), axis=-1, keepdims=True)
        e = jnp.where(mask, jnp.exp(x - m), 0.0)
        return e / jnp.sum(e, axis=-1, keepdims=True)

    def top1(prob, mask):
        p = jnp.max(jnp.where(mask, prob, -1.0), axis=-1, keepdims=True)
        i = jnp.min(jnp.where(mask & (prob == p), lane, float(ROUTER_W)), axis=-1, keepdims=True)
        return p, i

    g_mask = lane < MOE_GROUPS
    g_p, g_idx = top1(softmax(g_mask), g_mask)
    e_lo = MOE_GROUPS + MOE_EPG * g_idx
    e_mask = (lane >= e_lo) & (lane < e_lo + MOE_EPG)
    e_prob = softmax(e_mask)
    p1, i1 = top1(e_prob, e_mask)
    p2, i2 = top1(e_prob, e_mask & (lane != i1))
    denom = p1 + p2
    gate1 = g_p * p1 / denom
    gate2 = g_p * p2 / denom

    sel1 = lane == i1
    sel2 = lane == i2
    cnt = (sel1 | sel2).astype(F32)
    within = _dot(tri_ref[...], cnt.astype(BF16))
    cnt_tile = jnp.sum(cnt, axis=0, keepdims=True)
    ci = jnp.broadcast_to(cnt_tile, (8, ROUTER_W)).astype(jnp.int32)
    ci = (ci + (RUN_ALIGN - 1)) & ~(RUN_ALIGN - 1)
    run_tile = ci[0:1].astype(F32)
    start = (_dot((ci >> 4).astype(F32).astype(BF16), upper_ref[...]) * 16.0
             + _dot((ci & 15).astype(F32).astype(BF16), upper_ref[...]))[0:1]
    where_to = within + start
    lp1 = jnp.sum(jnp.where(sel1, where_to, 0.0), axis=-1, keepdims=True)
    lp2 = jnp.sum(jnp.where(sel2, where_to, 0.0), axis=-1, keepdims=True)

    before = base_ref[...]
    total = before + run_tile
    base_ref[...] = total
    cnt_ref[...] = jnp.broadcast_to(total, cnt_ref.shape)
    row = lax.broadcasted_iota(jnp.int32, tab_ref.shape, 0)
    tab_ref[...] = jnp.where(row == 0, run_tile, jnp.where(row == 1, before, jnp.where(row == 2, start, 0.0)))

    def pieces(g):
        hi = g.astype(BF16).astype(F32)
        mid = (g - hi).astype(BF16).astype(F32)
        return hi, mid, g - hi - mid

    a1, b1, c1 = pieces(gate1)
    a2, b2, c2 = pieces(gate2)
    cols = (a1, b1, c1, a2, b2, c2, i1 - MOE_GROUPS)
    ext = jnp.zeros(x.shape, F32)
    for j, col in enumerate(cols):
        ext = jnp.where(lane == j, col, ext)
    ext_ref[...] = ext.astype(BF16)
    lp = jnp.where(lane == 0, lp1, jnp.where(lane == 1, lp2, 0.0))
    lp_ref[...] = lp
    lpt_ref[...] = lp.T[0:8, :]


def _route(logits):
    n = logits.shape[0]
    n_tiles = n // TOK_TILE
    r = lax.broadcasted_iota(jnp.int32, (TOK_TILE, TOK_TILE), 0)
    c = lax.broadcasted_iota(jnp.int32, (TOK_TILE, TOK_TILE), 1)
    tri = (c < r).astype(BF16)
    r = lax.broadcasted_iota(jnp.int32, (ROUTER_W, ROUTER_W), 0)
    c = lax.broadcasted_iota(jnp.int32, (ROUTER_W, ROUTER_W), 1)
    upper = (r < c).astype(BF16)
    row = lambda i: (i, 0)
    return pl.pallas_call(
        _route_kernel,
        grid=(n_tiles,),
        in_specs=[pl.BlockSpec((TOK_TILE, ROUTER_W), row), _const_spec((TOK_TILE, TOK_TILE)),
                  _const_spec((ROUTER_W, ROUTER_W))],
        out_specs=[pl.BlockSpec((TOK_TILE, ROUTER_W), row), pl.BlockSpec((TOK_TILE, ROUTER_W), row),
                   pl.BlockSpec((8, TOK_TILE), row), pl.BlockSpec((8, ROUTER_W), row),
                   pl.BlockSpec((8, ROUTER_W), lambda i: (0, 0))],
        out_shape=[jax.ShapeDtypeStruct((n, ROUTER_W), BF16), jax.ShapeDtypeStruct((n, ROUTER_W), F32),
                   jax.ShapeDtypeStruct((n_tiles * 8, TOK_TILE), F32),
                   jax.ShapeDtypeStruct((n_tiles * 8, ROUTER_W), F32),
                   jax.ShapeDtypeStruct((8, ROUTER_W), F32)],
        scratch_shapes=[pltpu.VMEM((1, ROUTER_W), F32)],
        compiler_params=_params("arbitrary"),
        name="moe_route",
    )(logits, tri, upper)


def _plan(tables, counts, n_blocks):
    n_tiles = tables.shape[0] // 8
    tables = tables.reshape(n_tiles, 8, ROUTER_W)[:, :3, MOE_GROUPS:MOE_GROUPS + MOE_EXPERTS].astype(jnp.int32)
    counts = counts[0, MOE_GROUPS:MOE_GROUPS + MOE_EXPERTS].astype(jnp.int32)
    padded = (counts + EXP_BLOCK - 1) // EXP_BLOCK * EXP_BLOCK
    pad_ends = jnp.cumsum(padded)
    pad_starts = pad_ends - padded
    run_len = tables[:, 0].reshape(-1)
    run_src = tables[:, 2].reshape(-1)
    run_dst = (tables[:, 1] + pad_starts[None, :]).reshape(-1)
    first_row = jnp.arange(n_blocks, dtype=jnp.int32) * EXP_BLOCK
    block_e = jnp.minimum(jnp.sum(first_row[:, None] >= pad_ends[None, :], axis=1),
                          MOE_EXPERTS - 1).astype(jnp.int32)
    n_valid = (pad_ends[-1] // EXP_BLOCK).astype(jnp.int32).reshape(1)
    clear_row = jnp.concatenate([pad_starts + counts, pad_ends[-1:]])
    clear_len = jnp.concatenate([padded - counts, n_blocks - n_valid])
    return run_len, run_src, run_dst, clear_row, clear_len, block_e, n_valid


RUN_BITS = tuple(b for b in (1 << k for k in range(TOK_TILE.bit_length() - 1, -1, -1)) if b >= RUN_ALIGN)
RARE_BITS = 4
SORT_ROWS = MOE_TOPK * TOK_TILE + MOE_EXPERTS * RUN_ALIGN


def _for_each_piece(length, src, dst, fn):
    def pieces(bits):
        for bit in bits:
            above = length & ~(2 * bit - 1)

            @pl.when((length & bit) != 0)
            def _(bit=bit, above=above):
                fn(pl.multiple_of(src + above, RUN_ALIGN), pl.multiple_of(dst + above, RUN_ALIGN), bit)

    @pl.when(length >= RUN_BITS[RARE_BITS - 1])
    def _():
        pieces(RUN_BITS[:RARE_BITS])
    pieces(RUN_BITS[RARE_BITS:])


def _dispatch_kernel(len_ref, src_ref, dst_ref, zrow_ref, zlen_ref, t_ref, ext_ref, lpt_ref, xs_ref,
                     buf, zeros, sem, zsem, *, n_tiles):
    i = pl.program_id(0)
    half = t_ref.shape[1] // 2

    @pl.when(i == 0)
    def _():
        zeros[...] = jnp.zeros_like(zeros)

        def clear(e, c):
            _for_each_piece(zlen_ref[e], 0, zrow_ref[e], lambda s, d, rows: pltpu.make_async_copy(
                zeros.at[pl.ds(0, rows)], xs_ref.at[pl.ds(d, rows)], zsem).start())
            return c
        lax.fori_loop(0, MOE_EXPERTS, clear, 0)

        def drain(e, c):
            _for_each_piece(zlen_ref[e], 0, zrow_ref[e], lambda s, d, rows: pltpu.make_async_copy(
                zeros.at[pl.ds(0, rows)], xs_ref.at[pl.ds(d, rows)], zsem).wait())
            return c
        lax.fori_loop(0, MOE_EXPERTS, drain, 0)

        def tail_copy(bk):
            row = pl.multiple_of(zrow_ref[MOE_EXPERTS] + bk * EXP_BLOCK, EXP_BLOCK)
            return pltpu.make_async_copy(zeros.at[pl.ds(0, EXP_BLOCK)], xs_ref.at[pl.ds(row, EXP_BLOCK)], zsem)

        def clear_tail(bk, c):
            tail_copy(bk).start()
            return c
        lax.fori_loop(0, zlen_ref[MOE_EXPERTS], clear_tail, 0)

        def drain_tail(bk, c):
            tail_copy(bk).wait()
            return c
        lax.fori_loop(0, zlen_ref[MOE_EXPERTS], drain_tail, 0)

    slot = i % 2

    def scatter(tile, sl, start):
        def body(e, c):
            k = tile * MOE_EXPERTS + e

            def piece(s, d, rows):
                cp = pltpu.make_async_copy(buf.at[sl, pl.ds(s, rows)], xs_ref.at[pl.ds(d, rows)], sem.at[sl])
                cp.start() if start else cp.wait()
            _for_each_piece(len_ref[k], src_ref[k], dst_ref[k], piece)
            return c
        lax.fori_loop(0, MOE_EXPERTS, body, 0)

    @pl.when(i >= 2)
    def _():
        scatter(i - 2, slot, False)

    lp = lpt_ref[...].astype(jnp.int32)
    j = lax.broadcasted_iota(jnp.int32, (SORT_ROWS, TOK_TILE), 0)
    perm = ((j == lp[0:1, :]) | (j == lp[1:2, :])).astype(F32).astype(BF16)
    rhs = jnp.concatenate([t_ref[...], ext_ref[...]], axis=1)
    srt = _dot(perm, rhs)
    bits = lax.bitcast_convert_type(srt, jnp.uint32)
    buf[slot, :, :half] = (bits[:, :half] >> 16) | (bits[:, half:2 * half] & jnp.uint32(0xFFFF0000))
    buf[slot, :, half:] = bits[:, 2 * half:]
    scatter(i, slot, True)

    @pl.when(i == n_tiles - 1)
    def _():
        if n_tiles > 1:
            scatter(i - 1, 1 - slot, False)
        scatter(i, slot, False)


def _expert_kernel(be_ref, nv_ref, x_ref, w1_ref, w3_ref, w2_ref, o_ref, w1b, w3b, w2b):
    i = pl.program_id(0)
    valid = i < nv_ref[0]
    half = o_ref.shape[1] // 2

    @pl.when(valid & ((i == 0) | (be_ref[i] != be_ref[jnp.maximum(i - 1, 0)])))
    def _():
        w1b[...] = w1_ref[...].astype(BF16)
        w3b[...] = w3_ref[...].astype(BF16)
        w2b[...] = w2_ref[...].astype(BF16)

    @pl.when(valid)
    def _():
        packed = x_ref[:, :half]
        lo = lax.bitcast_convert_type(packed << 16, F32)
        hi = lax.bitcast_convert_type(packed & jnp.uint32(0xFFFF0000), F32)
        x = jnp.concatenate([lo, hi], axis=1).astype(BF16)
        ext = lax.bitcast_convert_type(x_ref[:, half:], F32)
        g1 = ext[:, 0:1] + ext[:, 1:2] + ext[:, 2:3]
        g2 = ext[:, 3:4] + ext[:, 4:5] + ext[:, 5:6]
        gate = jnp.where(ext[:, 6:7] == be_ref[i].astype(F32), g1, g2)
        hdn = _silu(_dot(x, w1b[...])) * _dot(x, w3b[...])
        y = _dot(hdn.astype(BF16), w2b[...]) * gate
        hi = y.astype(BF16).astype(F32)
        lo = (y - hi).astype(BF16).astype(F32)
        o_ref[...] = lax.bitcast_convert_type(hi, jnp.uint32) | (lax.bitcast_convert_type(lo, jnp.uint32) >> 16)

    @pl.when(jnp.logical_not(valid))
    def _():
        o_ref[...] = jnp.zeros_like(o_ref)


def _combine_kernel(len_ref, src_ref, dst_ref, ys_ref, lp_ref, h_ref, mod_ref, ln_ref, o_ref, buf, sem,
                    *, alpha, first_tile, n_steps):
    step = pl.program_id(0)
    tile = step + first_tile
    slot = step % 2

    def gather(tl, sl, start):
        def body(e, c):
            k = tl * MOE_EXPERTS + e

            def piece(s, d, rows):
                cp = pltpu.make_async_copy(ys_ref.at[pl.ds(d, rows)], buf.at[sl, pl.ds(s, rows)], sem.at[sl])
                cp.start() if start else cp.wait()
            _for_each_piece(len_ref[k], src_ref[k], dst_ref[k], piece)
            return c
        lax.fori_loop(0, MOE_EXPERTS, body, 0)

    @pl.when(step == 0)
    def _():
        buf[...] = jnp.zeros_like(buf)
        gather(tile, slot, True)

    @pl.when(step + 1 < n_steps)
    def _():
        gather(tile + 1, 1 - slot, True)
    gather(tile, slot, False)

    lp = lp_ref[...].astype(jnp.int32)
    j = lax.broadcasted_iota(jnp.int32, (TOK_TILE, SORT_ROWS), 1)
    pick = ((j == lp[:, 0:1]) | (j == lp[:, 1:2])).astype(F32).astype(BF16)
    y = buf[slot]
    hi = lax.bitcast_convert_type(y & jnp.uint32(0xFFFF0000), F32).astype(BF16)
    lo = lax.bitcast_convert_type(y << 16, F32).astype(BF16)
    f = _dot(pick, hi) + _dot(pick, lo)
    o_ref[...] = _layer_norm(alpha * h_ref[...] + mod_ref[5:6, :] * f, ln_ref[0:1, :], ln_ref[1:2, :])


def _moe(lay, t, logits, h, mod, ln, w1, w3, w2, layer, alpha, first_tile=0):
    n, d = t.shape
    hid = w1.shape[3]
    n_tiles = n // TOK_TILE
    max_rows = n * MOE_TOPK + n_tiles * MOE_EXPERTS * (RUN_ALIGN - 1)
    n_blocks = -(-max_rows // EXP_BLOCK) + MOE_EXPERTS
    n_rows = n_blocks * EXP_BLOCK
    xs_w = d // 2 + ROUTER_W
    ext, lp, lpt, tables, counts = _route(logits)
    run_len, run_src, run_dst, zrow, zlen, block_e, n_valid = _plan(tables, counts, n_blocks)

    xs = pl.pallas_call(
        functools.partial(_dispatch_kernel, n_tiles=n_tiles),
        grid_spec=pltpu.PrefetchScalarGridSpec(
            num_scalar_prefetch=5,
            grid=(n_tiles,),
            in_specs=[pl.BlockSpec((TOK_TILE, d), lambda i, *_: (i, 0)),
                      pl.BlockSpec((TOK_TILE, ROUTER_W), lambda i, *_: (i, 0)),
                      pl.BlockSpec((8, TOK_TILE), lambda i, *_: (i, 0))],
            out_specs=pl.BlockSpec(memory_space=pl.ANY),
            scratch_shapes=[pltpu.VMEM((2, SORT_ROWS, xs_w), jnp.uint32),
                            pltpu.VMEM((max(TOK_TILE, EXP_BLOCK), xs_w), jnp.uint32),
                            pltpu.SemaphoreType.DMA((2,)), pltpu.SemaphoreType.DMA],
        ),
        out_shape=jax.ShapeDtypeStruct((n_rows, xs_w), jnp.uint32),
        compiler_params=_params("arbitrary"),
        name="moe_dispatch",
    )(run_len, run_src, run_dst, zrow, zlen, t, ext, lpt)

    blk = lambda i, be, nv: (jnp.minimum(i, nv[0] - 1), 0)
    ys = pl.pallas_call(
        _expert_kernel,
        grid_spec=pltpu.PrefetchScalarGridSpec(
            num_scalar_prefetch=2,
            grid=(n_blocks,),
            in_specs=[
                pl.BlockSpec((EXP_BLOCK, xs_w), blk),
                pl.BlockSpec((None, None, d, hid), lambda i, be, nv: (layer, be[i], 0, 0)),
                pl.BlockSpec((None, None, d, hid), lambda i, be, nv: (layer, be[i], 0, 0)),
                pl.BlockSpec((None, None, hid, d), lambda i, be, nv: (layer, be[i], 0, 0)),
            ],
            out_specs=pl.BlockSpec((EXP_BLOCK, d), lambda i, be, nv: (i, 0)),
            scratch_shapes=[pltpu.VMEM((d, hid), BF16), pltpu.VMEM((d, hid), BF16), pltpu.VMEM((hid, d), BF16)],
        ),
        out_shape=jax.ShapeDtypeStruct((n_rows, d), jnp.uint32),
        compiler_params=_params("arbitrary"),
        name="moe_experts",
    )(block_e, n_valid, xs, w1, w3, w2)

    n_steps = n_tiles - first_tile
    return pl.pallas_call(
        functools.partial(_combine_kernel, alpha=alpha, first_tile=first_tile, n_steps=n_steps),
        grid_spec=pltpu.PrefetchScalarGridSpec(
            num_scalar_prefetch=3,
            grid=(n_steps,),
            in_specs=[pl.BlockSpec(memory_space=pl.ANY),
                      pl.BlockSpec((TOK_TILE, ROUTER_W), lambda i, *_: (i + first_tile, 0)),
                      pl.BlockSpec((TOK_TILE, d), lambda i, *_: (i + first_tile, 0)),
                      pl.BlockSpec((None, 6, d), lambda i, *_: (lay.mod_row(i + first_tile, TOK_TILE), 0, 0)),
                      pl.BlockSpec((2, d), lambda i, *_: (0, 0))],
            out_specs=pl.BlockSpec((TOK_TILE, d), lambda i, *_: (i, 0)),
            scratch_shapes=[pltpu.VMEM((2, SORT_ROWS, d), jnp.uint32), pltpu.SemaphoreType.DMA((2,))],
        ),
        out_shape=jax.ShapeDtypeStruct((n_steps * TOK_TILE, d), F32),
        compiler_params=_params("arbitrary"),
        name="moe_combine",
    )(run_len, run_src, run_dst, ys, lp, h, mod, ln)


def kernel(x, c, ctx, c_ctx, ada_w, ada_b, ln_g, ln_b, ret_w_in, ret_decay, ret_gn_g, ret_w_out, mla_w_down, mla_q_norm, mla_kv_norm, mla_w_uq, mla_w_ukv, mla_w_out, conv_w_in, conv_w, conv_b, conv_w_out, moe_w_group, moe_b_group, moe_w_expert, moe_b_expert, moe_w1, moe_w3, moe_w2):
    b, s, d = x.shape
    depth = ada_w.shape[0]
    lay = _Layout(b, s, ctx.shape[1])
    alpha = (2.0 * depth) ** 0.25

    mod_rows = -(-(b + 1) // 8) * 8
    cc = jnp.concatenate([c, c_ctx[None, :], jnp.zeros((mod_rows - b - 1, d), F32)], axis=0)
    mod_all = _ada(cc, ada_w, ada_b).reshape(depth, mod_rows, 6, d)

    h = (ctx.reshape(lay.n_ctx, d), x.reshape(b * s, d))
    for i in range(depth):
        kind, j = i % N_MIXERS, i // N_MIXERS
        mod = mod_all[i]
        ln1 = jnp.stack([ln_g[i, 0], ln_b[i, 0]])
        ln2 = jnp.stack([ln_g[i, 1], ln_b[i, 1]])
        pad = jnp.zeros((d, ROUTER_W - MOE_GROUPS - MOE_EXPERTS), F32)
        w_router = jnp.concatenate([moe_w_group[i], moe_w_expert[i], pad], axis=1).astype(BF16)
        b_router = jnp.concatenate([moe_b_group[i], moe_b_expert[i], pad[0]]).reshape(1, ROUTER_W)
        if kind == 0:
            a = _retention(lay, h, mod, ret_w_in[j], ret_decay[j], ret_gn_g[j], d)
            h1, t, logits = _post(lay, a, h, mod, ret_w_out[j], ln1, w_router, b_router, alpha)
        elif kind == 1:
            a = _mla(lay, h, mod, mla_w_down[j], mla_q_norm[j], mla_kv_norm[j], mla_w_uq[j], mla_w_ukv[j], d)
            h1, t, logits = _post(lay, a, h, mod, mla_w_out[j], ln1, w_router, b_router, alpha)
        else:
            h1, t, logits = _conv_layer(lay, h, mod, conv_w_in[j], conv_w[j], conv_b[j], conv_w_out[j],
                                        ln1, w_router, b_router, alpha)
        first_tile = lay.n_ctx // TOK_TILE if i == depth - 1 else 0
        h = _moe(lay, t, logits, h1, mod, ln2, moe_w1, moe_w3, moe_w2, i, alpha, first_tile)
    return h.reshape(b, s, d)
```

```python
import functools

import jax
import jax.numpy as jnp
from jax import lax
from jax.experimental import pallas as pl
from jax.experimental.pallas import tpu as pltpu

F32 = jnp.float32
BF16 = jnp.bfloat16

GRID_W = 64
LN_EPS = 1e-5
RMS_EPS = 1e-6
ROPE_BASE = 10000.0
N_MIXERS = 3
RET_HEADS = 4
MLA_HEADS = 8
MLA_NOPE = 128
MLA_ROPE = 64
MLA_V = 128
MLA_QK = 256
MOE_GROUPS = 4
MOE_EPG = 8
MOE_EXPERTS = MOE_GROUPS * MOE_EPG
MOE_TOPK = 2
ROUTER_W = 128

CHUNK = 256
TM = 512
TQ = 512
TOK_TILE = 512
EXP_BLOCK = 512
RUN_ALIGN = 8
VMEM_LIMIT = 56 * 1024 * 1024


def _params(*sem):
    return pltpu.CompilerParams(dimension_semantics=sem, vmem_limit_bytes=VMEM_LIMIT)


def _const_spec(shape):
    nd = len(shape)
    return pl.BlockSpec(shape, lambda *_: (0,) * nd, pipeline_mode=pl.Buffered(1))


def _dot(a, b):
    return jnp.dot(a, b, preferred_element_type=F32)


def _dot_nt(a, b):
    return lax.dot_general(a, b, (((1,), (1,)), ((), ())), preferred_element_type=F32)


def _dot_tn(a, b):
    return lax.dot_general(a, b, (((0,), (0,)), ((), ())), preferred_element_type=F32)


def _silu(x):
    return x * jax.nn.sigmoid(x)


def _layer_norm(x, g, b):
    mu = jnp.mean(x, axis=-1, keepdims=True)
    xc = x - mu
    var = jnp.mean(xc * xc, axis=-1, keepdims=True)
    return xc * lax.rsqrt(var + LN_EPS) * g + b


def _rms_norm(x, g):
    return x * lax.rsqrt(jnp.mean(x * x, axis=-1, keepdims=True) + RMS_EPS) * g


def _ada_kernel(c_ref, w_ref, b_ref, o_ref):
    a = _silu(c_ref[...]).astype(BF16)
    o_ref[...] = _dot(a, w_ref[...].astype(BF16)) + b_ref[...]


def _ada(cc, ada_w, ada_b):
    depth, d, n6 = ada_w.shape
    rows = cc.shape[0]
    tn = 1536
    return pl.pallas_call(
        _ada_kernel,
        grid=(depth, n6 // tn),
        in_specs=[
            pl.BlockSpec((rows, d), lambda l, j: (0, 0)),
            pl.BlockSpec((None, d, tn), lambda l, j: (l, 0, j)),
            pl.BlockSpec((None, 1, tn), lambda l, j: (l, 0, j)),
        ],
        out_specs=pl.BlockSpec((None, rows, tn), lambda l, j: (l, 0, j)),
        out_shape=jax.ShapeDtypeStruct((depth, rows, n6), F32),
        compiler_params=_params("parallel", "parallel"),
        name="ada_mod",
    )(cc, ada_w, ada_b.reshape(depth, 1, n6))


class _Layout:
    def __init__(self, batch, seq, ctx_len):
        assert ctx_len == CHUNK and seq % TM == 0 and seq % CHUNK == 0
        self.batch, self.seq, self.ctx = batch, seq, ctx_len
        self.n_ctx = batch * ctx_len
        self.n = self.n_ctx + batch * seq
        assert self.n_ctx % TM == 0 and self.n_ctx % seq == 0 and self.n_ctx % TOK_TILE == 0
        assert self.n_ctx % TQ == 0 and seq % TQ == 0 and seq % TOK_TILE == 0

    def mod_row(self, tile, rows_per_tile):
        ctx_tiles = self.n_ctx // rows_per_tile
        per_batch = self.seq // rows_per_tile
        return jnp.where(tile < ctx_tiles, self.batch, (tile - ctx_tiles) // per_batch)

    def pos_block(self, tile, rows_per_tile):
        ctx_tiles = self.n_ctx // rows_per_tile
        per_batch = self.seq // rows_per_tile
        return jnp.where(tile < ctx_tiles, 0, 1 + (tile - ctx_tiles) % per_batch)


def _row_specs(x, rows, ctx_tiles):
    if not isinstance(x, tuple):
        return (x,), [pl.BlockSpec((rows, x.shape[1]), lambda i: (i, 0))]
    width = x[0].shape[1]
    return x, [pl.BlockSpec((rows, width), lambda i: (jnp.minimum(i, ctx_tiles - 1), 0)),
               pl.BlockSpec((rows, width), lambda i: (jnp.maximum(i - ctx_tiles, 0), 0))]


def _row_tile(refs, ctx_tiles):
    if len(refs) == 1:
        return refs[0][...]
    return jnp.where(pl.program_id(0) < ctx_tiles, refs[0][...], refs[1][...])


def _rope_angles(seq, dim):
    n_rows = seq // GRID_W
    rows = jnp.repeat(jnp.arange(n_rows, dtype=F32), GRID_W)
    cols = jnp.tile(jnp.arange(GRID_W, dtype=F32), n_rows)
    quarter = dim // 4
    inv_freq = ROPE_BASE ** (-jnp.arange(quarter, dtype=F32) / quarter)
    return jnp.concatenate([rows[:, None] * inv_freq, cols[:, None] * inv_freq], axis=-1)


def _with_identity_block(table, fill, rows):
    ident = jnp.full((rows, table.shape[1]), fill, F32)
    return jnp.concatenate([ident, table], axis=0)


def _proj_ret_kernel(*refs, dk, ctx_tiles):
    h_refs = refs[:-14]
    (mod_ref, cos_ref, sin_ref, kdec_ref, wq_ref, wk_ref, wv_ref, wg_ref,
     q_ref, k_ref, kf_ref, kb_ref, v_ref, g_ref) = refs[-14:]
    u = (_row_tile(h_refs, ctx_tiles) * (1.0 + mod_ref[1:2, :]) + mod_ref[0:1, :]).astype(BF16)
    cos = cos_ref[...]
    sin = sin_ref[...]
    half = dk // 2
    yq = _dot(u, wq_ref[...])
    for hd in range(RET_HEADS):
        x1 = yq[:, hd * dk:hd * dk + half]
        x2 = yq[:, hd * dk + half:(hd + 1) * dk]
        q_ref[:, hd * dk:hd * dk + half] = (x1 * cos - x2 * sin).astype(BF16)
        q_ref[:, hd * dk + half:(hd + 1) * dk] = (x1 * sin + x2 * cos).astype(BF16)
    yk = _dot(u, wk_ref[...])
    k_scale = dk ** -0.5
    for hd in range(RET_HEADS):
        x1 = yk[:, hd * dk:hd * dk + half]
        x2 = yk[:, hd * dk + half:(hd + 1) * dk]
        df = kdec_ref[:, hd:hd + 1]
        db = kdec_ref[:, RET_HEADS + hd:RET_HEADS + hd + 1]
        for part, o in ((0, (x1 * cos - x2 * sin) * k_scale), (1, (x1 * sin + x2 * cos) * k_scale)):
            sl = slice(hd * dk + part * half, hd * dk + (part + 1) * half)
            k_ref[:, sl] = o.astype(BF16)
            kf_ref[:, sl] = (o * df).astype(BF16)
            kb_ref[:, sl] = (o * db).astype(BF16)
    v_ref[...] = _dot(u, wv_ref[...]).astype(BF16)
    g_ref[...] = _silu(_dot(u, wg_ref[...]))


def _ret_state_kernel(cdec_ref, kd_ref, v_ref, s_ref, state_ref, *, dk, dv):
    @pl.when(pl.program_id(1) == 0)
    def _():
        state_ref[...] = jnp.zeros_like(state_ref)

    for hd in range(RET_HEADS):
        st = state_ref[hd]
        s_ref[hd] = st.astype(BF16)
        kd = kd_ref[:, hd * dk:(hd + 1) * dk]
        v = v_ref[:, hd * dv:(hd + 1) * dv]
        state_ref[hd] = st * cdec_ref[RET_HEADS + hd] + _dot_tn(kd, v)


def _ret_out_kernel(cdec_ref, q_ref, k_ref, kd_ref, v_ref, sb_ref, mask_ref, qdec_ref, g_ref, gn_ref, o_ref,
                    state_ref, *, dk, dv):
    @pl.when(pl.program_id(1) == 0)
    def _():
        state_ref[...] = jnp.zeros_like(state_ref)

    for hd in range(RET_HEADS):
        q = q_ref[:, hd * dk:(hd + 1) * dk]
        k = k_ref[:, hd * dk:(hd + 1) * dk]
        kd = kd_ref[:, hd * dk:(hd + 1) * dk]
        v = v_ref[:, hd * dv:(hd + 1) * dv]
        p = (_dot_nt(q, k) * mask_ref[hd]).astype(BF16)
        st = state_ref[hd]
        o = (_dot(p, v) + qdec_ref[:, hd:hd + 1] * _dot(q, st.astype(BF16))
             + qdec_ref[:, RET_HEADS + hd:RET_HEADS + hd + 1] * _dot(q, sb_ref[hd]))
        state_ref[hd] = st * cdec_ref[hd] + _dot_tn(kd, v)
        mu = jnp.mean(o, axis=-1, keepdims=True)
        oc = o - mu
        var = jnp.mean(oc * oc, axis=-1, keepdims=True)
        on = oc * lax.rsqrt(var + LN_EPS) * gn_ref[:, hd * dv:(hd + 1) * dv]
        o_ref[:, hd * dv:(hd + 1) * dv] = (g_ref[:, hd * dv:(hd + 1) * dv] * on).astype(BF16)


def _retention(lay, h, mod, w_in, decay_logit, gn_g, d_model):
    n, b, nc = lay.n, lay.batch, lay.seq // CHUNK
    dk = d_model // RET_HEADS
    dv = 2 * dk
    qk, vw = RET_HEADS * dk, RET_HEADS * dv
    w_in = w_in.astype(BF16)
    wq, wk, wv, wg = w_in[:, :qk], w_in[:, qk:2 * qk], w_in[:, 2 * qk:2 * qk + vw], w_in[:, 2 * qk + vw:]

    ang = _rope_angles(lay.seq, dk)
    cos_t = _with_identity_block(jnp.cos(ang), 1.0, TM)
    sin_t = _with_identity_block(jnp.sin(ang), 0.0, TM)

    lg = jax.nn.log_sigmoid(decay_logit.astype(F32))
    idx = jnp.arange(CHUNK, dtype=F32)
    k_pow = jnp.stack([CHUNK - 1.0 - idx, idx])
    q_pow = jnp.stack([idx + 1.0, CHUNK - idx])
    kdec = jnp.exp(k_pow[:, :, None] * lg[:, None, :])
    qdec = jnp.exp(q_pow[:, :, None] * lg[:, None, :])
    kdec = jnp.moveaxis(kdec, 0, 1).reshape(CHUNK, 2 * RET_HEADS)
    qdec = jnp.moveaxis(qdec, 0, 1).reshape(CHUNK, 2 * RET_HEADS)
    cdec = jnp.exp(CHUNK * lg).reshape(2 * RET_HEADS)
    rel = idx[:, None] - idx[None, :]
    rel = jnp.stack([rel, -rel])
    intra = jnp.where(rel[:, None] >= 0, jnp.exp(jnp.maximum(rel[:, None], 0.0) * lg[:, :, None, None]), 0.0)
    kdec_tm = jnp.tile(kdec, (TM // CHUNK, 1))

    n_tiles = n // TM
    row = lambda i: (i, 0)
    ctx_tiles = lay.n_ctx // TM
    h, h_specs = _row_specs(h, TM, ctx_tiles)
    q, k, kf, kb, v, g = pl.pallas_call(
        functools.partial(_proj_ret_kernel, dk=dk, ctx_tiles=ctx_tiles),
        grid=(n_tiles,),
        in_specs=h_specs + [
            pl.BlockSpec((None, 6, d_model), lambda i: (lay.mod_row(i, TM), 0, 0)),
            pl.BlockSpec((TM, dk // 2), lambda i: (lay.pos_block(i, TM), 0)),
            pl.BlockSpec((TM, dk // 2), lambda i: (lay.pos_block(i, TM), 0)),
            _const_spec((TM, 2 * RET_HEADS)),
            _const_spec((d_model, qk)), _const_spec((d_model, qk)),
            _const_spec((d_model, vw)), _const_spec((d_model, vw)),
        ],
        out_specs=[pl.BlockSpec((TM, qk), row)] * 4 + [pl.BlockSpec((TM, vw), row)] * 2,
        out_shape=[jax.ShapeDtypeStruct((n, qk), BF16)] * 4
        + [jax.ShapeDtypeStruct((n, vw), BF16), jax.ShapeDtypeStruct((n, vw), F32)],
        compiler_params=_params("parallel"),
        name="ret_proj",
    )(*h, mod, cos_t, sin_t, kdec_tm, wq, wk, wv, wg)

    def chunk_fwd(bi, c):
        return (jnp.where(c == 0, bi, b + bi * nc + c - 1), 0)

    def chunk_bwd(bi, c):
        return (jnp.where(c == 0, bi, b + bi * nc + nc - c), 0)

    n_chunks = n // CHUNK
    s_b = pl.pallas_call(
        functools.partial(_ret_state_kernel, dk=dk, dv=dv),
        grid=(b, nc + 1),
        in_specs=[pl.BlockSpec(memory_space=pltpu.SMEM),
                  pl.BlockSpec((CHUNK, qk), chunk_bwd), pl.BlockSpec((CHUNK, vw), chunk_bwd)],
        out_specs=pl.BlockSpec((None, RET_HEADS, dk, dv), lambda bi, c: (chunk_bwd(bi, c)[0], 0, 0, 0)),
        out_shape=jax.ShapeDtypeStruct((n_chunks, RET_HEADS, dk, dv), BF16),
        scratch_shapes=[pltpu.VMEM((RET_HEADS, dk, dv), F32)],
        compiler_params=_params("parallel", "arbitrary"),
        name="ret_state_bwd",
    )(cdec, kb, v)

    return pl.pallas_call(
        functools.partial(_ret_out_kernel, dk=dk, dv=dv),
        grid=(b, nc + 1),
        in_specs=[
            pl.BlockSpec(memory_space=pltpu.SMEM),
            pl.BlockSpec((CHUNK, qk), chunk_fwd), pl.BlockSpec((CHUNK, qk), chunk_fwd),
            pl.BlockSpec((CHUNK, qk), chunk_fwd), pl.BlockSpec((CHUNK, vw), chunk_fwd),
            pl.BlockSpec((None, RET_HEADS, dk, dv), lambda bi, c: (chunk_fwd(bi, c)[0], 0, 0, 0)),
            _const_spec((RET_HEADS, CHUNK, CHUNK)), _const_spec((CHUNK, 2 * RET_HEADS)),
            pl.BlockSpec((CHUNK, vw), chunk_fwd), _const_spec((1, vw)),
        ],
        out_specs=pl.BlockSpec((CHUNK, vw), chunk_fwd),
        out_shape=jax.ShapeDtypeStruct((n, vw), BF16),
        scratch_shapes=[pltpu.VMEM((RET_HEADS, dk, dv), F32)],
        compiler_params=_params("parallel", "arbitrary"),
        name="ret_scan_out",
    )(cdec, q, k, kf, v, s_b, intra[0] + intra[1], qdec, g, gn_g.reshape(1, vw).astype(F32))


def _proj_mla_kernel(h_ref, mod_ref, ct_ref, st_ref, wdq_ref, wdkv_ref, wkr_ref, qn_ref, kvn_ref,
                     wuq_ref, wukn_ref, wuv_ref, q_ref, k_ref, vt_ref):
    u = (h_ref[...] * (1.0 + mod_ref[1:2, :]) + mod_ref[0:1, :]).astype(BF16)
    ct = ct_ref[...]
    st = st_ref[...]
    half = MLA_QK // 2

    def rope(x):
        return x * ct + pltpu.roll(x, half // 2, 1) * st

    cq = _rms_norm(_dot(u, wdq_ref[...]), qn_ref[...]).astype(BF16)
    yq = _dot(cq, wuq_ref[...])
    ckv = _rms_norm(_dot(u, wdkv_ref[...]), kvn_ref[...]).astype(BF16)
    kn = _dot(ckv, wukn_ref[...])
    kr = rope(_dot(u, wkr_ref[...])).astype(BF16)
    for hd in range(MLA_HEADS):
        q_ref[:, hd * MLA_QK:hd * MLA_QK + half] = yq[:, hd * MLA_QK:hd * MLA_QK + half].astype(BF16)
        q_ref[:, hd * MLA_QK + half:(hd + 1) * MLA_QK] = rope(
            yq[:, hd * MLA_QK + half:(hd + 1) * MLA_QK]).astype(BF16)
        k_ref[:, hd * MLA_QK:hd * MLA_QK + half] = kn[:, hd * half:(hd + 1) * half].astype(BF16)
        k_ref[:, hd * MLA_QK + half:(hd + 1) * MLA_QK] = kr
    vt_ref[...] = _dot(ckv, wuv_ref[...]).T.astype(BF16)


KV_CHUNK = 512
LOG2_E = 1.4426950408889634


def _kv_chunks(kv_refs, n_kv):
    chunks = []
    row = 0
    for j in range(n_kv):
        keys = kv_refs[2 * j].shape[0]
        step = min(keys, KV_CHUNK)
        for off in range(0, keys, step):
            chunks.append((j, off, row, step))
            row += step
    return chunks


def _attn_kernel(q_ref, *rest, scale, n_kv, pipelined):
    kv_refs = rest[:2 * n_kv]
    chunks = _kv_chunks(kv_refs, n_kv)

    def scores(c, s_ref, m):
        j, off, row, w = c
        s = _dot_nt(kv_refs[2 * j][off:off + w, :], q_ref[...])
        s_ref[row:row + w, :] = s
        cm = jnp.max(s, axis=0, keepdims=True)
        return cm if m is None else jnp.maximum(m, cm)

    def values(c, s_ref, m, acc, den):
        j, off, row, w = c
        p = jnp.exp2((s_ref[row:row + w, :] - m) * (scale * LOG2_E))
        cs = jnp.sum(p, axis=0, keepdims=True)
        pv = _dot(kv_refs[2 * j + 1][:, off:off + w], p.astype(BF16))
        return (pv if acc is None else acc + pv), (cs if den is None else den + cs)

    if not pipelined:
        o_ref, s_ref = rest[-2:]
        m = acc = den = None
        for c in chunks:
            m = scores(c, s_ref, m)
        for c in chunks:
            acc, den = values(c, s_ref, m, acc, den)
        o_ref[...] = (acc / den).T.astype(BF16)
        return

    o_ref, s0_ref, s1_ref, m0_ref, m1_ref = rest[-5:]
    t = pl.program_id(0)
    last = pl.num_programs(0) - 1
    bufs = ((s0_ref, m0_ref), (s1_ref, m1_ref))

    def step(new, old):
        m_prev = old[1][...] if old else None
        m = acc = den = None
        for c in chunks:
            if new:
                m = scores(c, new[0], m)
            if old:
                acc, den = values(c, old[0], m_prev, acc, den)
        if new:
            new[1][...] = m
        if old:
            o_ref[...] = (acc / den).T.astype(BF16)

    @pl.when(t == 0)
    def _():
        step(bufs[0], None)

    for parity in (0, 1):
        @pl.when((t > 0) & (t < last) & (t % 2 == parity))
        def _(parity=parity):
            step(bufs[parity], bufs[1 - parity])

        @pl.when((t == last) & (t % 2 == parity))
        def _(parity=parity):
            step(None, bufs[1 - parity])


def _pad_rope_cols(w):
    z = jnp.zeros((w.shape[0], MLA_ROPE // 2), w.dtype)
    return jnp.concatenate([w[:, :MLA_ROPE // 2], z, w[:, MLA_ROPE // 2:], z], axis=1)


def _mla(lay, h, mod, w_down, q_norm, kv_norm, w_uq, w_ukv, d_model):
    n, b, s = lay.n, lay.batch, lay.seq
    q_lora, kv_lora = q_norm.shape[0], kv_norm.shape[0]
    w_dq = w_down[:, :q_lora].astype(BF16)
    w_dkv = w_down[:, q_lora:q_lora + kv_lora].astype(BF16)
    w_kr = _pad_rope_cols(w_down[:, q_lora + kv_lora:]).astype(BF16)
    w_uq = w_uq.reshape(q_lora, MLA_HEADS, MLA_NOPE + MLA_ROPE)
    w_uq = jnp.concatenate(
        [w_uq[:, :, :MLA_NOPE], jax.vmap(_pad_rope_cols, 1, 1)(w_uq[:, :, MLA_NOPE:])], axis=2)
    w_uq = w_uq.reshape(q_lora, MLA_HEADS * MLA_QK).astype(BF16)
    w_ukv = w_ukv.reshape(kv_lora, MLA_HEADS, MLA_NOPE + MLA_V)
    w_ukn = w_ukv[:, :, :MLA_NOPE].reshape(kv_lora, MLA_HEADS * MLA_NOPE).astype(BF16)
    w_uv = w_ukv[:, :, MLA_NOPE:].reshape(kv_lora, MLA_HEADS * MLA_V).astype(BF16)

    ang = _rope_angles(s, MLA_ROPE)
    z = jnp.zeros_like(ang)
    ct = _with_identity_block(jnp.concatenate([jnp.cos(ang), z, jnp.cos(ang), z], axis=1), 1.0, TM)
    st = _with_identity_block(jnp.concatenate([-jnp.sin(ang), z, jnp.sin(ang), z], axis=1), 0.0, TM)

    row = lambda i: (i, 0)
    qw, vw = MLA_HEADS * MLA_QK, MLA_HEADS * MLA_V
    q, k, vt = pl.pallas_call(
        _proj_mla_kernel,
        grid=(n // TM,),
        in_specs=[
            pl.BlockSpec((TM, d_model), row),
            pl.BlockSpec((None, 6, d_model), lambda i: (lay.mod_row(i, TM), 0, 0)),
            pl.BlockSpec((TM, MLA_QK // 2), lambda i: (lay.pos_block(i, TM), 0)),
            pl.BlockSpec((TM, MLA_QK // 2), lambda i: (lay.pos_block(i, TM), 0)),
            _const_spec((d_model, q_lora)), _const_spec((d_model, kv_lora)), _const_spec((d_model, MLA_QK // 2)),
            _const_spec((1, q_lora)), _const_spec((1, kv_lora)),
            _const_spec((q_lora, qw)), _const_spec((kv_lora, vw)), _const_spec((kv_lora, vw)),
        ],
        out_specs=[pl.BlockSpec((TM, qw), row), pl.BlockSpec((TM, qw), row),
                   pl.BlockSpec((vw, TM), lambda i: (0, i))],
        out_shape=[jax.ShapeDtypeStruct((n, qw), BF16), jax.ShapeDtypeStruct((n, qw), BF16),
                   jax.ShapeDtypeStruct((vw, n), BF16)],
        compiler_params=_params("parallel"),
        name="mla_proj",
    )(h, mod, ct, st, w_dq, w_dkv, w_kr, q_norm.reshape(1, -1).astype(F32), kv_norm.reshape(1, -1).astype(F32),
      w_uq, w_ukn, w_uv)

    scale = (MLA_NOPE + MLA_ROPE) ** -0.5
    ctx_q = lambda bi, hd: (bi, hd)
    att_ctx = pl.pallas_call(
        functools.partial(_attn_kernel, scale=scale, n_kv=1, pipelined=False),
        grid=(b, MLA_HEADS),
        in_specs=[pl.BlockSpec((CHUNK, MLA_QK), ctx_q), pl.BlockSpec((CHUNK, MLA_QK), ctx_q),
                  pl.BlockSpec((MLA_V, CHUNK), lambda bi, hd: (hd, bi))],
        out_specs=pl.BlockSpec((CHUNK, MLA_V), ctx_q),
        out_shape=jax.ShapeDtypeStruct((lay.n_ctx, vw), BF16),
        scratch_shapes=[pltpu.VMEM((CHUNK, CHUNK), F32)],
        compiler_params=_params("parallel", "parallel"),
        name="mla_attn_ctx",
    )(q, k, vt)

    lat_blk = lay.n_ctx // s
    q_tiles = s // TQ
    n_work = b * MLA_HEADS * q_tiles

    def work(f):
        bh, t = f // q_tiles, f % q_tiles
        return bh // MLA_HEADS, bh % MLA_HEADS, t

    def scored(f):
        return work(jnp.minimum(f, n_work - 1))

    def finished(f):
        return work(jnp.maximum(f - 1, 0))

    def q_map(f):
        bi, hd, t = scored(f)
        return (lay.n_ctx // TQ + bi * q_tiles + t, hd)

    def out_map(f):
        bi, hd, t = finished(f)
        return (bi * q_tiles + t, hd)

    att_lat = pl.pallas_call(
        functools.partial(_attn_kernel, scale=scale, n_kv=2, pipelined=True),
        grid=(n_work + 1,),
        in_specs=[pl.BlockSpec((TQ, MLA_QK), q_map),
                  pl.BlockSpec((CHUNK, MLA_QK), lambda f: scored(f)[:2]),
                  pl.BlockSpec((MLA_V, CHUNK), lambda f: (finished(f)[1], finished(f)[0])),
                  pl.BlockSpec((s, MLA_QK), lambda f: (lat_blk + scored(f)[0], scored(f)[1])),
                  pl.BlockSpec((MLA_V, s), lambda f: (finished(f)[1], lat_blk + finished(f)[0]))],
        out_specs=pl.BlockSpec((TQ, MLA_V), out_map),
        out_shape=jax.ShapeDtypeStruct((n - lay.n_ctx, vw), BF16),
        scratch_shapes=[pltpu.VMEM((CHUNK + s, TQ), F32)] * 2 + [pltpu.VMEM((1, TQ), F32)] * 2,
        compiler_params=_params("arbitrary"),
        name="mla_attn_lat",
    )(q, k, vt, k, vt)
    return att_ctx, att_lat


N_TAIL_REFS = 14


def _post_tail(y, h, refs, alpha):
    mod_ref, ln_ref, wr_ref, br_ref, tri_ref, upper_ref, h_out, t_out = refs[0], refs[1], refs[2], refs[3], \
        refs[4], refs[5], refs[6], refs[7]
    h1 = _layer_norm(alpha * h + mod_ref[2:3, :] * y, ln_ref[0:1, :], ln_ref[1:2, :])
    t = h1 * (1.0 + mod_ref[4:5, :]) + mod_ref[3:4, :]
    t = t.astype(BF16)
    h_out[...] = h1
    t_out[...] = t
    _route_tile(_dot(t, wr_ref[...]) + br_ref[...], tri_ref, upper_ref, *refs[8:])


def _post_kernel(*refs, alpha, ctx_tiles, n_a):
    tail = refs[-(N_TAIL_REFS + 1):]
    a_refs, h_refs = refs[:n_a], refs[n_a:-(N_TAIL_REFS + 1)]
    y = _dot(_row_tile(a_refs, ctx_tiles), tail[0][...])
    _post_tail(y, _row_tile(h_refs, ctx_tiles), tail[1:], alpha)


def _tail_specs(n, d, row_tile):
    r = lax.broadcasted_iota(jnp.int32, (TOK_TILE, TOK_TILE), 0)
    c = lax.broadcasted_iota(jnp.int32, (TOK_TILE, TOK_TILE), 1)
    tri = (c < r).astype(BF16)
    r = lax.broadcasted_iota(jnp.int32, (ROUTER_W, ROUTER_W), 0)
    c = lax.broadcasted_iota(jnp.int32, (ROUTER_W, ROUTER_W), 1)
    upper = (r < c).astype(BF16)
    assert row_tile == TOK_TILE
    n_tiles = n // TOK_TILE
    row = lambda i: (i, 0)
    in_specs = [_const_spec((TOK_TILE, TOK_TILE)), _const_spec((ROUTER_W, ROUTER_W))]
    out_specs = [pl.BlockSpec((TOK_TILE, d), row), pl.BlockSpec((TOK_TILE, d), row),
                 pl.BlockSpec((TOK_TILE, ROUTER_W), row), pl.BlockSpec((TOK_TILE, ROUTER_W), row),
                 pl.BlockSpec((8, TOK_TILE), row), pl.BlockSpec((8, ROUTER_W), row),
                 pl.BlockSpec((8, ROUTER_W), lambda i: (0, 0))]
    out_shape = [jax.ShapeDtypeStruct((n, d), F32), jax.ShapeDtypeStruct((n, d), BF16),
                 jax.ShapeDtypeStruct((n, ROUTER_W), BF16), jax.ShapeDtypeStruct((n, ROUTER_W), F32),
                 jax.ShapeDtypeStruct((n_tiles * 8, TOK_TILE), F32),
                 jax.ShapeDtypeStruct((n_tiles * 8, ROUTER_W), F32),
                 jax.ShapeDtypeStruct((8, ROUTER_W), F32)]
    return (tri, upper), in_specs, out_specs, out_shape, [pltpu.VMEM((1, ROUTER_W), F32)]


def _post(lay, a, h, mod, w_out, ln, w_router, b_router, alpha):
    n, d = lay.n, w_out.shape[1]
    row = lambda i: (i, 0)
    ctx_tiles = lay.n_ctx // TM
    a, a_specs = _row_specs(a, TM, ctx_tiles)
    h, h_specs = _row_specs(h, TM, ctx_tiles)
    ka = a[0].shape[1]
    consts, c_specs, out_specs, out_shape, scratch = _tail_specs(n, d, TM)
    return pl.pallas_call(
        functools.partial(_post_kernel, alpha=alpha, ctx_tiles=ctx_tiles, n_a=len(a)),
        grid=(n // TM,),
        in_specs=a_specs + h_specs + [
            _const_spec((ka, d)),
            pl.BlockSpec((None, 6, d), lambda i: (lay.mod_row(i, TM), 0, 0)),
            _const_spec((2, d)), _const_spec((d, ROUTER_W)), _const_spec((1, ROUTER_W)),
        ] + c_specs,
        out_specs=out_specs,
        out_shape=out_shape,
        scratch_shapes=scratch,
        compiler_params=_params("arbitrary"),
        name="post_mixer",
    )(*a, *h, w_out.astype(BF16), mod, ln, w_router, b_router, *consts)


HALO = 8


def _conv_kernel(h_ref, hp_ref, hn_ref, win_ref, cw_ref, cb_ref, w_ref, *tail, alpha, tiles_per_seq, ctx_tiles):
    mod_ref = tail[0]
    i = pl.program_id(0)
    rows, d = h_ref.shape
    h = h_ref[...]
    hx = jnp.concatenate([hp_ref[...], h, hn_ref[...]], axis=0)
    u = (hx * (1.0 + mod_ref[1:2, :]) + mod_ref[0:1, :]).astype(BF16)
    y = _dot(u, win_ref[...])
    gate_b = y[HALO:HALO + rows, :d]
    zx = y[:, d:2 * d] * y[:, 2 * d:]
    z = zx[HALO:HALO + rows]
    is_ctx = i < ctx_tiles
    pos = jnp.where(is_ctx, 0, (i - ctx_tiles) % tiles_per_seq)
    last = jnp.where(is_ctx, 0, tiles_per_seq - 1)
    z_before = jnp.where(pos == 0, 0.0, zx[HALO - 1:HALO])
    z_after = jnp.where(pos == last, 0.0, zx[HALO + rows:HALO + rows + 1])
    r = lax.broadcasted_iota(jnp.int32, (rows, 1), 0)
    in_seq = r & (CHUNK - 1)
    z_prev = jnp.where(r == 0, z_before, pltpu.roll(z, 1, 0))
    z_next = jnp.where(r == rows - 1, z_after, pltpu.roll(z, rows - 1, 0))
    z_prev = jnp.where(jnp.logical_and(is_ctx, in_seq == 0), 0.0, z_prev)
    z_next = jnp.where(jnp.logical_and(is_ctx, in_seq == CHUNK - 1), 0.0, z_next)
    conv = cw_ref[0:1, :] * z_prev + cw_ref[1:2, :] * z + cw_ref[2:3, :] * z_next + cb_ref[...]
    yo = _dot((gate_b * conv).astype(BF16), w_ref[...])
    _post_tail(yo, h, tail, alpha)


def _conv_layer(lay, h, mod, w_in, cw, cb, w_out, ln, w_router, b_router, alpha):
    n, d = h.shape
    n_tiles = n // TM
    per = TM // HALO
    row = lambda i: (i, 0)
    consts, c_specs, out_specs, out_shape, scratch = _tail_specs(n, d, TM)
    return pl.pallas_call(
        functools.partial(_conv_kernel, alpha=alpha, tiles_per_seq=lay.seq // TM, ctx_tiles=lay.n_ctx // TM),
        grid=(n_tiles,),
        in_specs=[
            pl.BlockSpec((TM, d), row),
            pl.BlockSpec((HALO, d), lambda i: (jnp.maximum(i * per - 1, 0), 0)),
            pl.BlockSpec((HALO, d), lambda i: (jnp.minimum((i + 1) * per, n_tiles * per - 1), 0)),
            _const_spec((d, 3 * d)), _const_spec((3, d)), _const_spec((1, d)), _const_spec((d, d)),
            pl.BlockSpec((None, 6, d), lambda i: (lay.mod_row(i, TM), 0, 0)),
            _const_spec((2, d)), _const_spec((d, ROUTER_W)), _const_spec((1, ROUTER_W)),
        ] + c_specs,
        out_specs=out_specs,
        out_shape=out_shape,
        scratch_shapes=scratch,
        compiler_params=_params("arbitrary"),
        name="conv_layer",
    )(h, h, h, w_in.astype(BF16), cw.astype(F32), cb.reshape(1, d).astype(F32), w_out.astype(BF16),
      mod, ln, w_router, b_router, *consts)


def _route_tile(x, tri_ref, upper_ref, ext_ref, lp_ref, lpt_ref, tab_ref, cnt_ref, base_ref):
    @pl.when(pl.program_id(0) == 0)
    def _():
        base_ref[...] = jnp.zeros_like(base_ref)

    lane = lax.broadcasted_iota(jnp.int32, x.shape, 1).astype(F32)

    def softmax(mask):
        m = jnp.max(jnp.where(mask, x, -jnp.inf), axis=-1, keepdims=True)
        e = jnp.where(mask, jnp.exp(x - m), 0.0)
        return e / jnp.sum(e, axis=-1, keepdims=True)

    def top1(prob, mask):
        p = jnp.max(jnp.where(mask, prob, -1.0), axis=-1, keepdims=True)
        i = jnp.min(jnp.where(mask & (prob == p), lane, float(ROUTER_W)), axis=-1, keepdims=True)
        return p, i

    g_mask = lane < MOE_GROUPS
    g_p, g_idx = top1(softmax(g_mask), g_mask)
    e_lo = MOE_GROUPS + MOE_EPG * g_idx
    e_mask = (lane >= e_lo) & (lane < e_lo + MOE_EPG)
    e_prob = softmax(e_mask)
    p1, i1 = top1(e_prob, e_mask)
    p2, i2 = top1(e_prob, e_mask & (lane != i1))
    denom = p1 + p2
    gate1 = g_p * p1 / denom
    gate2 = g_p * p2 / denom

    sel1 = lane == i1
    sel2 = lane == i2
    cnt = (sel1 | sel2).astype(F32)
    within = _dot(tri_ref[...], cnt.astype(BF16))
    cnt_tile = jnp.sum(cnt, axis=0, keepdims=True)
    ci = jnp.broadcast_to(cnt_tile, (8, ROUTER_W)).astype(jnp.int32)
    ci = (ci + (RUN_ALIGN - 1)) & ~(RUN_ALIGN - 1)
    run_tile = ci[0:1].astype(F32)
    start = (_dot((ci >> 4).astype(F32).astype(BF16), upper_ref[...]) * 16.0
             + _dot((ci & 15).astype(F32).astype(BF16), upper_ref[...]))[0:1]
    where_to = within + start
    lp1 = jnp.sum(jnp.where(sel1, where_to, 0.0), axis=-1, keepdims=True)
    lp2 = jnp.sum(jnp.where(sel2, where_to, 0.0), axis=-1, keepdims=True)

    before = base_ref[...]
    total = before + run_tile
    base_ref[...] = total
    cnt_ref[...] = jnp.broadcast_to(total, cnt_ref.shape)
    row = lax.broadcasted_iota(jnp.int32, tab_ref.shape, 0)
    tab_ref[...] = jnp.where(row == 0, run_tile, jnp.where(row == 1, before, jnp.where(row == 2, start, 0.0)))

    def pieces(g):
        hi = g.astype(BF16).astype(F32)
        mid = (g - hi).astype(BF16).astype(F32)
        return hi, mid, g - hi - mid

    a1, b1, c1 = pieces(gate1)
    a2, b2, c2 = pieces(gate2)
    cols = (a1, b1, c1, a2, b2, c2, i1 - MOE_GROUPS)
    ext = jnp.zeros(x.shape, F32)
    for j, col in enumerate(cols):
        ext = jnp.where(lane == j, col, ext)
    ext_ref[...] = ext.astype(BF16)
    lp = jnp.where(lane == 0, lp1, jnp.where(lane == 1, lp2, 0.0))
    lp_ref[...] = lp
    lpt_ref[...] = lp.T[0:8, :]


def _plan(tables, counts, n_blocks):
    n_tiles = tables.shape[0] // 8
    tables = tables.reshape(n_tiles, 8, ROUTER_W)[:, :3, MOE_GROUPS:MOE_GROUPS + MOE_EXPERTS].astype(jnp.int32)
    counts = counts[0, MOE_GROUPS:MOE_GROUPS + MOE_EXPERTS].astype(jnp.int32)
    padded = (counts + EXP_BLOCK - 1) // EXP_BLOCK * EXP_BLOCK
    pad_ends = jnp.cumsum(padded)
    pad_starts = pad_ends - padded
    run_len = tables[:, 0].reshape(-1)
    run_src = tables[:, 2].reshape(-1)
    run_dst = (tables[:, 1] + pad_starts[None, :]).reshape(-1)
    first_row = jnp.arange(n_blocks, dtype=jnp.int32) * EXP_BLOCK
    block_e = jnp.minimum(jnp.sum(first_row[:, None] >= pad_ends[None, :], axis=1),
                          MOE_EXPERTS - 1).astype(jnp.int32)
    n_valid = (pad_ends[-1] // EXP_BLOCK).astype(jnp.int32).reshape(1)
    clear_row = jnp.concatenate([pad_starts + counts, pad_ends[-1:]])
    clear_len = jnp.concatenate([padded - counts, n_blocks - n_valid])
    return run_len, run_src, run_dst, clear_row, clear_len, block_e, n_valid


RUN_BITS = tuple(b for b in (1 << k for k in range(TOK_TILE.bit_length() - 1, -1, -1)) if b >= RUN_ALIGN)
RARE_BITS = 4
SORT_ROWS = MOE_TOPK * TOK_TILE + MOE_EXPERTS * RUN_ALIGN


def _for_each_piece(length, src, dst, fn):
    def pieces(bits):
        for bit in bits:
            above = length & ~(2 * bit - 1)

            @pl.when((length & bit) != 0)
            def _(bit=bit, above=above):
                fn(pl.multiple_of(src + above, RUN_ALIGN), pl.multiple_of(dst + above, RUN_ALIGN), bit)

    @pl.when(length >= RUN_BITS[RARE_BITS - 1])
    def _():
        pieces(RUN_BITS[:RARE_BITS])
    pieces(RUN_BITS[RARE_BITS:])


def _dispatch_kernel(len_ref, src_ref, dst_ref, zrow_ref, zlen_ref, t_ref, ext_ref, lpt_ref, xs_ref,
                     buf, zeros, sem, zsem, *, n_tiles):
    i = pl.program_id(0)
    half = t_ref.shape[1] // 2

    @pl.when(i == 0)
    def _():
        zeros[...] = jnp.zeros_like(zeros)

        def clear(e, c):
            _for_each_piece(zlen_ref[e], 0, zrow_ref[e], lambda s, d, rows: pltpu.make_async_copy(
                zeros.at[pl.ds(0, rows)], xs_ref.at[pl.ds(d, rows)], zsem).start())
            return c
        lax.fori_loop(0, MOE_EXPERTS, clear, 0)

        def drain(e, c):
            _for_each_piece(zlen_ref[e], 0, zrow_ref[e], lambda s, d, rows: pltpu.make_async_copy(
                zeros.at[pl.ds(0, rows)], xs_ref.at[pl.ds(d, rows)], zsem).wait())
            return c
        lax.fori_loop(0, MOE_EXPERTS, drain, 0)

        def tail_copy(bk):
            row = pl.multiple_of(zrow_ref[MOE_EXPERTS] + bk * EXP_BLOCK, EXP_BLOCK)
            return pltpu.make_async_copy(zeros.at[pl.ds(0, EXP_BLOCK)], xs_ref.at[pl.ds(row, EXP_BLOCK)], zsem)

        def clear_tail(bk, c):
            tail_copy(bk).start()
            return c
        lax.fori_loop(0, zlen_ref[MOE_EXPERTS], clear_tail, 0)

        def drain_tail(bk, c):
            tail_copy(bk).wait()
            return c
        lax.fori_loop(0, zlen_ref[MOE_EXPERTS], drain_tail, 0)

    slot = i % 2

    def scatter(tile, sl, start):
        def body(e, c):
            k = tile * MOE_EXPERTS + e

            def piece(s, d, rows):
                cp = pltpu.make_async_copy(buf.at[sl, pl.ds(s, rows)], xs_ref.at[pl.ds(d, rows)], sem.at[sl])
                cp.start() if start else cp.wait()
            _for_each_piece(len_ref[k], src_ref[k], dst_ref[k], piece)
            return c
        lax.fori_loop(0, MOE_EXPERTS, body, 0)

    @pl.when(i >= 2)
    def _():
        scatter(i - 2, slot, False)

    lp = lpt_ref[...].astype(jnp.int32)
    j = lax.broadcasted_iota(jnp.int32, (SORT_ROWS, TOK_TILE), 0)
    perm = ((j == lp[0:1, :]) | (j == lp[1:2, :])).astype(F32).astype(BF16)
    rhs = jnp.concatenate([t_ref[...], ext_ref[...]], axis=1)
    srt = _dot(perm, rhs)
    bits = lax.bitcast_convert_type(srt, jnp.uint32)
    buf[slot, :, :half] = (bits[:, :half] >> 16) | (bits[:, half:2 * half] & jnp.uint32(0xFFFF0000))
    buf[slot, :, half:] = bits[:, 2 * half:]
    scatter(i, slot, True)

    @pl.when(i == n_tiles - 1)
    def _():
        if n_tiles > 1:
            scatter(i - 1, 1 - slot, False)
        scatter(i, slot, False)


def _expert_kernel(be_ref, nv_ref, x_ref, w1_ref, w3_ref, w2_ref, o_ref, w1b, w3b, w2b):
    i = pl.program_id(0)
    valid = i < nv_ref[0]
    half = o_ref.shape[1] // 2

    @pl.when(valid & ((i == 0) | (be_ref[i] != be_ref[jnp.maximum(i - 1, 0)])))
    def _():
        w1b[...] = w1_ref[...].astype(BF16)
        w3b[...] = w3_ref[...].astype(BF16)
        w2b[...] = w2_ref[...].astype(BF16)

    @pl.when(valid)
    def _():
        packed = x_ref[:, :half]
        lo = lax.bitcast_convert_type(packed << 16, F32)
        hi = lax.bitcast_convert_type(packed & jnp.uint32(0xFFFF0000), F32)
        x = jnp.concatenate([lo, hi], axis=1).astype(BF16)
        ext = lax.bitcast_convert_type(x_ref[:, half:], F32)
        g1 = ext[:, 0:1] + ext[:, 1:2] + ext[:, 2:3]
        g2 = ext[:, 3:4] + ext[:, 4:5] + ext[:, 5:6]
        gate = jnp.where(ext[:, 6:7] == be_ref[i].astype(F32), g1, g2)
        hdn = _silu(_dot(x, w1b[...])) * _dot(x, w3b[...])
        y = _dot(hdn.astype(BF16), w2b[...]) * gate
        hi = y.astype(BF16).astype(F32)
        lo = (y - hi).astype(BF16).astype(F32)
        o_ref[...] = lax.bitcast_convert_type(hi, jnp.uint32) | (lax.bitcast_convert_type(lo, jnp.uint32) >> 16)

    @pl.when(jnp.logical_not(valid))
    def _():
        o_ref[...] = jnp.zeros_like(o_ref)


def _combine_kernel(len_ref, src_ref, dst_ref, ys_ref, lp_ref, h_ref, mod_ref, ln_ref, o_ref, buf, sem,
                    *, alpha, first_tile, n_steps):
    step = pl.program_id(0)
    tile = step + first_tile
    slot = step % 2

    def gather(tl, sl, start):
        def body(e, c):
            k = tl * MOE_EXPERTS + e

            def piece(s, d, rows):
                cp = pltpu.make_async_copy(ys_ref.at[pl.ds(d, rows)], buf.at[sl, pl.ds(s, rows)], sem.at[sl])
                cp.start() if start else cp.wait()
            _for_each_piece(len_ref[k], src_ref[k], dst_ref[k], piece)
            return c
        lax.fori_loop(0, MOE_EXPERTS, body, 0)

    @pl.when(step == 0)
    def _():
        buf[...] = jnp.zeros_like(buf)
        gather(tile, slot, True)

    @pl.when(step + 1 < n_steps)
    def _():
        gather(tile + 1, 1 - slot, True)
    gather(tile, slot, False)

    lp = lp_ref[...].astype(jnp.int32)
    j = lax.broadcasted_iota(jnp.int32, (TOK_TILE, SORT_ROWS), 1)
    pick = ((j == lp[:, 0:1]) | (j == lp[:, 1:2])).astype(F32).astype(BF16)
    y = buf[slot]
    hi = lax.bitcast_convert_type(y & jnp.uint32(0xFFFF0000), F32).astype(BF16)
    lo = lax.bitcast_convert_type(y << 16, F32).astype(BF16)
    f = _dot(pick, hi) + _dot(pick, lo)
    o_ref[...] = _layer_norm(alpha * h_ref[...] + mod_ref[5:6, :] * f, ln_ref[0:1, :], ln_ref[1:2, :])


def _moe(lay, t, route, h, mod, ln, w1, w3, w2, layer, alpha, first_tile=0):
    n, d = t.shape
    hid = w1.shape[3]
    n_tiles = n // TOK_TILE
    max_rows = n * MOE_TOPK + n_tiles * MOE_EXPERTS * (RUN_ALIGN - 1)
    n_blocks = -(-max_rows // EXP_BLOCK) + MOE_EXPERTS
    n_rows = n_blocks * EXP_BLOCK
    xs_w = d // 2 + ROUTER_W
    ext, lp, lpt, tables, counts = route
    run_len, run_src, run_dst, zrow, zlen, block_e, n_valid = _plan(tables, counts, n_blocks)

    xs = pl.pallas_call(
        functools.partial(_dispatch_kernel, n_tiles=n_tiles),
        grid_spec=pltpu.PrefetchScalarGridSpec(
            num_scalar_prefetch=5,
            grid=(n_tiles,),
            in_specs=[pl.BlockSpec((TOK_TILE, d), lambda i, *_: (i, 0)),
                      pl.BlockSpec((TOK_TILE, ROUTER_W), lambda i, *_: (i, 0)),
                      pl.BlockSpec((8, TOK_TILE), lambda i, *_: (i, 0))],
            out_specs=pl.BlockSpec(memory_space=pl.ANY),
            scratch_shapes=[pltpu.VMEM((2, SORT_ROWS, xs_w), jnp.uint32),
                            pltpu.VMEM((max(TOK_TILE, EXP_BLOCK), xs_w), jnp.uint32),
                            pltpu.SemaphoreType.DMA((2,)), pltpu.SemaphoreType.DMA],
        ),
        out_shape=jax.ShapeDtypeStruct((n_rows, xs_w), jnp.uint32),
        compiler_params=_params("arbitrary"),
        name="moe_dispatch",
    )(run_len, run_src, run_dst, zrow, zlen, t, ext, lpt)

    blk = lambda i, be, nv: (jnp.minimum(i, nv[0] - 1), 0)
    ys = pl.pallas_call(
        _expert_kernel,
        grid_spec=pltpu.PrefetchScalarGridSpec(
            num_scalar_prefetch=2,
            grid=(n_blocks,),
            in_specs=[
                pl.BlockSpec((EXP_BLOCK, xs_w), blk),
                pl.BlockSpec((None, None, d, hid), lambda i, be, nv: (layer, be[i], 0, 0)),
                pl.BlockSpec((None, None, d, hid), lambda i, be, nv: (layer, be[i], 0, 0)),
                pl.BlockSpec((None, None, hid, d), lambda i, be, nv: (layer, be[i], 0, 0)),
            ],
            out_specs=pl.BlockSpec((EXP_BLOCK, d), lambda i, be, nv: (i, 0)),
            scratch_shapes=[pltpu.VMEM((d, hid), BF16), pltpu.VMEM((d, hid), BF16), pltpu.VMEM((hid, d), BF16)],
        ),
        out_shape=jax.ShapeDtypeStruct((n_rows, d), jnp.uint32),
        compiler_params=_params("arbitrary"),
        name="moe_experts",
    )(block_e, n_valid, xs, w1, w3, w2)

    n_steps = n_tiles - first_tile
    return pl.pallas_call(
        functools.partial(_combine_kernel, alpha=alpha, first_tile=first_tile, n_steps=n_steps),
        grid_spec=pltpu.PrefetchScalarGridSpec(
            num_scalar_prefetch=3,
            grid=(n_steps,),
            in_specs=[pl.BlockSpec(memory_space=pl.ANY),
                      pl.BlockSpec((TOK_TILE, ROUTER_W), lambda i, *_: (i + first_tile, 0)),
                      pl.BlockSpec((TOK_TILE, d), lambda i, *_: (i + first_tile, 0)),
                      pl.BlockSpec((None, 6, d), lambda i, *_: (lay.mod_row(i + first_tile, TOK_TILE), 0, 0)),
                      pl.BlockSpec((2, d), lambda i, *_: (0, 0))],
            out_specs=pl.BlockSpec((TOK_TILE, d), lambda i, *_: (i, 0)),
            scratch_shapes=[pltpu.VMEM((2, SORT_ROWS, d), jnp.uint32), pltpu.SemaphoreType.DMA((2,))],
        ),
        out_shape=jax.ShapeDtypeStruct((n_steps * TOK_TILE, d), F32),
        compiler_params=_params("arbitrary"),
        name="moe_combine",
    )(run_len, run_src, run_dst, ys, lp, h, mod, ln)


def kernel(x, c, ctx, c_ctx, ada_w, ada_b, ln_g, ln_b, ret_w_in, ret_decay, ret_gn_g, ret_w_out, mla_w_down, mla_q_norm, mla_kv_norm, mla_w_uq, mla_w_ukv, mla_w_out, conv_w_in, conv_w, conv_b, conv_w_out, moe_w_group, moe_b_group, moe_w_expert, moe_b_expert, moe_w1, moe_w3, moe_w2):
    b, s, d = x.shape
    depth = ada_w.shape[0]
    lay = _Layout(b, s, ctx.shape[1])
    alpha = (2.0 * depth) ** 0.25

    mod_rows = -(-(b + 1) // 8) * 8
    cc = jnp.concatenate([c, c_ctx[None, :], jnp.zeros((mod_rows - b - 1, d), F32)], axis=0)
    mod_all = _ada(cc, ada_w, ada_b).reshape(depth, mod_rows, 6, d)

    h = (ctx.reshape(lay.n_ctx, d), x.reshape(b * s, d))
    for i in range(depth):
        kind, j = i % N_MIXERS, i // N_MIXERS
        mod = mod_all[i]
        ln1 = jnp.stack([ln_g[i, 0], ln_b[i, 0]])
        ln2 = jnp.stack([ln_g[i, 1], ln_b[i, 1]])
        pad = jnp.zeros((d, ROUTER_W - MOE_GROUPS - MOE_EXPERTS), F32)
        w_router = jnp.concatenate([moe_w_group[i], moe_w_expert[i], pad], axis=1).astype(BF16)
        b_router = jnp.concatenate([moe_b_group[i], moe_b_expert[i], pad[0]]).reshape(1, ROUTER_W)
        if kind == 0:
            a = _retention(lay, h, mod, ret_w_in[j], ret_decay[j], ret_gn_g[j], d)
            h1, t, *route = _post(lay, a, h, mod, ret_w_out[j], ln1, w_router, b_router, alpha)
        elif kind == 1:
            a = _mla(lay, h, mod, mla_w_down[j], mla_q_norm[j], mla_kv_norm[j], mla_w_uq[j], mla_w_ukv[j], d)
            h1, t, *route = _post(lay, a, h, mod, mla_w_out[j], ln1, w_router, b_router, alpha)
        else:
            h1, t, *route = _conv_layer(lay, h, mod, conv_w_in[j], conv_w[j], conv_b[j], conv_w_out[j],
                                        ln1, w_router, b_router, alpha)
        first_tile = lay.n_ctx // TOK_TILE if i == depth - 1 else 0
        h = _moe(lay, t, route, h1, mod, ln2, moe_w1, moe_w3, moe_w2, i, alpha, first_tile)
    return h.reshape(b, s, d)
```

```python
import functools

import jax
import jax.numpy as jnp
from jax import lax
from jax.experimental import pallas as pl
from jax.experimental.pallas import tpu as pltpu

F32 = jnp.float32
BF16 = jnp.bfloat16

GRID_W = 64
LN_EPS = 1e-5
RMS_EPS = 1e-6
ROPE_BASE = 10000.0
N_MIXERS = 3
RET_HEADS = 4
MLA_HEADS = 8
MLA_NOPE = 128
MLA_ROPE = 64
MLA_V = 128
MLA_QK = 256
MOE_GROUPS = 4
MOE_EPG = 8
MOE_EXPERTS = MOE_GROUPS * MOE_EPG
MOE_TOPK = 2
ROUTER_W = 128

CHUNK = 256
TM = 512
TQ = 512
TOK_TILE = 512
EXP_BLOCK = 512
RUN_ALIGN = 8
VMEM_LIMIT = 56 * 1024 * 1024


def _params(*sem):
    return pltpu.CompilerParams(dimension_semantics=sem, vmem_limit_bytes=VMEM_LIMIT)


def _const_spec(shape):
    nd = len(shape)
    return pl.BlockSpec(shape, lambda *_: (0,) * nd, pipeline_mode=pl.Buffered(1))


def _dot(a, b):
    return jnp.dot(a, b, preferred_element_type=F32)


def _dot_nt(a, b):
    return lax.dot_general(a, b, (((1,), (1,)), ((), ())), preferred_element_type=F32)


def _dot_tn(a, b):
    return lax.dot_general(a, b, (((0,), (0,)), ((), ())), preferred_element_type=F32)


def _silu(x):
    return x * jax.nn.sigmoid(x)


def _layer_norm(x, g, b):
    mu = jnp.mean(x, axis=-1, keepdims=True)
    xc = x - mu
    var = jnp.mean(xc * xc, axis=-1, keepdims=True)
    return xc * lax.rsqrt(var + LN_EPS) * g + b


def _rms_norm(x, g):
    return x * lax.rsqrt(jnp.mean(x * x, axis=-1, keepdims=True) + RMS_EPS) * g


def _ada_kernel(c_ref, w_ref, b_ref, o_ref):
    a = _silu(c_ref[...]).astype(BF16)
    o_ref[...] = _dot(a, w_ref[...].astype(BF16)) + b_ref[...]


def _ada(cc, ada_w, ada_b):
    depth, d, n6 = ada_w.shape
    rows = cc.shape[0]
    tn = 1536
    return pl.pallas_call(
        _ada_kernel,
        grid=(depth, n6 // tn),
        in_specs=[
            pl.BlockSpec((rows, d), lambda l, j: (0, 0)),
            pl.BlockSpec((None, d, tn), lambda l, j: (l, 0, j)),
            pl.BlockSpec((None, 1, tn), lambda l, j: (l, 0, j)),
        ],
        out_specs=pl.BlockSpec((None, rows, tn), lambda l, j: (l, 0, j)),
        out_shape=jax.ShapeDtypeStruct((depth, rows, n6), F32),
        compiler_params=_params("parallel", "parallel"),
        name="ada_mod",
    )(cc, ada_w, ada_b.reshape(depth, 1, n6))


class _Layout:
    def __init__(self, batch, seq, ctx_len):
        assert ctx_len == CHUNK and seq % TM == 0 and seq % CHUNK == 0
        self.batch, self.seq, self.ctx = batch, seq, ctx_len
        self.n_ctx = batch * ctx_len
        self.n = self.n_ctx + batch * seq
        assert self.n_ctx % TM == 0 and self.n_ctx % seq == 0 and self.n_ctx % TOK_TILE == 0
        assert self.n_ctx % TQ == 0 and seq % TQ == 0 and seq % TOK_TILE == 0

    def mod_row(self, tile, rows_per_tile):
        ctx_tiles = self.n_ctx // rows_per_tile
        per_batch = self.seq // rows_per_tile
        return jnp.where(tile < ctx_tiles, self.batch, (tile - ctx_tiles) // per_batch)

    def pos_block(self, tile, rows_per_tile):
        ctx_tiles = self.n_ctx // rows_per_tile
        per_batch = self.seq // rows_per_tile
        return jnp.where(tile < ctx_tiles, 0, 1 + (tile - ctx_tiles) % per_batch)


def _row_specs(x, rows, ctx_tiles):
    if not isinstance(x, tuple):
        return (x,), [pl.BlockSpec((rows, x.shape[1]), lambda i: (i, 0))]
    width = x[0].shape[1]
    return x, [pl.BlockSpec((rows, width), lambda i: (jnp.minimum(i, ctx_tiles - 1), 0)),
               pl.BlockSpec((rows, width), lambda i: (jnp.maximum(i - ctx_tiles, 0), 0))]


def _row_tile(refs, ctx_tiles):
    if len(refs) == 1:
        return refs[0][...]
    return jnp.where(pl.program_id(0) < ctx_tiles, refs[0][...], refs[1][...])


def _rope_angles(seq, dim):
    n_rows = seq // GRID_W
    rows = jnp.repeat(jnp.arange(n_rows, dtype=F32), GRID_W)
    cols = jnp.tile(jnp.arange(GRID_W, dtype=F32), n_rows)
    quarter = dim // 4
    inv_freq = ROPE_BASE ** (-jnp.arange(quarter, dtype=F32) / quarter)
    return jnp.concatenate([rows[:, None] * inv_freq, cols[:, None] * inv_freq], axis=-1)


def _with_identity_block(table, fill, rows):
    ident = jnp.full((rows, table.shape[1]), fill, F32)
    return jnp.concatenate([ident, table], axis=0)


def _proj_ret_kernel(*refs, dk, ctx_tiles):
    h_refs = refs[:-14]
    (mod_ref, cos_ref, sin_ref, kdec_ref, wq_ref, wk_ref, wv_ref, wg_ref,
     q_ref, k_ref, kf_ref, kb_ref, v_ref, g_ref) = refs[-14:]
    u = (_row_tile(h_refs, ctx_tiles) * (1.0 + mod_ref[1:2, :]) + mod_ref[0:1, :]).astype(BF16)
    cos = cos_ref[...]
    sin = sin_ref[...]
    half = dk // 2
    yq = _dot(u, wq_ref[...])
    for hd in range(RET_HEADS):
        x1 = yq[:, hd * dk:hd * dk + half]
        x2 = yq[:, hd * dk + half:(hd + 1) * dk]
        q_ref[:, hd * dk:hd * dk + half] = (x1 * cos - x2 * sin).astype(BF16)
        q_ref[:, hd * dk + half:(hd + 1) * dk] = (x1 * sin + x2 * cos).astype(BF16)
    yk = _dot(u, wk_ref[...])
    k_scale = dk ** -0.5
    for hd in range(RET_HEADS):
        x1 = yk[:, hd * dk:hd * dk + half]
        x2 = yk[:, hd * dk + half:(hd + 1) * dk]
        df = kdec_ref[:, hd:hd + 1]
        db = kdec_ref[:, RET_HEADS + hd:RET_HEADS + hd + 1]
        for part, o in ((0, (x1 * cos - x2 * sin) * k_scale), (1, (x1 * sin + x2 * cos) * k_scale)):
            sl = slice(hd * dk + part * half, hd * dk + (part + 1) * half)
            k_ref[:, sl] = o.astype(BF16)
            kf_ref[:, sl] = (o * df).astype(BF16)
            kb_ref[:, sl] = (o * db).astype(BF16)
    v_ref[...] = _dot(u, wv_ref[...]).astype(BF16)
    g_ref[...] = _silu(_dot(u, wg_ref[...]))


def _ret_state_kernel(cdec_ref, kd_ref, v_ref, s_ref, state_ref, *, dk, dv):
    @pl.when(pl.program_id(1) == 0)
    def _():
        state_ref[...] = jnp.zeros_like(state_ref)

    for hd in range(RET_HEADS):
        st = state_ref[hd]
        s_ref[hd] = st.astype(BF16)
        kd = kd_ref[:, hd * dk:(hd + 1) * dk]
        v = v_ref[:, hd * dv:(hd + 1) * dv]
        state_ref[hd] = st * cdec_ref[RET_HEADS + hd] + _dot_tn(kd, v)


def _ret_out_kernel(cdec_ref, q_ref, k_ref, kd_ref, v_ref, sb_ref, mask_ref, qdec_ref, g_ref, gn_ref, o_ref,
                    state_ref, *, dk, dv):
    @pl.when(pl.program_id(1) == 0)
    def _():
        state_ref[...] = jnp.zeros_like(state_ref)

    for hd in range(RET_HEADS):
        q = q_ref[:, hd * dk:(hd + 1) * dk]
        k = k_ref[:, hd * dk:(hd + 1) * dk]
        kd = kd_ref[:, hd * dk:(hd + 1) * dk]
        v = v_ref[:, hd * dv:(hd + 1) * dv]
        p = (_dot_nt(q, k) * mask_ref[hd]).astype(BF16)
        st = state_ref[hd]
        o = (_dot(p, v) + qdec_ref[:, hd:hd + 1] * _dot(q, st.astype(BF16))
             + qdec_ref[:, RET_HEADS + hd:RET_HEADS + hd + 1] * _dot(q, sb_ref[hd]))
        state_ref[hd] = st * cdec_ref[hd] + _dot_tn(kd, v)
        mu = jnp.mean(o, axis=-1, keepdims=True)
        oc = o - mu
        var = jnp.mean(oc * oc, axis=-1, keepdims=True)
        on = oc * lax.rsqrt(var + LN_EPS) * gn_ref[:, hd * dv:(hd + 1) * dv]
        o_ref[:, hd * dv:(hd + 1) * dv] = (g_ref[:, hd * dv:(hd + 1) * dv] * on).astype(BF16)


def _retention(lay, h, mod, w_in, decay_logit, gn_g, d_model):
    n, b, nc = lay.n, lay.batch, lay.seq // CHUNK
    dk = d_model // RET_HEADS
    dv = 2 * dk
    qk, vw = RET_HEADS * dk, RET_HEADS * dv
    w_in = w_in.astype(BF16)
    wq, wk, wv, wg = w_in[:, :qk], w_in[:, qk:2 * qk], w_in[:, 2 * qk:2 * qk + vw], w_in[:, 2 * qk + vw:]

    ang = _rope_angles(lay.seq, dk)
    cos_t = _with_identity_block(jnp.cos(ang), 1.0, TM)
    sin_t = _with_identity_block(jnp.sin(ang), 0.0, TM)

    lg = jax.nn.log_sigmoid(decay_logit.astype(F32))
    idx = jnp.arange(CHUNK, dtype=F32)
    k_pow = jnp.stack([CHUNK - 1.0 - idx, idx])
    q_pow = jnp.stack([idx + 1.0, CHUNK - idx])
    kdec = jnp.exp(k_pow[:, :, None] * lg[:, None, :])
    qdec = jnp.exp(q_pow[:, :, None] * lg[:, None, :])
    kdec = jnp.moveaxis(kdec, 0, 1).reshape(CHUNK, 2 * RET_HEADS)
    qdec = jnp.moveaxis(qdec, 0, 1).reshape(CHUNK, 2 * RET_HEADS)
    cdec = jnp.exp(CHUNK * lg).reshape(2 * RET_HEADS)
    rel = idx[:, None] - idx[None, :]
    rel = jnp.stack([rel, -rel])
    intra = jnp.where(rel[:, None] >= 0, jnp.exp(jnp.maximum(rel[:, None], 0.0) * lg[:, :, None, None]), 0.0)
    kdec_tm = jnp.tile(kdec, (TM // CHUNK, 1))

    n_tiles = n // TM
    row = lambda i: (i, 0)
    ctx_tiles = lay.n_ctx // TM
    h, h_specs = _row_specs(h, TM, ctx_tiles)
    q, k, kf, kb, v, g = pl.pallas_call(
        functools.partial(_proj_ret_kernel, dk=dk, ctx_tiles=ctx_tiles),
        grid=(n_tiles,),
        in_specs=h_specs + [
            pl.BlockSpec((None, 6, d_model), lambda i: (lay.mod_row(i, TM), 0, 0)),
            pl.BlockSpec((TM, dk // 2), lambda i: (lay.pos_block(i, TM), 0)),
            pl.BlockSpec((TM, dk // 2), lambda i: (lay.pos_block(i, TM), 0)),
            _const_spec((TM, 2 * RET_HEADS)),
            _const_spec((d_model, qk)), _const_spec((d_model, qk)),
            _const_spec((d_model, vw)), _const_spec((d_model, vw)),
        ],
        out_specs=[pl.BlockSpec((TM, qk), row)] * 4 + [pl.BlockSpec((TM, vw), row)] * 2,
        out_shape=[jax.ShapeDtypeStruct((n, qk), BF16)] * 4
        + [jax.ShapeDtypeStruct((n, vw), BF16), jax.ShapeDtypeStruct((n, vw), F32)],
        compiler_params=_params("parallel"),
        name="ret_proj",
    )(*h, mod, cos_t, sin_t, kdec_tm, wq, wk, wv, wg)

    def chunk_fwd(bi, c):
        return (jnp.where(c == 0, bi, b + bi * nc + c - 1), 0)

    def chunk_bwd(bi, c):
        return (jnp.where(c == 0, bi, b + bi * nc + nc - c), 0)

    n_chunks = n // CHUNK
    s_b = pl.pallas_call(
        functools.partial(_ret_state_kernel, dk=dk, dv=dv),
        grid=(b, nc + 1),
        in_specs=[pl.BlockSpec(memory_space=pltpu.SMEM),
                  pl.BlockSpec((CHUNK, qk), chunk_bwd), pl.BlockSpec((CHUNK, vw), chunk_bwd)],
        out_specs=pl.BlockSpec((None, RET_HEADS, dk, dv), lambda bi, c: (chunk_bwd(bi, c)[0], 0, 0, 0)),
        out_shape=jax.ShapeDtypeStruct((n_chunks, RET_HEADS, dk, dv), BF16),
        scratch_shapes=[pltpu.VMEM((RET_HEADS, dk, dv), F32)],
        compiler_params=_params("parallel", "arbitrary"),
        name="ret_state_bwd",
    )(cdec, kb, v)

    return pl.pallas_call(
        functools.partial(_ret_out_kernel, dk=dk, dv=dv),
        grid=(b, nc + 1),
        in_specs=[
            pl.BlockSpec(memory_space=pltpu.SMEM),
            pl.BlockSpec((CHUNK, qk), chunk_fwd), pl.BlockSpec((CHUNK, qk), chunk_fwd),
            pl.BlockSpec((CHUNK, qk), chunk_fwd), pl.BlockSpec((CHUNK, vw), chunk_fwd),
            pl.BlockSpec((None, RET_HEADS, dk, dv), lambda bi, c: (chunk_fwd(bi, c)[0], 0, 0, 0)),
            _const_spec((RET_HEADS, CHUNK, CHUNK)), _const_spec((CHUNK, 2 * RET_HEADS)),
            pl.BlockSpec((CHUNK, vw), chunk_fwd), _const_spec((1, vw)),
        ],
        out_specs=pl.BlockSpec((CHUNK, vw), chunk_fwd),
        out_shape=jax.ShapeDtypeStruct((n, vw), BF16),
        scratch_shapes=[pltpu.VMEM((RET_HEADS, dk, dv), F32)],
        compiler_params=_params("parallel", "arbitrary"),
        name="ret_scan_out",
    )(cdec, q, k, kf, v, s_b, intra[0] + intra[1], qdec, g, gn_g.reshape(1, vw).astype(F32))


def _proj_mla_kernel(h_ref, mod_ref, ct_ref, st_ref, wdq_ref, wdkv_ref, wkr_ref, qn_ref, kvn_ref,
                     wuq_ref, wukn_ref, wuv_ref, q_ref, k_ref, vt_ref):
    u = (h_ref[...] * (1.0 + mod_ref[1:2, :]) + mod_ref[0:1, :]).astype(BF16)
    ct = ct_ref[...]
    st = st_ref[...]
    half = MLA_QK // 2

    def rope(x):
        return x * ct + pltpu.roll(x, half // 2, 1) * st

    cq = _rms_norm(_dot(u, wdq_ref[...]), qn_ref[...]).astype(BF16)
    yq = _dot(cq, wuq_ref[...])
    ckv = _rms_norm(_dot(u, wdkv_ref[...]), kvn_ref[...]).astype(BF16)
    kn = _dot(ckv, wukn_ref[...])
    kr = rope(_dot(u, wkr_ref[...])).astype(BF16)
    for hd in range(MLA_HEADS):
        q_ref[:, hd * MLA_QK:hd * MLA_QK + half] = yq[:, hd * MLA_QK:hd * MLA_QK + half].astype(BF16)
        q_ref[:, hd * MLA_QK + half:(hd + 1) * MLA_QK] = rope(
            yq[:, hd * MLA_QK + half:(hd + 1) * MLA_QK]).astype(BF16)
        k_ref[:, hd * MLA_QK:hd * MLA_QK + half] = kn[:, hd * half:(hd + 1) * half].astype(BF16)
        k_ref[:, hd * MLA_QK + half:(hd + 1) * MLA_QK] = kr
    vt_ref[...] = _dot(ckv, wuv_ref[...]).T.astype(BF16)


KV_CHUNK = 512
LOG2_E = 1.4426950408889634


def _kv_chunks(kv_refs, n_kv):
    chunks = []
    row = 0
    for j in range(n_kv):
        keys = kv_refs[2 * j].shape[0]
        step = min(keys, KV_CHUNK)
        for off in range(0, keys, step):
            chunks.append((j, off, row, step))
            row += step
    return chunks


def _attn_kernel(q_ref, *rest, scale, n_kv, pipelined):
    kv_refs = rest[:2 * n_kv]
    chunks = _kv_chunks(kv_refs, n_kv)

    def scores(c, s_ref, m):
        j, off, row, w = c
        s = _dot_nt(kv_refs[2 * j][off:off + w, :], q_ref[...])
        s_ref[row:row + w, :] = s
        cm = jnp.max(s, axis=0, keepdims=True)
        return cm if m is None else jnp.maximum(m, cm)

    def values(c, s_ref, m, acc, den):
        j, off, row, w = c
        p = jnp.exp2((s_ref[row:row + w, :] - m) * (scale * LOG2_E))
        cs = jnp.sum(p, axis=0, keepdims=True)
        pv = _dot(kv_refs[2 * j + 1][:, off:off + w], p.astype(BF16))
        return (pv if acc is None else acc + pv), (cs if den is None else den + cs)

    if not pipelined:
        o_ref, s_ref = rest[-2:]
        m = acc = den = None
        for c in chunks:
            m = scores(c, s_ref, m)
        for c in chunks:
            acc, den = values(c, s_ref, m, acc, den)
        o_ref[...] = (acc / den).T.astype(BF16)
        return

    o_ref, s0_ref, s1_ref, m0_ref, m1_ref = rest[-5:]
    t = pl.program_id(0)
    last = pl.num_programs(0) - 1
    bufs = ((s0_ref, m0_ref), (s1_ref, m1_ref))

    def step(new, old):
        m_prev = old[1][...] if old else None
        m = acc = den = None
        for c in chunks:
            if new:
                m = scores(c, new[0], m)
            if old:
                acc, den = values(c, old[0], m_prev, acc, den)
        if new:
            new[1][...] = m
        if old:
            o_ref[...] = (acc / den).T.astype(BF16)

    @pl.when(t == 0)
    def _():
        step(bufs[0], None)

    for parity in (0, 1):
        @pl.when((t > 0) & (t < last) & (t % 2 == parity))
        def _(parity=parity):
            step(bufs[parity], bufs[1 - parity])

        @pl.when((t == last) & (t % 2 == parity))
        def _(parity=parity):
            step(None, bufs[1 - parity])


def _pad_rope_cols(w):
    z = jnp.zeros((w.shape[0], MLA_ROPE // 2), w.dtype)
    return jnp.concatenate([w[:, :MLA_ROPE // 2], z, w[:, MLA_ROPE // 2:], z], axis=1)


def _mla(lay, h, mod, w_down, q_norm, kv_norm, w_uq, w_ukv, d_model):
    n, b, s = lay.n, lay.batch, lay.seq
    q_lora, kv_lora = q_norm.shape[0], kv_norm.shape[0]
    w_dq = w_down[:, :q_lora].astype(BF16)
    w_dkv = w_down[:, q_lora:q_lora + kv_lora].astype(BF16)
    w_kr = _pad_rope_cols(w_down[:, q_lora + kv_lora:]).astype(BF16)
    w_uq = w_uq.reshape(q_lora, MLA_HEADS, MLA_NOPE + MLA_ROPE)
    w_uq = jnp.concatenate(
        [w_uq[:, :, :MLA_NOPE], jax.vmap(_pad_rope_cols, 1, 1)(w_uq[:, :, MLA_NOPE:])], axis=2)
    w_uq = w_uq.reshape(q_lora, MLA_HEADS * MLA_QK).astype(BF16)
    w_ukv = w_ukv.reshape(kv_lora, MLA_HEADS, MLA_NOPE + MLA_V)
    w_ukn = w_ukv[:, :, :MLA_NOPE].reshape(kv_lora, MLA_HEADS * MLA_NOPE).astype(BF16)
    w_uv = w_ukv[:, :, MLA_NOPE:].reshape(kv_lora, MLA_HEADS * MLA_V).astype(BF16)

    ang = _rope_angles(s, MLA_ROPE)
    z = jnp.zeros_like(ang)
    ct = _with_identity_block(jnp.concatenate([jnp.cos(ang), z, jnp.cos(ang), z], axis=1), 1.0, TM)
    st = _with_identity_block(jnp.concatenate([-jnp.sin(ang), z, jnp.sin(ang), z], axis=1), 0.0, TM)

    row = lambda i: (i, 0)
    qw, vw = MLA_HEADS * MLA_QK, MLA_HEADS * MLA_V
    q, k, vt = pl.pallas_call(
        _proj_mla_kernel,
        grid=(n // TM,),
        in_specs=[
            pl.BlockSpec((TM, d_model), row),
            pl.BlockSpec((None, 6, d_model), lambda i: (lay.mod_row(i, TM), 0, 0)),
            pl.BlockSpec((TM, MLA_QK // 2), lambda i: (lay.pos_block(i, TM), 0)),
            pl.BlockSpec((TM, MLA_QK // 2), lambda i: (lay.pos_block(i, TM), 0)),
            _const_spec((d_model, q_lora)), _const_spec((d_model, kv_lora)), _const_spec((d_model, MLA_QK // 2)),
            _const_spec((1, q_lora)), _const_spec((1, kv_lora)),
            _const_spec((q_lora, qw)), _const_spec((kv_lora, vw)), _const_spec((kv_lora, vw)),
        ],
        out_specs=[pl.BlockSpec((TM, qw), row), pl.BlockSpec((TM, qw), row),
                   pl.BlockSpec((vw, TM), lambda i: (0, i))],
        out_shape=[jax.ShapeDtypeStruct((n, qw), BF16), jax.ShapeDtypeStruct((n, qw), BF16),
                   jax.ShapeDtypeStruct((vw, n), BF16)],
        compiler_params=_params("parallel"),
        name="mla_proj",
    )(h, mod, ct, st, w_dq, w_dkv, w_kr, q_norm.reshape(1, -1).astype(F32), kv_norm.reshape(1, -1).astype(F32),
      w_uq, w_ukn, w_uv)

    scale = (MLA_NOPE + MLA_ROPE) ** -0.5
    ctx_q = lambda bi, hd: (bi, hd)
    att_ctx = pl.pallas_call(
        functools.partial(_attn_kernel, scale=scale, n_kv=1, pipelined=False),
        grid=(b, MLA_HEADS),
        in_specs=[pl.BlockSpec((CHUNK, MLA_QK), ctx_q), pl.BlockSpec((CHUNK, MLA_QK), ctx_q),
                  pl.BlockSpec((MLA_V, CHUNK), lambda bi, hd: (hd, bi))],
        out_specs=pl.BlockSpec((CHUNK, MLA_V), ctx_q),
        out_shape=jax.ShapeDtypeStruct((lay.n_ctx, vw), BF16),
        scratch_shapes=[pltpu.VMEM((CHUNK, CHUNK), F32)],
        compiler_params=_params("parallel", "parallel"),
        name="mla_attn_ctx",
    )(q, k, vt)

    lat_blk = lay.n_ctx // s
    q_tiles = s // TQ
    n_work = b * MLA_HEADS * q_tiles

    def work(f):
        bh, t = f // q_tiles, f % q_tiles
        return bh // MLA_HEADS, bh % MLA_HEADS, t

    def scored(f):
        return work(jnp.minimum(f, n_work - 1))

    def finished(f):
        return work(jnp.maximum(f - 1, 0))

    def q_map(f):
        bi, hd, t = scored(f)
        return (lay.n_ctx // TQ + bi * q_tiles + t, hd)

    def out_map(f):
        bi, hd, t = finished(f)
        return (bi * q_tiles + t, hd)

    att_lat = pl.pallas_call(
        functools.partial(_attn_kernel, scale=scale, n_kv=2, pipelined=True),
        grid=(n_work + 1,),
        in_specs=[pl.BlockSpec((TQ, MLA_QK), q_map),
                  pl.BlockSpec((CHUNK, MLA_QK), lambda f: scored(f)[:2]),
                  pl.BlockSpec((MLA_V, CHUNK), lambda f: (finished(f)[1], finished(f)[0])),
                  pl.BlockSpec((s, MLA_QK), lambda f: (lat_blk + scored(f)[0], scored(f)[1])),
                  pl.BlockSpec((MLA_V, s), lambda f: (finished(f)[1], lat_blk + finished(f)[0]))],
        out_specs=pl.BlockSpec((TQ, MLA_V), out_map),
        out_shape=jax.ShapeDtypeStruct((n - lay.n_ctx, vw), BF16),
        scratch_shapes=[pltpu.VMEM((CHUNK + s, TQ), F32)] * 2 + [pltpu.VMEM((1, TQ), F32)] * 2,
        compiler_params=_params("arbitrary"),
        name="mla_attn_lat",
    )(q, k, vt, k, vt)
    return att_ctx, att_lat


N_TAIL_REFS = 14


def _post_tail(y, h, refs, alpha):
    mod_ref, ln_ref, wr_ref, br_ref, tri_ref, upper_ref, h_out, t_out = refs[0], refs[1], refs[2], refs[3], \
        refs[4], refs[5], refs[6], refs[7]
    h1 = _layer_norm(alpha * h + mod_ref[2:3, :] * y, ln_ref[0:1, :], ln_ref[1:2, :])
    t = h1 * (1.0 + mod_ref[4:5, :]) + mod_ref[3:4, :]
    t = t.astype(BF16)
    h_out[...] = h1
    t_out[...] = t
    _route_tile(_dot(t, wr_ref[...]) + br_ref[...], tri_ref, upper_ref, *refs[8:])


def _post_kernel(*refs, alpha, ctx_tiles, n_a):
    tail = refs[-(N_TAIL_REFS + 1):]
    a_refs, h_refs = refs[:n_a], refs[n_a:-(N_TAIL_REFS + 1)]
    y = _dot(_row_tile(a_refs, ctx_tiles), tail[0][...])
    _post_tail(y, _row_tile(h_refs, ctx_tiles), tail[1:], alpha)


def _tail_specs(n, d, row_tile):
    r = lax.broadcasted_iota(jnp.int32, (TOK_TILE, TOK_TILE), 0)
    c = lax.broadcasted_iota(jnp.int32, (TOK_TILE, TOK_TILE), 1)
    tri = (c < r).astype(BF16)
    r = lax.broadcasted_iota(jnp.int32, (ROUTER_W, ROUTER_W), 0)
    c = lax.broadcasted_iota(jnp.int32, (ROUTER_W, ROUTER_W), 1)
    upper = (r < c).astype(BF16)
    assert row_tile == TOK_TILE
    n_tiles = n // TOK_TILE
    row = lambda i: (i, 0)
    in_specs = [_const_spec((TOK_TILE, TOK_TILE)), _const_spec((ROUTER_W, ROUTER_W))]
    out_specs = [pl.BlockSpec((TOK_TILE, d), row), pl.BlockSpec((TOK_TILE, d), row),
                 pl.BlockSpec((TOK_TILE, ROUTER_W), row), pl.BlockSpec((TOK_TILE, ROUTER_W), row),
                 pl.BlockSpec((8, TOK_TILE), row), pl.BlockSpec((8, ROUTER_W), row),
                 pl.BlockSpec((8, ROUTER_W), lambda i: (0, 0))]
    out_shape = [jax.ShapeDtypeStruct((n, d), F32), jax.ShapeDtypeStruct((n, d), BF16),
                 jax.ShapeDtypeStruct((n, ROUTER_W), BF16), jax.ShapeDtypeStruct((n, ROUTER_W), F32),
                 jax.ShapeDtypeStruct((n_tiles * 8, TOK_TILE), F32),
                 jax.ShapeDtypeStruct((n_tiles * 8, ROUTER_W), F32),
                 jax.ShapeDtypeStruct((8, ROUTER_W), F32)]
    return (tri, upper), in_specs, out_specs, out_shape, [pltpu.VMEM((1, ROUTER_W), F32)]


def _post(lay, a, h, mod, w_out, ln, w_router, b_router, alpha):
    n, d = lay.n, w_out.shape[1]
    row = lambda i: (i, 0)
    ctx_tiles = lay.n_ctx // TM
    a, a_specs = _row_specs(a, TM, ctx_tiles)
    h, h_specs = _row_specs(h, TM, ctx_tiles)
    ka = a[0].shape[1]
    consts, c_specs, out_specs, out_shape, scratch = _tail_specs(n, d, TM)
    return pl.pallas_call(
        functools.partial(_post_kernel, alpha=alpha, ctx_tiles=ctx_tiles, n_a=len(a)),
        grid=(n // TM,),
        in_specs=a_specs + h_specs + [
            _const_spec((ka, d)),
            pl.BlockSpec((None, 6, d), lambda i: (lay.mod_row(i, TM), 0, 0)),
            _const_spec((2, d)), _const_spec((d, ROUTER_W)), _const_spec((1, ROUTER_W)),
        ] + c_specs,
        out_specs=out_specs,
        out_shape=out_shape,
        scratch_shapes=scratch,
        compiler_params=_params("arbitrary"),
        name="post_mixer",
    )(*a, *h, w_out.astype(BF16), mod, ln, w_router, b_router, *consts)


HALO = 8


def _conv_kernel(h_ref, hp_ref, hn_ref, win_ref, cw_ref, cb_ref, w_ref, *tail, alpha, tiles_per_seq, ctx_tiles):
    mod_ref = tail[0]
    i = pl.program_id(0)
    rows, d = h_ref.shape
    h = h_ref[...]
    hx = jnp.concatenate([hp_ref[...], h, hn_ref[...]], axis=0)
    u = (hx * (1.0 + mod_ref[1:2, :]) + mod_ref[0:1, :]).astype(BF16)
    y = _dot(u, win_ref[...])
    gate_b = y[HALO:HALO + rows, :d]
    zx = y[:, d:2 * d] * y[:, 2 * d:]
    z = zx[HALO:HALO + rows]
    is_ctx = i < ctx_tiles
    pos = jnp.where(is_ctx, 0, (i - ctx_tiles) % tiles_per_seq)
    last = jnp.where(is_ctx, 0, tiles_per_seq - 1)
    z_before = jnp.where(pos == 0, 0.0, zx[HALO - 1:HALO])
    z_after = jnp.where(pos == last, 0.0, zx[HALO + rows:HALO + rows + 1])
    r = lax.broadcasted_iota(jnp.int32, (rows, 1), 0)
    in_seq = r & (CHUNK - 1)
    z_prev = jnp.where(r == 0, z_before, pltpu.roll(z, 1, 0))
    z_next = jnp.where(r == rows - 1, z_after, pltpu.roll(z, rows - 1, 0))
    z_prev = jnp.where(jnp.logical_and(is_ctx, in_seq == 0), 0.0, z_prev)
    z_next = jnp.where(jnp.logical_and(is_ctx, in_seq == CHUNK - 1), 0.0, z_next)
    conv = cw_ref[0:1, :] * z_prev + cw_ref[1:2, :] * z + cw_ref[2:3, :] * z_next + cb_ref[...]
    yo = _dot((gate_b * conv).astype(BF16), w_ref[...])
    _post_tail(yo, h, tail, alpha)


def _conv_layer(lay, h, mod, w_in, cw, cb, w_out, ln, w_router, b_router, alpha):
    n, d = h.shape
    n_tiles = n // TM
    per = TM // HALO
    row = lambda i: (i, 0)
    consts, c_specs, out_specs, out_shape, scratch = _tail_specs(n, d, TM)
    return pl.pallas_call(
        functools.partial(_conv_kernel, alpha=alpha, tiles_per_seq=lay.seq // TM, ctx_tiles=lay.n_ctx // TM),
        grid=(n_tiles,),
        in_specs=[
            pl.BlockSpec((TM, d), row),
            pl.BlockSpec((HALO, d), lambda i: (jnp.maximum(i * per - 1, 0), 0)),
            pl.BlockSpec((HALO, d), lambda i: (jnp.minimum((i + 1) * per, n_tiles * per - 1), 0)),
            _const_spec((d, 3 * d)), _const_spec((3, d)), _const_spec((1, d)), _const_spec((d, d)),
            pl.BlockSpec((None, 6, d), lambda i: (lay.mod_row(i, TM), 0, 0)),
            _const_spec((2, d)), _const_spec((d, ROUTER_W)), _const_spec((1, ROUTER_W)),
        ] + c_specs,
        out_specs=out_specs,
        out_shape=out_shape,
        scratch_shapes=scratch,
        compiler_params=_params("arbitrary"),
        name="conv_layer",
    )(h, h, h, w_in.astype(BF16), cw.astype(F32), cb.reshape(1, d).astype(F32), w_out.astype(BF16),
      mod, ln, w_router, b_router, *consts)


def _route_tile(x, tri_ref, upper_ref, ext_ref, lp_ref, lpt_ref, tab_ref, cnt_ref, base_ref):
    @pl.when(pl.program_id(0) == 0)
    def _():
        base_ref[...] = jnp.zeros_like(base_ref)

    lane = lax.broadcasted_iota(jnp.int32, x.shape, 1).astype(F32)
    n_out = -(-(MOE_GROUPS + MOE_EXPERTS) // 8) * 8
    xt = x.T[0:n_out, :]
    out = lax.broadcasted_iota(jnp.int32, xt.shape, 0).astype(F32)

    def softmax(mask):
        m = jnp.max(jnp.where(mask, xt, -jnp.inf), axis=0, keepdims=True)
        e = jnp.where(mask, jnp.exp(xt - m), 0.0)
        return e / jnp.sum(e, axis=0, keepdims=True)

    def top1(prob, mask):
        p = jnp.max(jnp.where(mask, prob, -1.0), axis=0, keepdims=True)
        i = jnp.min(jnp.where(mask & (prob == p), out, float(ROUTER_W)), axis=0, keepdims=True)
        return p, i

    g_mask = out < MOE_GROUPS
    g_p, g_idx = top1(softmax(g_mask), g_mask)
    e_lo = MOE_GROUPS + MOE_EPG * g_idx
    e_mask = (out >= e_lo) & (out < e_lo + MOE_EPG)
    e_prob = softmax(e_mask)
    p1, i1 = top1(e_prob, e_mask)
    p2, i2 = top1(e_prob, e_mask & (out != i1))
    denom = p1 + p2
    r8 = lax.broadcasted_iota(jnp.int32, (8, xt.shape[1]), 0)
    picked = jnp.where(r8 == 0, i1, jnp.where(r8 == 1, i2, jnp.where(r8 == 2, g_p * p1 / denom,
                                                                      jnp.where(r8 == 3, g_p * p2 / denom, 0.0))))
    picked = jnp.concatenate([picked, jnp.zeros((ROUTER_W - 8, xt.shape[1]), F32)], axis=0).T
    i1, i2, gate1, gate2 = picked[:, 0:1], picked[:, 1:2], picked[:, 2:3], picked[:, 3:4]

    sel1 = lane == i1
    sel2 = lane == i2
    cnt = (sel1 | sel2).astype(F32)
    within = _dot(tri_ref[...], cnt.astype(BF16))
    cnt_tile = jnp.sum(cnt, axis=0, keepdims=True)
    ci = jnp.broadcast_to(cnt_tile, (8, ROUTER_W)).astype(jnp.int32)
    ci = (ci + (RUN_ALIGN - 1)) & ~(RUN_ALIGN - 1)
    run_tile = ci[0:1].astype(F32)
    start = (_dot((ci >> 4).astype(F32).astype(BF16), upper_ref[...]) * 16.0
             + _dot((ci & 15).astype(F32).astype(BF16), upper_ref[...]))[0:1]
    where_to = within + start
    lp1 = jnp.sum(jnp.where(sel1, where_to, 0.0), axis=-1, keepdims=True)
    lp2 = jnp.sum(jnp.where(sel2, where_to, 0.0), axis=-1, keepdims=True)

    before = base_ref[...]
    total = before + run_tile
    base_ref[...] = total
    cnt_ref[...] = jnp.broadcast_to(total, cnt_ref.shape)
    row = lax.broadcasted_iota(jnp.int32, tab_ref.shape, 0)
    tab_ref[...] = jnp.where(row == 0, run_tile, jnp.where(row == 1, before, jnp.where(row == 2, start, 0.0)))

    def pieces(g):
        hi = g.astype(BF16).astype(F32)
        mid = (g - hi).astype(BF16).astype(F32)
        return hi, mid, g - hi - mid

    a1, b1, c1 = pieces(gate1)
    a2, b2, c2 = pieces(gate2)
    cols = (a1, b1, c1, a2, b2, c2, i1 - MOE_GROUPS)
    ext = jnp.zeros(x.shape, F32)
    for j, col in enumerate(cols):
        ext = jnp.where(lane == j, col, ext)
    ext_ref[...] = ext.astype(BF16)
    lp = jnp.where(lane == 0, lp1, jnp.where(lane == 1, lp2, 0.0))
    lp_ref[...] = lp
    lpt_ref[...] = lp.T[0:8, :]


def _plan(tables, counts, n_blocks):
    n_tiles = tables.shape[0] // 8
    tables = tables.reshape(n_tiles, 8, ROUTER_W)[:, :3, MOE_GROUPS:MOE_GROUPS + MOE_EXPERTS].astype(jnp.int32)
    counts = counts[0, MOE_GROUPS:MOE_GROUPS + MOE_EXPERTS].astype(jnp.int32)
    padded = (counts + EXP_BLOCK - 1) // EXP_BLOCK * EXP_BLOCK
    pad_ends = jnp.cumsum(padded)
    pad_starts = pad_ends - padded
    run_len = tables[:, 0].reshape(-1)
    run_src = tables[:, 2].reshape(-1)
    run_dst = (tables[:, 1] + pad_starts[None, :]).reshape(-1)
    first_row = jnp.arange(n_blocks, dtype=jnp.int32) * EXP_BLOCK
    block_e = jnp.minimum(jnp.sum(first_row[:, None] >= pad_ends[None, :], axis=1),
                          MOE_EXPERTS - 1).astype(jnp.int32)
    n_valid = (pad_ends[-1] // EXP_BLOCK).astype(jnp.int32).reshape(1)
    clear_row = jnp.concatenate([pad_starts + counts, pad_ends[-1:]])
    clear_len = jnp.concatenate([padded - counts, n_blocks - n_valid])
    return run_len, run_src, run_dst, clear_row, clear_len, block_e, n_valid


RUN_BITS = tuple(b for b in (1 << k for k in range(TOK_TILE.bit_length() - 1, -1, -1)) if b >= RUN_ALIGN)
RARE_BITS = 4
SORT_ROWS = MOE_TOPK * TOK_TILE + MOE_EXPERTS * RUN_ALIGN


def _for_each_piece(length, src, dst, fn):
    def pieces(bits):
        for bit in bits:
            above = length & ~(2 * bit - 1)

            @pl.when((length & bit) != 0)
            def _(bit=bit, above=above):
                fn(pl.multiple_of(src + above, RUN_ALIGN), pl.multiple_of(dst + above, RUN_ALIGN), bit)

    @pl.when(length >= RUN_BITS[RARE_BITS - 1])
    def _():
        pieces(RUN_BITS[:RARE_BITS])
    pieces(RUN_BITS[RARE_BITS:])


def _dispatch_kernel(len_ref, src_ref, dst_ref, zrow_ref, zlen_ref, t_ref, ext_ref, lpt_ref, xs_ref,
                     buf, zeros, sem, zsem, *, n_tiles):
    i = pl.program_id(0)
    half = t_ref.shape[1] // 2

    @pl.when(i == 0)
    def _():
        zeros[...] = jnp.zeros_like(zeros)

        def clear(e, c):
            _for_each_piece(zlen_ref[e], 0, zrow_ref[e], lambda s, d, rows: pltpu.make_async_copy(
                zeros.at[pl.ds(0, rows)], xs_ref.at[pl.ds(d, rows)], zsem).start())
            return c
        lax.fori_loop(0, MOE_EXPERTS, clear, 0)

        def drain(e, c):
            _for_each_piece(zlen_ref[e], 0, zrow_ref[e], lambda s, d, rows: pltpu.make_async_copy(
                zeros.at[pl.ds(0, rows)], xs_ref.at[pl.ds(d, rows)], zsem).wait())
            return c
        lax.fori_loop(0, MOE_EXPERTS, drain, 0)

        def tail_copy(bk):
            row = pl.multiple_of(zrow_ref[MOE_EXPERTS] + bk * EXP_BLOCK, EXP_BLOCK)
            return pltpu.make_async_copy(zeros.at[pl.ds(0, EXP_BLOCK)], xs_ref.at[pl.ds(row, EXP_BLOCK)], zsem)

        def clear_tail(bk, c):
            tail_copy(bk).start()
            return c
        lax.fori_loop(0, zlen_ref[MOE_EXPERTS], clear_tail, 0)

        def drain_tail(bk, c):
            tail_copy(bk).wait()
            return c
        lax.fori_loop(0, zlen_ref[MOE_EXPERTS], drain_tail, 0)

    slot = i % 2

    def scatter(tile, sl, start):
        def body(e, c):
            k = tile * MOE_EXPERTS + e

            def piece(s, d, rows):
                cp = pltpu.make_async_copy(buf.at[sl, pl.ds(s, rows)], xs_ref.at[pl.ds(d, rows)], sem.at[sl])
                cp.start() if start else cp.wait()
            _for_each_piece(len_ref[k], src_ref[k], dst_ref[k], piece)
            return c
        lax.fori_loop(0, MOE_EXPERTS, body, 0)

    @pl.when(i >= 2)
    def _():
        scatter(i - 2, slot, False)

    lp = lpt_ref[...].astype(jnp.int32)
    j = lax.broadcasted_iota(jnp.int32, (SORT_ROWS, TOK_TILE), 0)
    perm = ((j == lp[0:1, :]) | (j == lp[1:2, :])).astype(F32).astype(BF16)
    rhs = jnp.concatenate([t_ref[...], ext_ref[...]], axis=1)
    srt = _dot(perm, rhs)
    bits = lax.bitcast_convert_type(srt, jnp.uint32)
    buf[slot, :, :half] = (bits[:, :half] >> 16) | (bits[:, half:2 * half] & jnp.uint32(0xFFFF0000))
    buf[slot, :, half:] = bits[:, 2 * half:]
    scatter(i, slot, True)

    @pl.when(i == n_tiles - 1)
    def _():
        if n_tiles > 1:
            scatter(i - 1, 1 - slot, False)
        scatter(i, slot, False)


def _expert_kernel(be_ref, nv_ref, x_ref, w1_ref, w3_ref, w2_ref, o_ref, w1b, w3b, w2b):
    i = pl.program_id(0)
    valid = i < nv_ref[0]
    half = o_ref.shape[1] // 2

    @pl.when(valid & ((i == 0) | (be_ref[i] != be_ref[jnp.maximum(i - 1, 0)])))
    def _():
        w1b[...] = w1_ref[...].astype(BF16)
        w3b[...] = w3_ref[...].astype(BF16)
        w2b[...] = w2_ref[...].astype(BF16)

    @pl.when(valid)
    def _():
        packed = x_ref[:, :half]
        lo = lax.bitcast_convert_type(packed << 16, F32)
        hi = lax.bitcast_convert_type(packed & jnp.uint32(0xFFFF0000), F32)
        x = jnp.concatenate([lo, hi], axis=1).astype(BF16)
        ext = lax.bitcast_convert_type(x_ref[:, half:], F32)
        g1 = ext[:, 0:1] + ext[:, 1:2] + ext[:, 2:3]
        g2 = ext[:, 3:4] + ext[:, 4:5] + ext[:, 5:6]
        gate = jnp.where(ext[:, 6:7] == be_ref[i].astype(F32), g1, g2)
        hdn = _silu(_dot(x, w1b[...])) * _dot(x, w3b[...])
        y = _dot(hdn.astype(BF16), w2b[...]) * gate
        hi = y.astype(BF16).astype(F32)
        lo = (y - hi).astype(BF16).astype(F32)
        o_ref[...] = lax.bitcast_convert_type(hi, jnp.uint32) | (lax.bitcast_convert_type(lo, jnp.uint32) >> 16)

    @pl.when(jnp.logical_not(valid))
    def _():
        o_ref[...] = jnp.zeros_like(o_ref)


def _combine_kernel(len_ref, src_ref, dst_ref, ys_ref, lp_ref, h_ref, mod_ref, ln_ref, o_ref, buf, sem,
                    *, alpha, first_tile, n_steps):
    step = pl.program_id(0)
    tile = step + first_tile
    slot = step % 2

    def gather(tl, sl, start):
        def body(e, c):
            k = tl * MOE_EXPERTS + e

            def piece(s, d, rows):
                cp = pltpu.make_async_copy(ys_ref.at[pl.ds(d, rows)], buf.at[sl, pl.ds(s, rows)], sem.at[sl])
                cp.start() if start else cp.wait()
            _for_each_piece(len_ref[k], src_ref[k], dst_ref[k], piece)
            return c
        lax.fori_loop(0, MOE_EXPERTS, body, 0)

    @pl.when(step == 0)
    def _():
        buf[...] = jnp.zeros_like(buf)
        gather(tile, slot, True)

    @pl.when(step + 1 < n_steps)
    def _():
        gather(tile + 1, 1 - slot, True)
    gather(tile, slot, False)

    lp = lp_ref[...].astype(jnp.int32)
    j = lax.broadcasted_iota(jnp.int32, (TOK_TILE, SORT_ROWS), 1)
    pick = ((j == lp[:, 0:1]) | (j == lp[:, 1:2])).astype(F32).astype(BF16)
    y = buf[slot]
    hi = lax.bitcast_convert_type(y & jnp.uint32(0xFFFF0000), F32).astype(BF16)
    lo = lax.bitcast_convert_type(y << 16, F32).astype(BF16)
    f = _dot(pick, hi) + _dot(pick, lo)
    o_ref[...] = _layer_norm(alpha * h_ref[...] + mod_ref[5:6, :] * f, ln_ref[0:1, :], ln_ref[1:2, :])


def _moe(lay, t, route, h, mod, ln, w1, w3, w2, layer, alpha, first_tile=0):
    n, d = t.shape
    hid = w1.shape[3]
    n_tiles = n // TOK_TILE
    max_rows = n * MOE_TOPK + n_tiles * MOE_EXPERTS * (RUN_ALIGN - 1)
    n_blocks = -(-max_rows // EXP_BLOCK) + MOE_EXPERTS
    n_rows = n_blocks * EXP_BLOCK
    xs_w = d // 2 + ROUTER_W
    ext, lp, lpt, tables, counts = route
    run_len, run_src, run_dst, zrow, zlen, block_e, n_valid = _plan(tables, counts, n_blocks)

    xs = pl.pallas_call(
        functools.partial(_dispatch_kernel, n_tiles=n_tiles),
        grid_spec=pltpu.PrefetchScalarGridSpec(
            num_scalar_prefetch=5,
            grid=(n_tiles,),
            in_specs=[pl.BlockSpec((TOK_TILE, d), lambda i, *_: (i, 0)),
                      pl.BlockSpec((TOK_TILE, ROUTER_W), lambda i, *_: (i, 0)),
                      pl.BlockSpec((8, TOK_TILE), lambda i, *_: (i, 0))],
            out_specs=pl.BlockSpec(memory_space=pl.ANY),
            scratch_shapes=[pltpu.VMEM((2, SORT_ROWS, xs_w), jnp.uint32),
                            pltpu.VMEM((max(TOK_TILE, EXP_BLOCK), xs_w), jnp.uint32),
                            pltpu.SemaphoreType.DMA((2,)), pltpu.SemaphoreType.DMA],
        ),
        out_shape=jax.ShapeDtypeStruct((n_rows, xs_w), jnp.uint32),
        compiler_params=_params("arbitrary"),
        name="moe_dispatch",
    )(run_len, run_src, run_dst, zrow, zlen, t, ext, lpt)

    blk = lambda i, be, nv: (jnp.minimum(i, nv[0] - 1), 0)
    ys = pl.pallas_call(
        _expert_kernel,
        grid_spec=pltpu.PrefetchScalarGridSpec(
            num_scalar_prefetch=2,
            grid=(n_blocks,),
            in_specs=[
                pl.BlockSpec((EXP_BLOCK, xs_w), blk),
                pl.BlockSpec((None, None, d, hid), lambda i, be, nv: (layer, be[i], 0, 0)),
                pl.BlockSpec((None, None, d, hid), lambda i, be, nv: (layer, be[i], 0, 0)),
                pl.BlockSpec((None, None, hid, d), lambda i, be, nv: (layer, be[i], 0, 0)),
            ],
            out_specs=pl.BlockSpec((EXP_BLOCK, d), lambda i, be, nv: (i, 0)),
            scratch_shapes=[pltpu.VMEM((d, hid), BF16), pltpu.VMEM((d, hid), BF16), pltpu.VMEM((hid, d), BF16)],
        ),
        out_shape=jax.ShapeDtypeStruct((n_rows, d), jnp.uint32),
        compiler_params=_params("arbitrary"),
        name="moe_experts",
    )(block_e, n_valid, xs, w1, w3, w2)

    n_steps = n_tiles - first_tile
    return pl.pallas_call(
        functools.partial(_combine_kernel, alpha=alpha, first_tile=first_tile, n_steps=n_steps),
        grid_spec=pltpu.PrefetchScalarGridSpec(
            num_scalar_prefetch=3,
            grid=(n_steps,),
            in_specs=[pl.BlockSpec(memory_space=pl.ANY),
                      pl.BlockSpec((TOK_TILE, ROUTER_W), lambda i, *_: (i + first_tile, 0)),
                      pl.BlockSpec((TOK_TILE, d), lambda i, *_: (i + first_tile, 0)),
                      pl.BlockSpec((None, 6, d), lambda i, *_: (lay.mod_row(i + first_tile, TOK_TILE), 0, 0)),
                      pl.BlockSpec((2, d), lambda i, *_: (0, 0))],
            out_specs=pl.BlockSpec((TOK_TILE, d), lambda i, *_: (i, 0)),
            scratch_shapes=[pltpu.VMEM((2, SORT_ROWS, d), jnp.uint32), pltpu.SemaphoreType.DMA((2,))],
        ),
        out_shape=jax.ShapeDtypeStruct((n_steps * TOK_TILE, d), F32),
        compiler_params=_params("arbitrary"),
        name="moe_combine",
    )(run_len, run_src, run_dst, ys, lp, h, mod, ln)


def kernel(x, c, ctx, c_ctx, ada_w, ada_b, ln_g, ln_b, ret_w_in, ret_decay, ret_gn_g, ret_w_out, mla_w_down, mla_q_norm, mla_kv_norm, mla_w_uq, mla_w_ukv, mla_w_out, conv_w_in, conv_w, conv_b, conv_w_out, moe_w_group, moe_b_group, moe_w_expert, moe_b_expert, moe_w1, moe_w3, moe_w2):
    b, s, d = x.shape
    depth = ada_w.shape[0]
    lay = _Layout(b, s, ctx.shape[1])
    alpha = (2.0 * depth) ** 0.25

    mod_rows = -(-(b + 1) // 8) * 8
    cc = jnp.concatenate([c, c_ctx[None, :], jnp.zeros((mod_rows - b - 1, d), F32)], axis=0)
    mod_all = _ada(cc, ada_w, ada_b).reshape(depth, mod_rows, 6, d)

    h = (ctx.reshape(lay.n_ctx, d), x.reshape(b * s, d))
    for i in range(depth):
        kind, j = i % N_MIXERS, i // N_MIXERS
        mod = mod_all[i]
        ln1 = jnp.stack([ln_g[i, 0], ln_b[i, 0]])
        ln2 = jnp.stack([ln_g[i, 1], ln_b[i, 1]])
        pad = jnp.zeros((d, ROUTER_W - MOE_GROUPS - MOE_EXPERTS), F32)
        w_router = jnp.concatenate([moe_w_group[i], moe_w_expert[i], pad], axis=1).astype(BF16)
        b_router = jnp.concatenate([moe_b_group[i], moe_b_expert[i], pad[0]]).reshape(1, ROUTER_W)
        if kind == 0:
            a = _retention(lay, h, mod, ret_w_in[j], ret_decay[j], ret_gn_g[j], d)
            h1, t, *route = _post(lay, a, h, mod, ret_w_out[j], ln1, w_router, b_router, alpha)
        elif kind == 1:
            a = _mla(lay, h, mod, mla_w_down[j], mla_q_norm[j], mla_kv_norm[j], mla_w_uq[j], mla_w_ukv[j], d)
            h1, t, *route = _post(lay, a, h, mod, mla_w_out[j], ln1, w_router, b_router, alpha)
        else:
            h1, t, *route = _conv_layer(lay, h, mod, conv_w_in[j], conv_w[j], conv_b[j], conv_w_out[j],
                                        ln1, w_router, b_router, alpha)
        first_tile = lay.n_ctx // TOK_TILE if i == depth - 1 else 0
        h = _moe(lay, t, route, h1, mod, ln2, moe_w1, moe_w3, moe_w2, i, alpha, first_tile)
    return h.reshape(b, s, d)
```
